```python
import jax
import jax.numpy as jnp
from jax import lax
import numpy as np


D_MODEL = 2048
BATCH = 4
SEQ = 4096
DEPTH = 2

BRANCH_WIDTH = D_MODEL // 2
N_BRANCHES = 4
PLE_DIM = 256
EPS = 1e-6
FOX_HEAD_DIM = 128
FOX_HEADS = BRANCH_WIDTH // FOX_HEAD_DIM
FOX_BLOCK = 128
FOX_FORGET_BIAS = 2.0
POOL_WINDOWS = (2, 4, 8, 16)
POOL_GROUP = BRANCH_WIDTH // len(POOL_WINDOWS)
SGU_GROUPS = 4
SGU_CHUNK = 128
SGU_GROUP_WIDTH = BRANCH_WIDTH // SGU_GROUPS
HGRN_KEY_DIM = 128
HGRN_VAL_DIM = 128
HGRN_HEADS = BRANCH_WIDTH // HGRN_VAL_DIM
HGRN_CHUNK = 64
N_GROUPS = 4
EXPERTS_PER_GROUP = 8
N_EXPERTS = N_GROUPS * EXPERTS_PER_GROUP
TOP_K = 2
EXPERT_HIDDEN = D_MODEL // 2
MOE_BLOCK = 128
FOX_QKV_END = 3 * BRANCH_WIDTH
FOX_F_END = FOX_QKV_END + FOX_HEADS
POOL_END = FOX_F_END + BRANCH_WIDTH
SGU_END = POOL_END + 2 * BRANCH_WIDTH
HGRN_END = SGU_END + 2 * HGRN_HEADS * HGRN_KEY_DIM + 2 * HGRN_HEADS * HGRN_VAL_DIM
IN_COLS = HGRN_END + N_BRANCHES * D_MODEL
SPLIT_POINTS = (FOX_QKV_END, FOX_F_END, POOL_END, SGU_END, HGRN_END)

kernel_name = 'hybrid_fox_pool_sgu_hgrn2_hmoe_ple'


def rms_norm(x, w):
    x32 = x.astype(jnp.float32)
    y = x32 * lax.rsqrt(jnp.mean(x32 * x32, axis=-1, keepdims=True) + EPS)
    return (y * w.astype(jnp.float32)).astype(x.dtype)


def layer_norm(x, w):
    x32 = x.astype(jnp.float32)
    xc = x32 - jnp.mean(x32, axis=-1, keepdims=True)
    y = xc * lax.rsqrt(jnp.mean(xc * xc, axis=-1, keepdims=True) + EPS)
    return (y * w.astype(jnp.float32)).astype(x.dtype)


def fox_attention(qkv, f_logit, f_bias):
    B, S, _ = qkv.shape
    qkv = qkv.reshape(B, S, 3, FOX_HEADS, FOX_HEAD_DIM)
    q, k, v = qkv[:, :, 0], qkv[:, :, 1], qkv[:, :, 2]
    log_f = jax.nn.log_sigmoid(f_logit.astype(jnp.float32) + f_bias.astype(jnp.float32))
    c = jnp.cumsum(log_f, axis=1).transpose(0, 2, 1)
    scale = FOX_HEAD_DIM ** -0.5
    outs = []
    for blk in range(S // FOX_BLOCK):
        lo = blk * FOX_BLOCK
        hi = lo + FOX_BLOCK
        s = jnp.einsum('bqhd,bkhd->bhqk', q[:, lo:hi], k[:, :hi]).astype(jnp.float32) * scale
        s = s + (c[:, :, lo:hi, None] - c[:, :, None, :hi])
        mask = jnp.arange(hi)[None, :] <= jnp.arange(lo, hi)[:, None]
        s = jnp.where(mask, s, -jnp.inf)
        w = jax.nn.softmax(s, axis=-1).astype(v.dtype)
        outs.append(jnp.einsum('bhqk,bkhd->bqhd', w, v[:, :hi]))
    return jnp.concatenate(outs, axis=1).reshape(B, S, FOX_HEADS * FOX_HEAD_DIM)


def multiscale_pool(u, pool_w, pool_scale):
    B, S, _ = u.shape
    u32 = u.astype(jnp.float32)
    cs = jnp.cumsum(u32, axis=1)
    cs_pad = jnp.pad(cs, ((0, 0), (1, 0), (0, 0)))
    outs = []
    for g, win in enumerate(POOL_WINDOWS):
        sl = slice(g * POOL_GROUP, (g + 1) * POOL_GROUP)
        upper = cs[:, :, sl]
        lower = jnp.pad(cs_pad[:, :S + 1 - win, sl], ((0, 0), (win - 1, 0), (0, 0)))
        count = jnp.minimum(jnp.arange(1, S + 1), win).astype(jnp.float32)[None, :, None]
        d = ((upper - lower) / count - u32[:, :, sl]).astype(u.dtype)
        outs.append(d @ pool_w[g])
    return jnp.concatenate(outs, axis=-1) * pool_scale


def spatial_gating(z, norm_w, w_s, b_s):
    B, S, _ = z.shape
    z = jax.nn.gelu(z)
    u, v = z[..., :BRANCH_WIDTH], z[..., BRANCH_WIDTH:]
    v = layer_norm(v, norm_w)
    n = S // SGU_CHUNK
    v = v.reshape(B, n, SGU_CHUNK, SGU_GROUPS, SGU_GROUP_WIDTH)
    mask = jnp.tril(jnp.ones((SGU_CHUNK, SGU_CHUNK), dtype=bool))
    w = jnp.where(mask, w_s, 0)
    sv = jnp.einsum('gts,bnsgc->bntgc', w, v) + b_s.T[:, :, None]
    return u * sv.reshape(B, S, BRANCH_WIDTH)


def hgrn2(z, lb, norm_w):
    B, S, _ = z.shape
    H, K, V, C = HGRN_HEADS, HGRN_KEY_DIM, HGRN_VAL_DIM, HGRN_CHUNK
    q = z[..., :H * K].astype(jnp.float32).reshape(B, S, H, K)
    fl = z[..., H * K:2 * H * K].astype(jnp.float32).reshape(B, S, H, K)
    vi = z[..., 2 * H * K:2 * H * K + H * V].astype(jnp.float32).reshape(B, S, H, V)
    g = z[..., 2 * H * K + H * V:].astype(jnp.float32).reshape(B, S, H, V)
    lb = lb.reshape(H, K)
    log_f = jnp.log(lb + (1.0 - lb) * jax.nn.sigmoid(fl))
    k = (1.0 - lb) * jax.nn.sigmoid(-fl)
    n = S // C

    def to_chunks(a):
        return a.reshape(B, n, C, H, a.shape[-1]).transpose(1, 0, 3, 2, 4)

    mask = jnp.tril(jnp.ones((C, C), dtype=bool))[:, :, None]

    def step(state, inp):
        qc, kc, vc, lf = inp
        b = jnp.cumsum(lf, axis=2)
        o = jnp.einsum('bhtk,bhkv->bhtv', qc * jnp.exp(b), state)
        diff = b[:, :, :, None, :] - b[:, :, None, :, :]
        decay = jnp.exp(jnp.where(mask, diff, -jnp.inf))
        a = jnp.einsum('bhtk,bhsk,bhtsk->bhts', qc, kc, decay)
        o = o + jnp.einsum('bhts,bhsv->bhtv', a, vc)
        b_end = b[:, :, -1:, :]
        state = jnp.exp(b_end[:, :, 0, :])[..., None] * state + jnp.einsum('bhsk,bhsv->bhkv', kc * jnp.exp(b_end - b), vc)
        return state, o

    state0 = jnp.zeros((B, H, K, V), jnp.float32)
    _, o = lax.scan(step, state0, (to_chunks(q), to_chunks(k), to_chunks(vi), to_chunks(log_f)))
    o = o.transpose(1, 0, 3, 2, 4).reshape(B, S, H, V)
    o = o * lax.rsqrt(jnp.mean(o * o, axis=-1, keepdims=True) + EPS) * norm_w.astype(jnp.float32).reshape(H, V)
    o = o * jax.nn.silu(g)
    return o.reshape(B, S, H * V).astype(z.dtype)


def hierarchical_moe(xn, wg, bg, we, be, w_gate, w_up, w_down):
    B, S, D = xn.shape
    T = B * S
    A = T * TOP_K
    xf = xn.reshape(T, D)
    x32 = xf.astype(jnp.float32)
    group_logits = x32 @ wg.astype(jnp.float32) + bg.astype(jnp.float32)
    group_probs = jax.nn.softmax(group_logits, axis=-1)
    g = jnp.argmax(group_logits, axis=-1).astype(jnp.int32)
    p_group = jnp.take_along_axis(group_probs, g[:, None], axis=1)[:, 0]
    exp_logits = (x32 @ we.astype(jnp.float32) + be.astype(jnp.float32)).reshape(T, N_GROUPS, EXPERTS_PER_GROUP)
    in_group = jnp.take_along_axis(exp_logits, g[:, None, None], axis=1)[:, 0, :]
    top_vals, top_idx = lax.top_k(in_group, TOP_K)
    p_in = jax.nn.softmax(top_vals, axis=-1)
    weights = (p_group[:, None] * p_in).astype(xn.dtype)
    expert_ids = (g[:, None] * EXPERTS_PER_GROUP + top_idx.astype(jnp.int32)).reshape(A)
    token_ids = jnp.repeat(jnp.arange(T, dtype=jnp.int32), TOP_K)
    flat_w = weights.reshape(A)
    order = jnp.argsort(expert_ids)
    sorted_e = expert_ids[order]
    counts = jnp.bincount(expert_ids, length=N_EXPERTS).astype(jnp.int32)
    padded = (counts + MOE_BLOCK - 1) // MOE_BLOCK * MOE_BLOCK
    pad_end = jnp.cumsum(padded)
    pad_start = pad_end - padded
    start = jnp.cumsum(counts) - counts
    dest = pad_start[sorted_e] + jnp.arange(A, dtype=jnp.int32) - start[sorted_e]
    P = (A + MOE_BLOCK - 1) // MOE_BLOCK * MOE_BLOCK + N_EXPERTS * MOE_BLOCK
    NB = P // MOE_BLOCK
    buf_tok = jnp.zeros((P,), jnp.int32).at[dest].set(token_ids[order])
    buf_w = jnp.zeros((P,), xn.dtype).at[dest].set(flat_w[order])
    block_e = jnp.clip(jnp.searchsorted(pad_end, jnp.arange(NB, dtype=jnp.int32) * MOE_BLOCK, side='right'), 0, N_EXPERTS - 1)
    xs = xf[buf_tok].reshape(NB, MOE_BLOCK, D)

    def run_block(args):
        xb, e = args
        hdn = jax.nn.silu(xb @ w_gate[e]) * (xb @ w_up[e])
        return hdn @ w_down[e]

    ys = lax.map(run_block, (xs, block_e)).reshape(P, D)
    out = jnp.zeros((T, D), xn.dtype).at[buf_tok].add(ys * buf_w[:, None])
    return out.reshape(B, S, D)


def setup_inputs(seed: int = 0) -> dict:
    key = jax.random.key(seed)
    ks = jax.random.split(key, 26)
    f32 = jnp.float32

    def nrm(k, shape, scale):
        return jax.random.normal(k, shape, f32) * scale

    def gain(k, shape):
        return 1.0 + 0.05 * jax.random.normal(k, shape, f32)

    return {
        'x': nrm(ks[0], (BATCH, SEQ, D_MODEL), 1.0),
        'p': nrm(ks[1], (DEPTH, BATCH, SEQ, PLE_DIM), 1.0),
        'mix_norm_w': gain(ks[2], (DEPTH, D_MODEL)),
        'w_in': nrm(ks[3], (DEPTH, D_MODEL, IN_COLS), D_MODEL ** -0.5),
        'fox_f_bias': FOX_FORGET_BIAS + nrm(ks[4], (DEPTH, FOX_HEADS), 0.5),
        'pool_w': nrm(ks[5], (DEPTH, len(POOL_WINDOWS), POOL_GROUP, POOL_GROUP), POOL_GROUP ** -0.5),
        'pool_scale': gain(ks[6], (DEPTH, BRANCH_WIDTH)),
        'sgu_norm_w': gain(ks[7], (DEPTH, BRANCH_WIDTH)),
        'sgu_w': nrm(ks[8], (DEPTH, SGU_GROUPS, SGU_CHUNK, SGU_CHUNK), SGU_CHUNK ** -0.5),
        'sgu_b': gain(ks[9], (DEPTH, SGU_GROUPS, SGU_CHUNK)),
        'hgrn_lb_logits': nrm(ks[10], (DEPTH, HGRN_HEADS * HGRN_KEY_DIM), 1.0),
        'hgrn_norm_w': gain(ks[11], (DEPTH, HGRN_HEADS * HGRN_VAL_DIM)),
        'branch_proj': nrm(ks[12], (DEPTH, N_BRANCHES, BRANCH_WIDTH, D_MODEL), BRANCH_WIDTH ** -0.5),
        'w_out': nrm(ks[13], (DEPTH, D_MODEL, D_MODEL), D_MODEL ** -0.5),
        'ffn_norm_w': gain(ks[14], (DEPTH, D_MODEL)),
        'router_group_w': nrm(ks[15], (DEPTH, D_MODEL, N_GROUPS), D_MODEL ** -0.5),
        'router_group_b': nrm(ks[16], (DEPTH, N_GROUPS), 0.01),
        'router_expert_w': nrm(ks[17], (DEPTH, D_MODEL, N_EXPERTS), D_MODEL ** -0.5),
        'router_expert_b': nrm(ks[18], (DEPTH, N_EXPERTS), 0.01),
        'expert_w_gate': nrm(ks[19], (DEPTH, N_EXPERTS, D_MODEL, EXPERT_HIDDEN), D_MODEL ** -0.5),
        'expert_w_up': nrm(ks[20], (DEPTH, N_EXPERTS, D_MODEL, EXPERT_HIDDEN), D_MODEL ** -0.5),
        'expert_w_down': nrm(ks[21], (DEPTH, N_EXPERTS, EXPERT_HIDDEN, D_MODEL), EXPERT_HIDDEN ** -0.5),
        'ple_norm_w': gain(ks[22], (DEPTH, D_MODEL)),
        'ple_gate_w': nrm(ks[23], (DEPTH, D_MODEL, D_MODEL), D_MODEL ** -0.5),
        'ple_proj_w': nrm(ks[24], (DEPTH, PLE_DIM, D_MODEL), PLE_DIM ** -0.5),
        'final_norm_w': gain(ks[25], (D_MODEL,)),
    }


def reference(x, p, mix_norm_w, w_in, fox_f_bias, pool_w, pool_scale, sgu_norm_w, sgu_w, sgu_b,
              hgrn_lb_logits, hgrn_norm_w, branch_proj, w_out, ffn_norm_w, router_group_w,
              router_group_b, router_expert_w, router_expert_b, expert_w_gate, expert_w_up,
              expert_w_down, ple_norm_w, ple_gate_w, ple_proj_w, final_norm_w):
    B, S, D = x.shape
    lb_p = jax.nn.softmax(hgrn_lb_logits.astype(jnp.float32), axis=0)
    lower_bounds = jnp.cumsum(lb_p, axis=0) - lb_p[0:1]
    h = x
    for i in range(DEPTH):
        n = rms_norm(h, mix_norm_w[i])
        z = n @ w_in[i]
        z_qkv, z_f, z_pool, z_sgu, z_hgrn, z_gate = jnp.split(z, SPLIT_POINTS, axis=-1)
        branches = (
            fox_attention(z_qkv, z_f, fox_f_bias[i]),
            multiscale_pool(z_pool, pool_w[i], pool_scale[i]),
            spatial_gating(z_sgu, sgu_norm_w[i], sgu_w[i], sgu_b[i]),
            hgrn2(z_hgrn, lower_bounds[i], hgrn_norm_w[i]),
        )
        merged = jnp.zeros_like(h)
        for bi in range(N_BRANCHES):
            gate = jax.nn.sigmoid(z_gate[..., bi * D:(bi + 1) * D].astype(jnp.float32)).astype(h.dtype)
            merged = merged + gate * (branches[bi] @ branch_proj[i, bi])
        h = h + merged @ w_out[i]
        h = h + hierarchical_moe(rms_norm(h, ffn_norm_w[i]), router_group_w[i], router_group_b[i],
                                 router_expert_w[i], router_expert_b[i], expert_w_gate[i],
                                 expert_w_up[i], expert_w_down[i])
        ple_gate = jax.nn.sigmoid((rms_norm(h, ple_norm_w[i]) @ ple_gate_w[i]).astype(jnp.float32)).astype(h.dtype)
        h = h + ple_gate * (p[i] @ ple_proj_w[i])
    return rms_norm(h, final_norm_w)
```

```python
import functools

import jax
import jax.numpy as jnp
from jax import lax
from jax.experimental import pallas as pl
from jax.experimental.pallas import tpu as pltpu

F32 = jnp.float32
BF16 = jnp.bfloat16
HIGHEST = lax.Precision.HIGHEST

D_MODEL = 2048
DEPTH = 2
BRANCH_WIDTH = D_MODEL // 2
N_BRANCHES = 4
PLE_DIM = 256
EPS = 1e-6
FOX_HEAD_DIM = 128
FOX_HEADS = BRANCH_WIDTH // FOX_HEAD_DIM
POOL_WINDOWS = (2, 4, 8, 16)
POOL_GROUP = BRANCH_WIDTH // len(POOL_WINDOWS)
POOL_HALO = 16
SGU_GROUPS = 4
SGU_CHUNK = 128
SGU_GROUP_WIDTH = BRANCH_WIDTH // SGU_GROUPS
HGRN_KEY_DIM = 128
HGRN_VAL_DIM = 128
HGRN_HEADS = BRANCH_WIDTH // HGRN_VAL_DIM
HGRN_CHUNK = 64
N_GROUPS = 4
EXPERTS_PER_GROUP = 8
N_EXPERTS = N_GROUPS * EXPERTS_PER_GROUP
TOP_K = 2
EXPERT_HIDDEN = D_MODEL // 2
FOX_QKV_END = 3 * BRANCH_WIDTH
FOX_F_END = FOX_QKV_END + FOX_HEADS
POOL_END = FOX_F_END + BRANCH_WIDTH
SGU_END = POOL_END + 2 * BRANCH_WIDTH
HGRN_END = SGU_END + 4 * BRANCH_WIDTH
IN_COLS = HGRN_END + N_BRANCHES * D_MODEL

LANES = 128
VMEM_LIMIT = 56 * 1024 * 1024

NORM_ROWS = 512
MM_ROWS = 2048
MM_COLS = 512
MERGE_ROWS = 1024
ATT_BLOCK = 512
CUMSUM_BLOCK = 512
POOL_ROWS = 512
SGU_ROWS = 512
HGRN_ROWS = 256
MOE_ROWS = 256
COMBINE_ROWS = 256
PLE_ROWS = 1024


def _params(semantics, **kw):
    return pltpu.CompilerParams(dimension_semantics=semantics, vmem_limit_bytes=VMEM_LIMIT, **kw)


def _sigmoid(x):
    return 1.0 / (1.0 + jnp.exp(-x))


def _log_sigmoid(x):
    return jnp.minimum(x, 0.0) - jnp.log(1.0 + jnp.exp(-jnp.abs(x)))


def _rms(x, w):
    return x * lax.rsqrt(jnp.mean(x * x, axis=-1, keepdims=True) + EPS) * w


def _norm_kernel(x_ref, w_ref, n_ref):
    n_ref[...] = _rms(x_ref[...], w_ref[...]).astype(n_ref.dtype)


def rms_norm(x, w, out_dtype):
    t, d = x.shape
    return pl.pallas_call(
        _norm_kernel,
        grid=(t // NORM_ROWS,),
        in_specs=[pl.BlockSpec((NORM_ROWS, d), lambda i: (i, 0)),
                  pl.BlockSpec((1, d), lambda i: (0, 0))],
        out_specs=pl.BlockSpec((NORM_ROWS, d), lambda i: (i, 0)),
        out_shape=jax.ShapeDtypeStruct((t, d), out_dtype),
        compiler_params=_params(("parallel",)),
        name="rms_norm",
    )(x, w.reshape(1, d))


def _norm_proj_kernel(x_ref, w_ref, wp_ref, n_ref, p_ref, *, exact_proj):
    n = _rms(x_ref[...], w_ref[...])
    n_ref[...] = n.astype(n_ref.dtype)
    if exact_proj:
        p_ref[...] = jnp.dot(n, wp_ref[...], precision=HIGHEST, preferred_element_type=F32)
    else:
        p_ref[...] = jnp.dot(n.astype(BF16), wp_ref[...], preferred_element_type=F32)


def rms_norm_proj(x, w, wp, out_dtype, exact_proj):
    t, d = x.shape
    return pl.pallas_call(
        functools.partial(_norm_proj_kernel, exact_proj=exact_proj),
        grid=(t // NORM_ROWS,),
        in_specs=[pl.BlockSpec((NORM_ROWS, d), lambda i: (i, 0)),
                  pl.BlockSpec((1, d), lambda i: (0, 0)),
                  pl.BlockSpec((d, LANES), lambda i: (0, 0))],
        out_specs=[pl.BlockSpec((NORM_ROWS, d), lambda i: (i, 0)),
                   pl.BlockSpec((NORM_ROWS, LANES), lambda i: (i, 0))],
        out_shape=[jax.ShapeDtypeStruct((t, d), out_dtype),
                   jax.ShapeDtypeStruct((t, LANES), F32)],
        compiler_params=_params(("parallel",)),
        name="rms_norm_proj",
    )(x, w.reshape(1, d), wp)


def _mm_kernel(a_ref, w_ref, o_ref):
    o_ref[...] = jnp.dot(a_ref[...], w_ref[...], preferred_element_type=F32).astype(o_ref.dtype)


def matmul(a, w, out_dtype):
    m, k = a.shape
    n = w.shape[1]
    return pl.pallas_call(
        _mm_kernel,
        grid=(m // MM_ROWS, n // MM_COLS),
        in_specs=[pl.BlockSpec((MM_ROWS, k), lambda i, j: (i, 0)),
                  pl.BlockSpec((k, MM_COLS), lambda i, j: (0, j))],
        out_specs=pl.BlockSpec((MM_ROWS, MM_COLS), lambda i, j: (i, j)),
        out_shape=jax.ShapeDtypeStruct((m, n), out_dtype),
        compiler_params=_params(("parallel", "arbitrary")),
        name="matmul",
    )(a, w)


def _mm_res_kernel(a_ref, w_ref, r_ref, o_ref):
    o_ref[...] = r_ref[...] + jnp.dot(a_ref[...], w_ref[...], preferred_element_type=F32)


def matmul_residual(a, w, res):
    m, k = a.shape
    n = w.shape[1]
    return pl.pallas_call(
        _mm_res_kernel,
        grid=(m // MM_ROWS, n // MM_COLS),
        in_specs=[pl.BlockSpec((MM_ROWS, k), lambda i, j: (i, 0)),
                  pl.BlockSpec((k, MM_COLS), lambda i, j: (0, j)),
                  pl.BlockSpec((MM_ROWS, MM_COLS), lambda i, j: (i, j))],
        out_specs=pl.BlockSpec((MM_ROWS, MM_COLS), lambda i, j: (i, j)),
        out_shape=jax.ShapeDtypeStruct((m, n), F32),
        compiler_params=_params(("parallel", "arbitrary")),
        name="matmul_residual",
    )(a, w, res)


def _merge_kernel(b0_ref, b1_ref, b2_ref, b3_ref, p_ref, g0_ref, g1_ref, g2_ref, g3_ref, o_ref):
    acc = None
    for bi, (b_ref, g_ref) in enumerate(((b0_ref, g0_ref), (b1_ref, g1_ref),
                                         (b2_ref, g2_ref), (b3_ref, g3_ref))):
        y = jnp.dot(b_ref[...], p_ref[bi], preferred_element_type=F32)
        y = _sigmoid(g_ref[...].astype(F32)) * y
        acc = y if acc is None else acc + y
    o_ref[...] = acc.astype(o_ref.dtype)


def merge_branches(branches, proj, z_a, gate_col0):
    t = branches[0].shape[0]
    g0 = gate_col0 // MM_COLS
    per = D_MODEL // MM_COLS
    b_spec = pl.BlockSpec((MERGE_ROWS, BRANCH_WIDTH), lambda i, j: (i, 0))
    g_specs = [pl.BlockSpec((MERGE_ROWS, MM_COLS), functools.partial(lambda i, j, o: (i, o + j), o=g0 + bi * per))
               for bi in range(N_BRANCHES)]
    return pl.pallas_call(
        _merge_kernel,
        grid=(t // MERGE_ROWS, per),
        in_specs=[b_spec] * N_BRANCHES
        + [pl.BlockSpec((N_BRANCHES, BRANCH_WIDTH, MM_COLS), lambda i, j: (0, 0, j))] + g_specs,
        out_specs=pl.BlockSpec((MERGE_ROWS, MM_COLS), lambda i, j: (i, j)),
        out_shape=jax.ShapeDtypeStruct((t, D_MODEL), BF16),
        compiler_params=_params(("parallel", "arbitrary")),
        name="merge_branches",
    )(*branches, proj, z_a, z_a, z_a, z_a)


def _ple_kernel(n_ref, wg_ref, p_ref, wp_ref, h_ref, o_ref):
    gate = _sigmoid(jnp.dot(n_ref[...], wg_ref[...], preferred_element_type=F32))
    emb = jnp.dot(p_ref[...], wp_ref[...], preferred_element_type=F32)
    o_ref[...] = h_ref[...] + gate * emb


def ple_update(n, wg, p, wp, h):
    t = h.shape[0]
    return pl.pallas_call(
        _ple_kernel,
        grid=(t // PLE_ROWS, D_MODEL // MM_COLS),
        in_specs=[pl.BlockSpec((PLE_ROWS, D_MODEL), lambda i, j: (i, 0)),
                  pl.BlockSpec((D_MODEL, MM_COLS), lambda i, j: (0, j)),
                  pl.BlockSpec((PLE_ROWS, PLE_DIM), lambda i, j: (i, 0)),
                  pl.BlockSpec((PLE_DIM, MM_COLS), lambda i, j: (0, j)),
                  pl.BlockSpec((PLE_ROWS, MM_COLS), lambda i, j: (i, j))],
        out_specs=pl.BlockSpec((PLE_ROWS, MM_COLS), lambda i, j: (i, j)),
        out_shape=jax.ShapeDtypeStruct((t, D_MODEL), F32),
        compiler_params=_params(("parallel", "arbitrary")),
        name="ple_update",
    )(n, wg, p, wp, h)


def _forget_cumsum_kernel(z_ref, b_ref, c_ref, carry_ref):
    @pl.when(pl.program_id(1) == 0)
    def _():
        carry_ref[...] = jnp.zeros_like(carry_ref)

    lf = _log_sigmoid(z_ref[0] + b_ref[...])
    n = lf.shape[0]
    tril = (lax.broadcasted_iota(jnp.int32, (n, n), 0) >= lax.broadcasted_iota(jnp.int32, (n, n), 1)).astype(F32)
    c = jnp.dot(tril, lf, precision=HIGHEST, preferred_element_type=F32) + carry_ref[...]
    c_ref[0] = c
    carry_ref[...] = c[n - 1:n, :]


def forget_cumsum(z_f, bias):
    b, s, _ = z_f.shape
    return pl.pallas_call(
        _forget_cumsum_kernel,
        grid=(b, s // CUMSUM_BLOCK),
        in_specs=[pl.BlockSpec((1, CUMSUM_BLOCK, LANES), lambda bi, i: (bi, i, 0)),
                  pl.BlockSpec((1, LANES), lambda bi, i: (0, 0))],
        out_specs=pl.BlockSpec((1, CUMSUM_BLOCK, LANES), lambda bi, i: (bi, i, 0)),
        out_shape=jax.ShapeDtypeStruct((b, s, LANES), F32),
        scratch_shapes=[pltpu.VMEM((1, LANES), F32)],
        compiler_params=_params(("parallel", "arbitrary")),
        name="forget_cumsum",
    )(z_f, bias)


def _fox_kernel(q_ref, k_ref, v_ref, cq_ref, ck_ref, o_ref, m_ref, l_ref, acc_ref, cqh_ref):
    h = pl.program_id(1)
    qi = pl.program_id(2)
    ki = pl.program_id(3)
    blk = ATT_BLOCK

    @pl.when(ki == 0)
    def _():
        m_ref[...] = jnp.full_like(m_ref, -jnp.inf)
        l_ref[...] = jnp.zeros_like(l_ref)
        acc_ref[...] = jnp.zeros_like(acc_ref)
        lane = lax.broadcasted_iota(jnp.int32, (blk, LANES), 1)
        cqh_ref[...] = jnp.sum(jnp.where(lane == h, cq_ref[0], 0.0), axis=-1, keepdims=True)

    def step(masked):
        s = lax.dot_general(q_ref[0], k_ref[0], (((1,), (1,)), ((), ())), preferred_element_type=F32)
        s = s * (FOX_HEAD_DIM ** -0.5) + (cqh_ref[...] - ck_ref[0, 0])
        if masked:
            row = lax.broadcasted_iota(jnp.int32, (blk, blk), 0)
            col = lax.broadcasted_iota(jnp.int32, (blk, blk), 1)
            s = jnp.where(col <= row, s, -jnp.inf)
        m_old = m_ref[...]
        m_new = jnp.maximum(m_old, jnp.max(s, axis=-1, keepdims=True))
        alpha = jnp.exp(m_old - m_new)
        p = jnp.exp(s - m_new)
        l_ref[...] = alpha * l_ref[...] + jnp.sum(p, axis=-1, keepdims=True)
        acc_ref[...] = alpha * acc_ref[...] + jnp.dot(p.astype(BF16), v_ref[0], preferred_element_type=F32)
        m_ref[...] = m_new

    @pl.when(ki < qi)
    def _():
        step(False)

    @pl.when(ki == qi)
    def _():
        step(True)
        o_ref[0] = (acc_ref[...] / l_ref[...]).astype(o_ref.dtype)


def fox_attention(z_a, c_col, c_row):
    b, s, _ = z_a.shape
    nb = s // ATT_BLOCK
    hd = FOX_HEADS
    return pl.pallas_call(
        _fox_kernel,
        grid=(b, hd, nb, nb),
        in_specs=[pl.BlockSpec((1, ATT_BLOCK, FOX_HEAD_DIM), lambda bi, h, qi, ki: (bi, qi, h)),
                  pl.BlockSpec((1, ATT_BLOCK, FOX_HEAD_DIM), lambda bi, h, qi, ki: (bi, jnp.minimum(ki, qi), hd + h)),
                  pl.BlockSpec((1, ATT_BLOCK, FOX_HEAD_DIM), lambda bi, h, qi, ki: (bi, jnp.minimum(ki, qi), 2 * hd + h)),
                  pl.BlockSpec((1, ATT_BLOCK, LANES), lambda bi, h, qi, ki: (bi, qi, 0)),
                  pl.BlockSpec((1, 1, 1, ATT_BLOCK), lambda bi, h, qi, ki: (bi, h, 0, jnp.minimum(ki, qi)))],
        out_specs=pl.BlockSpec((1, ATT_BLOCK, FOX_HEAD_DIM), lambda bi, h, qi, ki: (bi, qi, h)),
        out_shape=jax.ShapeDtypeStruct((b, s, BRANCH_WIDTH), BF16),
        scratch_shapes=[pltpu.VMEM((ATT_BLOCK, 1), F32), pltpu.VMEM((ATT_BLOCK, 1), F32),
                        pltpu.VMEM((ATT_BLOCK, FOX_HEAD_DIM), F32), pltpu.VMEM((ATT_BLOCK, 1), F32)],
        compiler_params=_params(("parallel", "parallel", "parallel", "arbitrary")),
        name="fox_attention",
    )(z_a, z_a, z_a, c_col, c_row)


def _pool_kernel(u_ref, halo_ref, w_ref, scale_ref, o_ref):
    i = pl.program_id(1)
    rows = POOL_ROWS
    t = i * rows + lax.broadcasted_iota(jnp.int32, (rows, 1), 0)
    keep_halo = (i > 0).astype(F32)
    for g, win in enumerate(POOL_WINDOWS):
        cols = slice(g * POOL_GROUP, (g + 1) * POOL_GROUP)
        u = u_ref[0, :, cols]
        ext = jnp.concatenate([halo_ref[0, :, cols] * keep_halo, u], axis=0)
        span = 1
        while span < win:
            n = ext.shape[0]
            ext = ext[:n - span] + ext[span:]
            span *= 2
        window_sum = ext[POOL_HALO + 1 - win:POOL_HALO + 1 - win + rows]
        count = jnp.minimum(t + 1, win).astype(F32)
        d = (window_sum / count - u).astype(BF16)
        y = jnp.dot(d, w_ref[g], preferred_element_type=F32) * scale_ref[:, cols]
        o_ref[0, :, cols] = y.astype(o_ref.dtype)


def multiscale_pool(z_b, pool_w, pool_scale):
    b, s, _ = z_b.shape
    per = POOL_ROWS // POOL_HALO
    return pl.pallas_call(
        _pool_kernel,
        grid=(b, s // POOL_ROWS),
        in_specs=[pl.BlockSpec((1, POOL_ROWS, BRANCH_WIDTH), lambda bi, i: (bi, i, 0)),
                  pl.BlockSpec((1, POOL_HALO, BRANCH_WIDTH), lambda bi, i: (bi, jnp.maximum(i * per - 1, 0), 0)),
                  pl.BlockSpec((len(POOL_WINDOWS), POOL_GROUP, POOL_GROUP), lambda bi, i: (0, 0, 0)),
                  pl.BlockSpec((1, BRANCH_WIDTH), lambda bi, i: (0, 0))],
        out_specs=pl.BlockSpec((1, POOL_ROWS, BRANCH_WIDTH), lambda bi, i: (bi, i, 0)),
        out_shape=jax.ShapeDtypeStruct((b, s, BRANCH_WIDTH), BF16),
        compiler_params=_params(("parallel", "parallel")),
        name="multiscale_pool",
    )(z_b, z_b, pool_w, pool_scale.reshape(1, BRANCH_WIDTH))


def _sgu_kernel(u_ref, v_ref, nw_ref, w_ref, b_ref, o_ref):
    c = SGU_CHUNK
    v = jax.nn.gelu(v_ref[...].astype(F32))
    vc = v - jnp.mean(v, axis=-1, keepdims=True)
    vn = (vc * lax.rsqrt(jnp.mean(vc * vc, axis=-1, keepdims=True) + EPS) * nw_ref[...]).astype(BF16)
    causal = lax.broadcasted_iota(jnp.int32, (c, c), 0) >= lax.broadcasted_iota(jnp.int32, (c, c), 1)
    for g in range(SGU_GROUPS):
        cols = slice(g * SGU_GROUP_WIDTH, (g + 1) * SGU_GROUP_WIDTH)
        w = jnp.where(causal, w_ref[g], 0.0).astype(BF16)
        bias = b_ref[g]
        for n in range(SGU_ROWS // c):
            rows = slice(n * c, (n + 1) * c)
            sv = jnp.dot(w, vn[rows, cols], preferred_element_type=F32) + bias
            u = jax.nn.gelu(u_ref[rows, cols].astype(F32))
            o_ref[rows, cols] = (u * sv).astype(o_ref.dtype)


def spatial_gating(z_a, u_col0, norm_w, w_s, b_s):
    t = z_a.shape[0]
    ub = u_col0 // BRANCH_WIDTH
    return pl.pallas_call(
        _sgu_kernel,
        grid=(t // SGU_ROWS,),
        in_specs=[pl.BlockSpec((SGU_ROWS, BRANCH_WIDTH), lambda i: (i, ub)),
                  pl.BlockSpec((SGU_ROWS, BRANCH_WIDTH), lambda i: (i, ub + 1)),
                  pl.BlockSpec((1, BRANCH_WIDTH), lambda i: (0, 0)),
                  pl.BlockSpec((SGU_GROUPS, SGU_CHUNK, SGU_CHUNK), lambda i: (0, 0, 0)),
                  pl.BlockSpec((SGU_GROUPS, SGU_CHUNK, 1), lambda i: (0, 0, 0))],
        out_specs=pl.BlockSpec((SGU_ROWS, BRANCH_WIDTH), lambda i: (i, 0)),
        out_shape=jax.ShapeDtypeStruct((t, BRANCH_WIDTH), BF16),
        compiler_params=_params(("parallel",)),
        name="spatial_gating",
    )(z_a, z_a, norm_w.reshape(1, BRANCH_WIDTH), w_s, b_s.reshape(SGU_GROUPS, SGU_CHUNK, 1))


def _hgrn_kernel(q_ref, f_ref, v_ref, g_ref, lb_ref, nw_ref, o_ref, state_ref, oacc_ref):
    c = HGRN_CHUNK

    @pl.when(pl.program_id(2) == 0)
    def _():
        state_ref[...] = jnp.zeros_like(state_ref)

    lb = lb_ref[0]
    nw = nw_ref[0]
    tril = (lax.broadcasted_iota(jnp.int32, (c, c), 0) >= lax.broadcasted_iota(jnp.int32, (c, c), 1)).astype(F32)

    def chunk(ci, carry):
        rows = pl.ds(pl.multiple_of(ci * c, c), c)
        q = q_ref[0, rows, :]
        fl = f_ref[0, rows, :]
        v = v_ref[0, rows, :]
        g = g_ref[0, rows, :]
        log_f = jnp.log(lb + (1.0 - lb) * _sigmoid(fl))
        kk = (1.0 - lb) * _sigmoid(-fl)
        b = jnp.dot(tril, log_f, precision=HIGHEST, preferred_element_type=F32)
        state_t = state_ref[...]
        oacc_ref[...] = lax.dot_general((q * jnp.exp(b)).astype(BF16), state_t.astype(BF16),
                                        (((1,), (1,)), ((), ())), preferred_element_type=F32)
        for s in range(c):
            r0 = (s // 8) * 8
            t_idx = r0 + lax.broadcasted_iota(jnp.int32, (c - r0, 1), 0)
            diff = jnp.where(t_idx >= s, b[r0:, :] - b[s:s + 1, :], -jnp.inf)
            a_col = jnp.sum(q[r0:, :] * kk[s:s + 1, :] * jnp.exp(diff), axis=-1, keepdims=True)
            oacc_ref[r0:, :] += a_col * v[s:s + 1, :]
        b_end = b[c - 1:c, :]
        k_dec = (kk * jnp.exp(b_end - b)).astype(BF16)
        upd = lax.dot_general(v.astype(BF16), k_dec, (((0,), (0,)), ((), ())), preferred_element_type=F32)
        state_ref[...] = state_t * jnp.exp(b_end) + upd
        o = oacc_ref[...]
        o = o * lax.rsqrt(jnp.mean(o * o, axis=-1, keepdims=True) + EPS) * nw
        o_ref[0, rows, :] = (o * (g * _sigmoid(g))).astype(o_ref.dtype)
        return carry

    lax.fori_loop(0, HGRN_ROWS // c, chunk, 0)


def hgrn2(z_b, col0, lower_bound, norm_w):
    b, s, _ = z_b.shape
    c0 = col0 // LANES
    nh = HGRN_HEADS

    def spec(part):
        return pl.BlockSpec((1, HGRN_ROWS, LANES), lambda bi, h, i: (bi, i, c0 + part * nh + h))

    vec = pl.BlockSpec((1, 1, LANES), lambda bi, h, i: (h, 0, 0))
    return pl.pallas_call(
        _hgrn_kernel,
        grid=(b, nh, s // HGRN_ROWS),
        in_specs=[spec(0), spec(1), spec(2), spec(3), vec, vec],
        out_specs=pl.BlockSpec((1, HGRN_ROWS, LANES), lambda bi, h, i: (bi, i, h)),
        out_shape=jax.ShapeDtypeStruct((b, s, BRANCH_WIDTH), BF16),
        scratch_shapes=[pltpu.VMEM((HGRN_VAL_DIM, HGRN_KEY_DIM), F32), pltpu.VMEM((HGRN_CHUNK, HGRN_VAL_DIM), F32)],
        compiler_params=_params(("parallel", "parallel", "arbitrary")),
        name="hgrn2",
    )(z_b, z_b, z_b, z_b, lower_bound.reshape(nh, 1, LANES), norm_w.reshape(nh, 1, LANES))


def _router_kernel(x_ref, w_ref, wr_ref, br_ref, n_ref, e_ref, p_ref):
    n = _rms(x_ref[...], w_ref[...])
    n_ref[...] = n
    logits = jnp.dot(n, wr_ref[...], precision=HIGHEST, preferred_element_type=F32) + br_ref[...]
    col = lax.broadcasted_iota(jnp.int32, logits.shape, 1)
    big = jnp.int32(LANES)
    neg = -jnp.inf

    def first_argmax(vals):
        top = jnp.max(vals, axis=-1, keepdims=True)
        return top, jnp.min(jnp.where(vals == top, col, big), axis=-1, keepdims=True)

    group_logits = jnp.where(col < N_GROUPS, logits, neg)
    g_top, g_idx = first_argmax(group_logits)
    p_group = 1.0 / jnp.sum(jnp.exp(group_logits - g_top), axis=-1, keepdims=True)
    lo = N_GROUPS + g_idx * EXPERTS_PER_GROUP
    in_group = jnp.where((col >= lo) & (col < lo + EXPERTS_PER_GROUP), logits, neg)
    top1, idx1 = first_argmax(in_group)
    top2, idx2 = first_argmax(jnp.where(col == idx1, neg, in_group))
    r = jnp.exp(top2 - top1)
    p1 = p_group / (1.0 + r)
    p2 = p_group * r / (1.0 + r)
    e_ref[...] = jnp.where(col == 0, idx1 - N_GROUPS, jnp.where(col == 1, idx2 - N_GROUPS, 0))
    p_ref[...] = jnp.where(col == 0, p1, jnp.where(col == 1, p2, 0.0))


def moe_router(h, norm_w, w_router, b_router):
    t, d = h.shape
    row = pl.BlockSpec((NORM_ROWS, d), lambda i: (i, 0))
    small = pl.BlockSpec((NORM_ROWS, LANES), lambda i: (i, 0))
    return pl.pallas_call(
        _router_kernel,
        grid=(t // NORM_ROWS,),
        in_specs=[row, pl.BlockSpec((1, d), lambda i: (0, 0)),
                  pl.BlockSpec((d, LANES), lambda i: (0, 0)), pl.BlockSpec((1, LANES), lambda i: (0, 0))],
        out_specs=[row, small, small],
        out_shape=[jax.ShapeDtypeStruct((t, d), F32), jax.ShapeDtypeStruct((t, LANES), jnp.int32),
                   jax.ShapeDtypeStruct((t, LANES), F32)],
        compiler_params=_params(("parallel",)),
        name="moe_router",
    )(h, norm_w.reshape(1, d), w_router, b_router)


def _gather_rows(idx_ref, n_rows, src_hbm, dst_ref, sem):
    def body(r, carry):
        pltpu.make_async_copy(src_hbm.at[pl.ds(idx_ref[r], 1)], dst_ref.at[pl.ds(r, 1)], sem).start()
        return carry
    lax.fori_loop(0, n_rows, body, 0)


def _wait_rows(src_hbm, dst_ref, sem):
    pltpu.make_async_copy(src_hbm.at[pl.ds(0, dst_ref.shape[0])], dst_ref, sem).wait()


def _expert_kernel(be_ref, nused_ref, tok_cur_ref, tok_next_ref, x_hbm, wg_ref, wu_ref, wd_ref, y_ref,
                   xbuf_ref, sem_ref):
    i = pl.program_id(0)
    n_used = nused_ref[0]
    slot = i % 2

    @pl.when((i == 0) & (n_used > 0))
    def _():
        _gather_rows(tok_cur_ref, MOE_ROWS, x_hbm, xbuf_ref.at[0], sem_ref.at[0])

    @pl.when(i + 1 < n_used)
    def _():
        _gather_rows(tok_next_ref, MOE_ROWS, x_hbm, xbuf_ref.at[1 - slot], sem_ref.at[1 - slot])

    @pl.when(i < n_used)
    def _():
        _wait_rows(x_hbm, xbuf_ref.at[slot], sem_ref.at[slot])
        x = xbuf_ref[slot].astype(BF16)
        gate = jnp.dot(x, wg_ref[0], preferred_element_type=F32)
        up = jnp.dot(x, wu_ref[0], preferred_element_type=F32)
        hidden = (gate * _sigmoid(gate) * up).astype(BF16)
        y_ref[...] = jnp.dot(hidden, wd_ref[0], preferred_element_type=F32)

    @pl.when(i >= n_used)
    def _():
        y_ref[...] = jnp.zeros_like(y_ref)


def expert_mlp(x, buf_tok, block_e, n_used, w_gate, w_up, w_down):
    p = buf_tok.shape[0]
    nb = p // MOE_ROWS
    d = x.shape[1]
    smem = pltpu.SMEM
    grid_spec = pltpu.PrefetchScalarGridSpec(
        num_scalar_prefetch=2,
        grid=(nb,),
        in_specs=[pl.BlockSpec((MOE_ROWS,), lambda i, be, nu: (i,), memory_space=smem),
                  pl.BlockSpec((MOE_ROWS,), lambda i, be, nu: (jnp.minimum(i + 1, nb - 1),), memory_space=smem),
                  pl.BlockSpec(memory_space=pl.ANY),
                  pl.BlockSpec((1, d, EXPERT_HIDDEN), lambda i, be, nu: (be[i], 0, 0)),
                  pl.BlockSpec((1, d, EXPERT_HIDDEN), lambda i, be, nu: (be[i], 0, 0)),
                  pl.BlockSpec((1, EXPERT_HIDDEN, d), lambda i, be, nu: (be[i], 0, 0))],
        out_specs=pl.BlockSpec((MOE_ROWS, d), lambda i, be, nu: (i, 0)),
        scratch_shapes=[pltpu.VMEM((2, MOE_ROWS, d), F32), pltpu.SemaphoreType.DMA((2,))],
    )
    return pl.pallas_call(
        _expert_kernel,
        grid_spec=grid_spec,
        out_shape=jax.ShapeDtypeStruct((p, d), F32),
        compiler_params=_params(("arbitrary",)),
        name="expert_mlp",
    )(block_e, n_used, buf_tok, buf_tok, x, w_gate, w_up, w_down)


def _combine_kernel(pos0_ref, pos1_ref, y_hbm, h_ref, p_ref, w_ref, o_ref, n_ref, ybuf_ref, sem_ref):
    rows = COMBINE_ROWS
    _gather_rows(pos0_ref, rows, y_hbm, ybuf_ref.at[0], sem_ref.at[0])
    _gather_rows(pos1_ref, rows, y_hbm, ybuf_ref.at[1], sem_ref.at[1])
    _wait_rows(y_hbm, ybuf_ref.at[0], sem_ref.at[0])
    _wait_rows(y_hbm, ybuf_ref.at[1], sem_ref.at[1])
    p = p_ref[...]
    h = h_ref[...] + (ybuf_ref[0] * p[:, 0:1] + ybuf_ref[1] * p[:, 1:2])
    o_ref[...] = h
    n_ref[...] = _rms(h, w_ref[...]).astype(n_ref.dtype)


def moe_combine(y, pos0, pos1, probs, h, norm_w):
    t, d = h.shape
    smem = pltpu.SMEM
    row = pl.BlockSpec((COMBINE_ROWS, d), lambda i: (i, 0))
    return pl.pallas_call(
        _combine_kernel,
        grid=(t // COMBINE_ROWS,),
        in_specs=[pl.BlockSpec((COMBINE_ROWS,), lambda i: (i,), memory_space=smem),
                  pl.BlockSpec((COMBINE_ROWS,), lambda i: (i,), memory_space=smem),
                  pl.BlockSpec(memory_space=pl.ANY),
                  row,
                  pl.BlockSpec((COMBINE_ROWS, LANES), lambda i: (i, 0)),
                  pl.BlockSpec((1, d), lambda i: (0, 0))],
        out_specs=[row, row],
        out_shape=[jax.ShapeDtypeStruct((t, d), F32), jax.ShapeDtypeStruct((t, d), BF16)],
        scratch_shapes=[pltpu.VMEM((2, COMBINE_ROWS, d), F32), pltpu.SemaphoreType.DMA((2,))],
        compiler_params=_params(("arbitrary",)),
        name="moe_combine",
    )(pos0, pos1, y, h, probs, norm_w.reshape(1, d))


def _dispatch_plan(expert_ids):
    t = expert_ids.shape[0]
    a = t * TOP_K
    flat_e = expert_ids.reshape(a)
    onehot = (flat_e[:, None] == jnp.arange(N_EXPERTS, dtype=jnp.int32)[None, :]).astype(jnp.int32)
    rank = jnp.sum((jnp.cumsum(onehot, axis=0) - onehot) * onehot, axis=1)
    counts = jnp.sum(onehot, axis=0)
    padded = (counts + MOE_ROWS - 1) // MOE_ROWS * MOE_ROWS
    pad_end = jnp.cumsum(padded)
    pad_start = pad_end - padded
    dest = pad_start[flat_e] + rank
    p = a + N_EXPERTS * MOE_ROWS
    nb = p // MOE_ROWS
    token_ids = jnp.repeat(jnp.arange(t, dtype=jnp.int32), TOP_K)
    buf_tok = jnp.zeros((p,), jnp.int32).at[dest].set(token_ids)
    block_e = jnp.clip(jnp.searchsorted(pad_end, jnp.arange(nb, dtype=jnp.int32) * MOE_ROWS, side='right'),
                       0, N_EXPERTS - 1).astype(jnp.int32)
    n_used = (pad_end[-1:] // MOE_ROWS).astype(jnp.int32)
    pos = dest.reshape(t, TOP_K).astype(jnp.int32)
    return buf_tok, block_e, n_used, pos[:, 0], pos[:, 1]


def _pad_cols(w, n):
    return jnp.pad(w, ((0, 0), (0, n - w.shape[1])))


def _layer_weights(i, w_in, router_group_w, router_group_b, router_expert_w, router_expert_b):
    wi = w_in[i]
    w_a = jnp.concatenate([wi[:, :FOX_QKV_END], wi[:, POOL_END:SGU_END], wi[:, HGRN_END:]], axis=1).astype(BF16)
    w_b = jnp.concatenate([wi[:, FOX_F_END:POOL_END], wi[:, SGU_END:HGRN_END]], axis=1).astype(BF16)
    w_f = _pad_cols(wi[:, FOX_QKV_END:FOX_F_END], LANES).astype(BF16)
    w_r = _pad_cols(jnp.concatenate([router_group_w[i], router_expert_w[i]], axis=1), LANES)
    b_r = _pad_cols(jnp.concatenate([router_group_b[i], router_expert_b[i]])[None, :], LANES)
    return w_a, w_b, w_f, w_r, b_r


def kernel(x, p, mix_norm_w, w_in, fox_f_bias, pool_w, pool_scale, sgu_norm_w, sgu_w, sgu_b,
           hgrn_lb_logits, hgrn_norm_w, branch_proj, w_out, ffn_norm_w, router_group_w,
           router_group_b, router_expert_w, router_expert_b, expert_w_gate, expert_w_up,
           expert_w_down, ple_norm_w, ple_gate_w, ple_proj_w, final_norm_w):
    bsz, seq, d = x.shape
    t = bsz * seq
    lb_p = jax.nn.softmax(hgrn_lb_logits.astype(F32), axis=0)
    lower_bounds = jnp.cumsum(lb_p, axis=0) - lb_p[0:1]
    sgu_u_col = FOX_QKV_END
    gate_col = FOX_QKV_END + 2 * BRANCH_WIDTH
    h = x.reshape(t, d)
    for i in range(DEPTH):
        w_a, w_b, w_f, w_r, b_r = _layer_weights(i, w_in, router_group_w, router_group_b,
                                                 router_expert_w, router_expert_b)
        n, z_f = rms_norm_proj(h, mix_norm_w[i], w_f, BF16, exact_proj=False)
        z_a = matmul(n, w_a, BF16)
        z_b = matmul(n, w_b, F32)
        z_a3 = z_a.reshape(bsz, seq, -1)
        z_b3 = z_b.reshape(bsz, seq, -1)

        f_bias = _pad_cols(fox_f_bias[i][None, :].astype(F32), LANES)
        c_col = forget_cumsum(z_f.reshape(bsz, seq, LANES), f_bias)
        c_row = jnp.transpose(c_col[:, :, :FOX_HEADS], (0, 2, 1))[:, :, None, :]
        br_fox = fox_attention(z_a3, c_col, c_row).reshape(t, BRANCH_WIDTH)
        br_pool = multiscale_pool(z_b3, pool_w[i].astype(BF16), pool_scale[i]).reshape(t, BRANCH_WIDTH)
        br_sgu = spatial_gating(z_a, sgu_u_col, sgu_norm_w[i], sgu_w[i], sgu_b[i])
        br_hgrn = hgrn2(z_b3, BRANCH_WIDTH, lower_bounds[i], hgrn_norm_w[i]).reshape(t, BRANCH_WIDTH)

        merged = merge_branches((br_fox, br_pool, br_sgu, br_hgrn), branch_proj[i].astype(BF16), z_a, gate_col)
        h = matmul_residual(merged, w_out[i].astype(BF16), h)

        xn, e_ids, probs = moe_router(h, ffn_norm_w[i], w_r, b_r)
        buf_tok, block_e, n_used, pos0, pos1 = _dispatch_plan(e_ids[:, :TOP_K])
        y = expert_mlp(xn, buf_tok, block_e, n_used, expert_w_gate[i].astype(BF16),
                       expert_w_up[i].astype(BF16), expert_w_down[i].astype(BF16))
        h, n_ple = moe_combine(y, pos0, pos1, probs, h, ple_norm_w[i])

        h = ple_update(n_ple, ple_gate_w[i].astype(BF16), p[i].reshape(t, PLE_DIM).astype(BF16),
                       ple_proj_w[i].astype(BF16), h)
    return rms_norm(h, final_norm_w, F32).reshape(bsz, seq, d)
```

```python
import functools

import jax
import jax.numpy as jnp
from jax import lax
from jax.experimental import pallas as pl
from jax.experimental.pallas import tpu as pltpu

F32 = jnp.float32
BF16 = jnp.bfloat16
HIGHEST = lax.Precision.HIGHEST

D_MODEL = 2048
DEPTH = 2
BRANCH_WIDTH = D_MODEL // 2
N_BRANCHES = 4
PLE_DIM = 256
EPS = 1e-6
FOX_HEAD_DIM = 128
FOX_HEADS = BRANCH_WIDTH // FOX_HEAD_DIM
POOL_WINDOWS = (2, 4, 8, 16)
POOL_GROUP = BRANCH_WIDTH // len(POOL_WINDOWS)
POOL_HALO = 16
SGU_GROUPS = 4
SGU_CHUNK = 128
SGU_GROUP_WIDTH = BRANCH_WIDTH // SGU_GROUPS
HGRN_KEY_DIM = 128
HGRN_VAL_DIM = 128
HGRN_HEADS = BRANCH_WIDTH // HGRN_VAL_DIM
HGRN_CHUNK = 64
N_GROUPS = 4
EXPERTS_PER_GROUP = 8
N_EXPERTS = N_GROUPS * EXPERTS_PER_GROUP
TOP_K = 2
EXPERT_HIDDEN = D_MODEL // 2
FOX_QKV_END = 3 * BRANCH_WIDTH
FOX_F_END = FOX_QKV_END + FOX_HEADS
POOL_END = FOX_F_END + BRANCH_WIDTH
SGU_END = POOL_END + 2 * BRANCH_WIDTH
HGRN_END = SGU_END + 4 * BRANCH_WIDTH
IN_COLS = HGRN_END + N_BRANCHES * D_MODEL

LANES = 128
LOG2E = 1.4426950408889634
FOX_HEADS_PER_STEP = 2
FOX_QUERY_CHUNK = 256
VMEM_LIMIT = 56 * 1024 * 1024

NORM_ROWS = 512
MM_ROWS = 2048
MM_COLS = 512
MERGE_ROWS = 1024
ATT_BLOCK = 512
CUMSUM_BLOCK = 512
POOL_ROWS = 512
SGU_ROWS = 512
HGRN_ROWS = 256
MOE_ROWS = 256
COMBINE_ROWS = 256
PLE_ROWS = 1024


def _params(semantics, **kw):
    return pltpu.CompilerParams(dimension_semantics=semantics, vmem_limit_bytes=VMEM_LIMIT, **kw)


def _sigmoid(x):
    return 1.0 / (1.0 + jnp.exp(-x))


def _log_sigmoid(x):
    return jnp.minimum(x, 0.0) - jnp.log(1.0 + jnp.exp(-jnp.abs(x)))


def _rms(x, w):
    return x * lax.rsqrt(jnp.mean(x * x, axis=-1, keepdims=True) + EPS) * w


def _norm_kernel(x_ref, w_ref, n_ref):
    n_ref[...] = _rms(x_ref[...], w_ref[...]).astype(n_ref.dtype)


def rms_norm(x, w, out_dtype):
    t, d = x.shape
    return pl.pallas_call(
        _norm_kernel,
        grid=(t // NORM_ROWS,),
        in_specs=[pl.BlockSpec((NORM_ROWS, d), lambda i: (i, 0)),
                  pl.BlockSpec((1, d), lambda i: (0, 0))],
        out_specs=pl.BlockSpec((NORM_ROWS, d), lambda i: (i, 0)),
        out_shape=jax.ShapeDtypeStruct((t, d), out_dtype),
        compiler_params=_params(("parallel",)),
        name="rms_norm",
    )(x, w.reshape(1, d))


def _norm_proj_kernel(x_ref, w_ref, wp_ref, n_ref, p_ref, *, exact_proj):
    n = _rms(x_ref[...], w_ref[...])
    n_ref[...] = n.astype(n_ref.dtype)
    if exact_proj:
        p_ref[...] = jnp.dot(n, wp_ref[...], precision=HIGHEST, preferred_element_type=F32)
    else:
        p_ref[...] = jnp.dot(n.astype(BF16), wp_ref[...], preferred_element_type=F32)


def rms_norm_proj(x, w, wp, out_dtype, exact_proj):
    t, d = x.shape
    return pl.pallas_call(
        functools.partial(_norm_proj_kernel, exact_proj=exact_proj),
        grid=(t // NORM_ROWS,),
        in_specs=[pl.BlockSpec((NORM_ROWS, d), lambda i: (i, 0)),
                  pl.BlockSpec((1, d), lambda i: (0, 0)),
                  pl.BlockSpec((d, LANES), lambda i: (0, 0))],
        out_specs=[pl.BlockSpec((NORM_ROWS, d), lambda i: (i, 0)),
                   pl.BlockSpec((NORM_ROWS, LANES), lambda i: (i, 0))],
        out_shape=[jax.ShapeDtypeStruct((t, d), out_dtype),
                   jax.ShapeDtypeStruct((t, LANES), F32)],
        compiler_params=_params(("parallel",)),
        name="rms_norm_proj",
    )(x, w.reshape(1, d), wp)


def _mm_kernel(a_ref, w_ref, s_ref, o_ref):
    acc = jnp.dot(a_ref[...], w_ref[...], preferred_element_type=F32)
    o_ref[...] = (acc * s_ref[...]).astype(o_ref.dtype)


def matmul(a, w, col_scale, out_dtype):
    m, k = a.shape
    n = w.shape[1]
    return pl.pallas_call(
        _mm_kernel,
        grid=(m // MM_ROWS, n // MM_COLS),
        in_specs=[pl.BlockSpec((MM_ROWS, k), lambda i, j: (i, 0)),
                  pl.BlockSpec((k, MM_COLS), lambda i, j: (0, j)),
                  pl.BlockSpec((1, MM_COLS), lambda i, j: (0, j))],
        out_specs=pl.BlockSpec((MM_ROWS, MM_COLS), lambda i, j: (i, j)),
        out_shape=jax.ShapeDtypeStruct((m, n), out_dtype),
        compiler_params=_params(("parallel", "arbitrary")),
        name="matmul",
    )(a, w, col_scale)


def _mm_res_kernel(a_ref, w_ref, r_ref, o_ref):
    o_ref[...] = r_ref[...] + jnp.dot(a_ref[...], w_ref[...], preferred_element_type=F32)


def matmul_residual(a, w, res):
    m, k = a.shape
    n = w.shape[1]
    return pl.pallas_call(
        _mm_res_kernel,
        grid=(m // MM_ROWS, n // MM_COLS),
        in_specs=[pl.BlockSpec((MM_ROWS, k), lambda i, j: (i, 0)),
                  pl.BlockSpec((k, MM_COLS), lambda i, j: (0, j)),
                  pl.BlockSpec((MM_ROWS, MM_COLS), lambda i, j: (i, j))],
        out_specs=pl.BlockSpec((MM_ROWS, MM_COLS), lambda i, j: (i, j)),
        out_shape=jax.ShapeDtypeStruct((m, n), F32),
        compiler_params=_params(("parallel", "arbitrary")),
        name="matmul_residual",
    )(a, w, res)


def _merge_kernel(b0_ref, b1_ref, b2_ref, b3_ref, p_ref, g0_ref, g1_ref, g2_ref, g3_ref, o_ref):
    acc = None
    for bi, (b_ref, g_ref) in enumerate(((b0_ref, g0_ref), (b1_ref, g1_ref),
                                         (b2_ref, g2_ref), (b3_ref, g3_ref))):
        y = jnp.dot(b_ref[...], p_ref[bi], preferred_element_type=F32)
        y = _sigmoid(g_ref[...].astype(F32)) * y
        acc = y if acc is None else acc + y
    o_ref[...] = acc.astype(o_ref.dtype)


def merge_branches(branches, proj, z_a, gate_col0):
    t = branches[0].shape[0]
    g0 = gate_col0 // MM_COLS
    per = D_MODEL // MM_COLS
    b_spec = pl.BlockSpec((MERGE_ROWS, BRANCH_WIDTH), lambda i, j: (i, 0))
    g_specs = [pl.BlockSpec((MERGE_ROWS, MM_COLS), functools.partial(lambda i, j, o: (i, o + j), o=g0 + bi * per))
               for bi in range(N_BRANCHES)]
    return pl.pallas_call(
        _merge_kernel,
        grid=(t // MERGE_ROWS, per),
        in_specs=[b_spec] * N_BRANCHES
        + [pl.BlockSpec((N_BRANCHES, BRANCH_WIDTH, MM_COLS), lambda i, j: (0, 0, j))] + g_specs,
        out_specs=pl.BlockSpec((MERGE_ROWS, MM_COLS), lambda i, j: (i, j)),
        out_shape=jax.ShapeDtypeStruct((t, D_MODEL), BF16),
        compiler_params=_params(("parallel", "arbitrary")),
        name="merge_branches",
    )(*branches, proj, z_a, z_a, z_a, z_a)


def _ple_kernel(n_ref, wg_ref, p_ref, wp_ref, h_ref, o_ref):
    gate = _sigmoid(jnp.dot(n_ref[...], wg_ref[...], preferred_element_type=F32))
    emb = jnp.dot(p_ref[...], wp_ref[...], preferred_element_type=F32)
    o_ref[...] = h_ref[...] + gate * emb


def ple_update(n, wg, p, wp, h):
    t = h.shape[0]
    return pl.pallas_call(
        _ple_kernel,
        grid=(t // PLE_ROWS, D_MODEL // MM_COLS),
        in_specs=[pl.BlockSpec((PLE_ROWS, D_MODEL), lambda i, j: (i, 0)),
                  pl.BlockSpec((D_MODEL, MM_COLS), lambda i, j: (0, j)),
                  pl.BlockSpec((PLE_ROWS, PLE_DIM), lambda i, j: (i, 0)),
                  pl.BlockSpec((PLE_DIM, MM_COLS), lambda i, j: (0, j)),
                  pl.BlockSpec((PLE_ROWS, MM_COLS), lambda i, j: (i, j))],
        out_specs=pl.BlockSpec((PLE_ROWS, MM_COLS), lambda i, j: (i, j)),
        out_shape=jax.ShapeDtypeStruct((t, D_MODEL), F32),
        compiler_params=_params(("parallel", "arbitrary")),
        name="ple_update",
    )(n, wg, p, wp, h)


def _bf16_split3(x):
    hi = x.astype(BF16).astype(F32)
    r = x - hi
    mid = r.astype(BF16).astype(F32)
    lo = (r - mid).astype(BF16).astype(F32)
    return hi, mid, lo


def _forget_prep_kernel(z_ref, b_ref, eq_ref, ek_ref, carry_ref):
    @pl.when(pl.program_id(1) == 0)
    def _():
        carry_ref[...] = jnp.zeros_like(carry_ref)

    lf = _log_sigmoid(z_ref[0] + b_ref[...]) * LOG2E
    n = lf.shape[0]
    tril = (lax.broadcasted_iota(jnp.int32, (n, n), 0) >= lax.broadcasted_iota(jnp.int32, (n, n), 1)).astype(F32)
    c = jnp.dot(tril, lf, precision=HIGHEST, preferred_element_type=F32) + carry_ref[...]
    carry_ref[...] = c[n - 1:n, :]
    parts = _bf16_split3(c)
    lane = lax.broadcasted_iota(jnp.int32, (n, LANES), 1)
    for h in range(FOX_HEADS):
        hi, mid, lo = (jnp.sum(jnp.where(lane == h, part, 0.0), axis=-1, keepdims=True) for part in parts)
        eq = jnp.where(lane == 0, hi, jnp.where(lane == 1, mid, jnp.where(lane == 2, lo,
                       jnp.where(lane < 6, 1.0, 0.0))))
        ek = jnp.where(lane < 3, 1.0, jnp.where(lane == 3, -hi, jnp.where(lane == 4, -mid,
                       jnp.where(lane == 5, -lo, 0.0))))
        eq_ref[0, h] = eq.astype(BF16)
        ek_ref[0, h] = ek.astype(BF16)


def forget_prep(z_f, bias):
    b, s, _ = z_f.shape
    out = pl.BlockSpec((1, FOX_HEADS, CUMSUM_BLOCK, LANES), lambda bi, i: (bi, 0, i, 0))
    shape = jax.ShapeDtypeStruct((b, FOX_HEADS, s, LANES), BF16)
    return pl.pallas_call(
        _forget_prep_kernel,
        grid=(b, s // CUMSUM_BLOCK),
        in_specs=[pl.BlockSpec((1, CUMSUM_BLOCK, LANES), lambda bi, i: (bi, i, 0)),
                  pl.BlockSpec((1, LANES), lambda bi, i: (0, 0))],
        out_specs=[out, out],
        out_shape=[shape, shape],
        scratch_shapes=[pltpu.VMEM((1, LANES), F32)],
        compiler_params=_params(("parallel", "arbitrary")),
        name="forget_prep",
    )(z_f, bias)


def _fox_kernel(qi_ref, ki_ref, q_ref, k_ref, vt_ref, eq_ref, ek_ref, o_ref, m_ref, l_ref, acc_ref):
    step = pl.program_id(2)
    qi = qi_ref[step]
    ki = ki_ref[step]
    blk = ATT_BLOCK
    dh = FOX_HEAD_DIM
    qc = FOX_QUERY_CHUNK

    @pl.when(ki == 0)
    def _():
        m_ref[...] = jnp.full_like(m_ref, -jnp.inf)
        l_ref[...] = jnp.zeros_like(l_ref)
        acc_ref[...] = jnp.zeros_like(acc_ref)

    def update(masked):
        for hh in range(FOX_HEADS_PER_STEP):
            cols = slice(hh * dh, (hh + 1) * dh)
            k = jnp.concatenate([k_ref[0, :, cols], ek_ref[0, hh]], axis=1)
            for c in range(blk // qc):
                qs = slice(c * qc, (c + 1) * qc)
                q = jnp.concatenate([q_ref[0, qs, cols], eq_ref[0, hh, qs, :]], axis=1)
                s = lax.dot_general(k, q, (((1,), (1,)), ((), ())), preferred_element_type=F32)
                if masked:
                    key = lax.broadcasted_iota(jnp.int32, (blk, qc), 0)
                    qry = lax.broadcasted_iota(jnp.int32, (blk, qc), 1) + c * qc
                    s = jnp.where(key <= qry, s, -jnp.inf)
                m_old = m_ref[hh, :, qs]
                m_new = jnp.maximum(m_old, jnp.max(s, axis=0, keepdims=True))
                alpha = jnp.exp2(m_old - m_new)
                p = jnp.exp2(s - m_new)
                l_ref[hh, :, qs] = alpha * l_ref[hh, :, qs] + jnp.sum(p, axis=0, keepdims=True)
                acc_ref[hh, :, qs] = alpha * acc_ref[hh, :, qs] + jnp.dot(
                    vt_ref[0, hh], p.astype(BF16), preferred_element_type=F32)
                m_ref[hh, :, qs] = m_new

    @pl.when(ki < qi)
    def _():
        update(False)

    @pl.when(ki == qi)
    def _():
        update(True)
        for hh in range(FOX_HEADS_PER_STEP):
            o_ref[0, :, hh * dh:(hh + 1) * dh] = jnp.transpose(acc_ref[hh] / l_ref[hh]).astype(o_ref.dtype)


def fox_attention(z_a, v_t, eq, ek):
    b, s, _ = z_a.shape
    nb = s // ATT_BLOCK
    hp = FOX_HEADS_PER_STEP
    groups = FOX_HEADS // hp
    pairs = [(qi, ki) for qi in range(nb) for ki in range(qi + 1)]
    qi_tab = jnp.asarray([pr[0] for pr in pairs], jnp.int32)
    ki_tab = jnp.asarray([pr[1] for pr in pairs], jnp.int32)
    width = hp * FOX_HEAD_DIM
    grid_spec = pltpu.PrefetchScalarGridSpec(
        num_scalar_prefetch=2,
        grid=(b, groups, len(pairs)),
        in_specs=[pl.BlockSpec((1, ATT_BLOCK, width), lambda bi, g, st, qt, kt: (bi, qt[st], g)),
                  pl.BlockSpec((1, ATT_BLOCK, width), lambda bi, g, st, qt, kt: (bi, kt[st], groups + g)),
                  pl.BlockSpec((1, hp, FOX_HEAD_DIM, ATT_BLOCK), lambda bi, g, st, qt, kt: (bi, g, 0, kt[st])),
                  pl.BlockSpec((1, hp, ATT_BLOCK, LANES), lambda bi, g, st, qt, kt: (bi, g, qt[st], 0)),
                  pl.BlockSpec((1, hp, ATT_BLOCK, LANES), lambda bi, g, st, qt, kt: (bi, g, kt[st], 0))],
        out_specs=pl.BlockSpec((1, ATT_BLOCK, width), lambda bi, g, st, qt, kt: (bi, qt[st], g)),
        scratch_shapes=[pltpu.VMEM((hp, 1, ATT_BLOCK), F32), pltpu.VMEM((hp, 1, ATT_BLOCK), F32),
                        pltpu.VMEM((hp, FOX_HEAD_DIM, ATT_BLOCK), F32)],
    )
    return pl.pallas_call(
        _fox_kernel,
        grid_spec=grid_spec,
        out_shape=jax.ShapeDtypeStruct((b, s, BRANCH_WIDTH), BF16),
        compiler_params=_params(("parallel", "parallel", "arbitrary")),
        name="fox_attention",
    )(qi_tab, ki_tab, z_a, z_a, v_t, eq, ek)


def _pool_kernel(u_ref, halo_ref, w_ref, scale_ref, o_ref):
    i = pl.program_id(1)
    rows = POOL_ROWS
    t = i * rows + lax.broadcasted_iota(jnp.int32, (rows, 1), 0)
    keep_halo = (i > 0).astype(F32)
    for g, win in enumerate(POOL_WINDOWS):
        cols = slice(g * POOL_GROUP, (g + 1) * POOL_GROUP)
        u = u_ref[0, :, cols]
        ext = jnp.concatenate([halo_ref[0, :, cols] * keep_halo, u], axis=0)
        span = 1
        while span < win:
            n = ext.shape[0]
            ext = ext[:n - span] + ext[span:]
            span *= 2
        window_sum = ext[POOL_HALO + 1 - win:POOL_HALO + 1 - win + rows]
        count = jnp.minimum(t + 1, win).astype(F32)
        d = (window_sum / count - u).astype(BF16)
        y = jnp.dot(d, w_ref[g], preferred_element_type=F32) * scale_ref[:, cols]
        o_ref[0, :, cols] = y.astype(o_ref.dtype)


def multiscale_pool(z_b, pool_w, pool_scale):
    b, s, _ = z_b.shape
    per = POOL_ROWS // POOL_HALO
    return pl.pallas_call(
        _pool_kernel,
        grid=(b, s // POOL_ROWS),
        in_specs=[pl.BlockSpec((1, POOL_ROWS, BRANCH_WIDTH), lambda bi, i: (bi, i, 0)),
                  pl.BlockSpec((1, POOL_HALO, BRANCH_WIDTH), lambda bi, i: (bi, jnp.maximum(i * per - 1, 0), 0)),
                  pl.BlockSpec((len(POOL_WINDOWS), POOL_GROUP, POOL_GROUP), lambda bi, i: (0, 0, 0)),
                  pl.BlockSpec((1, BRANCH_WIDTH), lambda bi, i: (0, 0))],
        out_specs=pl.BlockSpec((1, POOL_ROWS, BRANCH_WIDTH), lambda bi, i: (bi, i, 0)),
        out_shape=jax.ShapeDtypeStruct((b, s, BRANCH_WIDTH), BF16),
        compiler_params=_params(("parallel", "parallel")),
        name="multiscale_pool",
    )(z_b, z_b, pool_w, pool_scale.reshape(1, BRANCH_WIDTH))


def _sgu_kernel(u_ref, v_ref, nw_ref, w_ref, b_ref, o_ref):
    c = SGU_CHUNK
    v = jax.nn.gelu(v_ref[...].astype(F32))
    vc = v - jnp.mean(v, axis=-1, keepdims=True)
    vn = (vc * lax.rsqrt(jnp.mean(vc * vc, axis=-1, keepdims=True) + EPS) * nw_ref[...]).astype(BF16)
    causal = lax.broadcasted_iota(jnp.int32, (c, c), 0) >= lax.broadcasted_iota(jnp.int32, (c, c), 1)
    for g in range(SGU_GROUPS):
        cols = slice(g * SGU_GROUP_WIDTH, (g + 1) * SGU_GROUP_WIDTH)
        w = jnp.where(causal, w_ref[g], 0.0).astype(BF16)
        bias = b_ref[g]
        for n in range(SGU_ROWS // c):
            rows = slice(n * c, (n + 1) * c)
            sv = jnp.dot(w, vn[rows, cols], preferred_element_type=F32) + bias
            u = jax.nn.gelu(u_ref[rows, cols].astype(F32))
            o_ref[rows, cols] = (u * sv).astype(o_ref.dtype)


def spatial_gating(z_a, u_col0, norm_w, w_s, b_s):
    t = z_a.shape[0]
    ub = u_col0 // BRANCH_WIDTH
    return pl.pallas_call(
        _sgu_kernel,
        grid=(t // SGU_ROWS,),
        in_specs=[pl.BlockSpec((SGU_ROWS, BRANCH_WIDTH), lambda i: (i, ub)),
                  pl.BlockSpec((SGU_ROWS, BRANCH_WIDTH), lambda i: (i, ub + 1)),
                  pl.BlockSpec((1, BRANCH_WIDTH), lambda i: (0, 0)),
                  pl.BlockSpec((SGU_GROUPS, SGU_CHUNK, SGU_CHUNK), lambda i: (0, 0, 0)),
                  pl.BlockSpec((SGU_GROUPS, SGU_CHUNK, 1), lambda i: (0, 0, 0))],
        out_specs=pl.BlockSpec((SGU_ROWS, BRANCH_WIDTH), lambda i: (i, 0)),
        out_shape=jax.ShapeDtypeStruct((t, BRANCH_WIDTH), BF16),
        compiler_params=_params(("parallel",)),
        name="spatial_gating",
    )(z_a, z_a, norm_w.reshape(1, BRANCH_WIDTH), w_s, b_s.reshape(SGU_GROUPS, SGU_CHUNK, 1))


def _hgrn_kernel(q_ref, f_ref, v_ref, g_ref, lb_ref, nw_ref, o_ref, state_ref, oacc_ref):
    c = HGRN_CHUNK

    @pl.when(pl.program_id(2) == 0)
    def _():
        state_ref[...] = jnp.zeros_like(state_ref)

    lb = lb_ref[0]
    nw = nw_ref[0]
    tril = (lax.broadcasted_iota(jnp.int32, (c, c), 0) >= lax.broadcasted_iota(jnp.int32, (c, c), 1)).astype(F32)

    def chunk(ci, carry):
        rows = pl.ds(pl.multiple_of(ci * c, c), c)
        q = q_ref[0, rows, :]
        fl = f_ref[0, rows, :]
        v = v_ref[0, rows, :]
        g = g_ref[0, rows, :]
        log_f = jnp.log(lb + (1.0 - lb) * _sigmoid(fl))
        kk = (1.0 - lb) * _sigmoid(-fl)
        b = jnp.dot(tril, log_f, precision=HIGHEST, preferred_element_type=F32)
        state_t = state_ref[...]
        oacc_ref[...] = lax.dot_general((q * jnp.exp(b)).astype(BF16), state_t.astype(BF16),
                                        (((1,), (1,)), ((), ())), preferred_element_type=F32)
        for s in range(c):
            r0 = (s // 8) * 8
            t_idx = r0 + lax.broadcasted_iota(jnp.int32, (c - r0, 1), 0)
            diff = jnp.where(t_idx >= s, b[r0:, :] - b[s:s + 1, :], -jnp.inf)
            a_col = jnp.sum(q[r0:, :] * kk[s:s + 1, :] * jnp.exp(diff), axis=-1, keepdims=True)
            oacc_ref[r0:, :] += a_col * v[s:s + 1, :]
        b_end = b[c - 1:c, :]
        k_dec = (kk * jnp.exp(b_end - b)).astype(BF16)
        upd = lax.dot_general(v.astype(BF16), k_dec, (((0,), (0,)), ((), ())), preferred_element_type=F32)
        state_ref[...] = state_t * jnp.exp(b_end) + upd
        o = oacc_ref[...]
        o = o * lax.rsqrt(jnp.mean(o * o, axis=-1, keepdims=True) + EPS) * nw
        o_ref[0, rows, :] = (o * (g * _sigmoid(g))).astype(o_ref.dtype)
        return carry

    lax.fori_loop(0, HGRN_ROWS // c, chunk, 0)


def hgrn2(z_b, col0, lower_bound, norm_w):
    b, s, _ = z_b.shape
    c0 = col0 // LANES
    nh = HGRN_HEADS

    def spec(part):
        return pl.BlockSpec((1, HGRN_ROWS, LANES), lambda bi, h, i: (bi, i, c0 + part * nh + h))

    vec = pl.BlockSpec((1, 1, LANES), lambda bi, h, i: (h, 0, 0))
    return pl.pallas_call(
        _hgrn_kernel,
        grid=(b, nh, s // HGRN_ROWS),
        in_specs=[spec(0), spec(1), spec(2), spec(3), vec, vec],
        out_specs=pl.BlockSpec((1, HGRN_ROWS, LANES), lambda bi, h, i: (bi, i, h)),
        out_shape=jax.ShapeDtypeStruct((b, s, BRANCH_WIDTH), BF16),
        scratch_shapes=[pltpu.VMEM((HGRN_VAL_DIM, HGRN_KEY_DIM), F32), pltpu.VMEM((HGRN_CHUNK, HGRN_VAL_DIM), F32)],
        compiler_params=_params(("parallel", "parallel", "arbitrary")),
        name="hgrn2",
    )(z_b, z_b, z_b, z_b, lower_bound.reshape(nh, 1, LANES), norm_w.reshape(nh, 1, LANES))


def _router_kernel(x_ref, w_ref, wr_ref, br_ref, n_ref, meta_ref, p_ref, cnt_ref, carry_ref):
    @pl.when(pl.program_id(0) == 0)
    def _():
        carry_ref[...] = jnp.zeros_like(carry_ref)

    n = _rms(x_ref[...], w_ref[...])
    n_ref[...] = n
    logits = jnp.dot(n, wr_ref[...], precision=HIGHEST, preferred_element_type=F32) + br_ref[...]
    rows = logits.shape[0]
    col = lax.broadcasted_iota(jnp.int32, logits.shape, 1)
    big = jnp.int32(LANES)
    neg = -jnp.inf

    def first_argmax(vals):
        top = jnp.max(vals, axis=-1, keepdims=True)
        return top, jnp.min(jnp.where(vals == top, col, big), axis=-1, keepdims=True)

    group_logits = jnp.where(col < N_GROUPS, logits, neg)
    g_top, g_idx = first_argmax(group_logits)
    p_group = 1.0 / jnp.sum(jnp.exp(group_logits - g_top), axis=-1, keepdims=True)
    lo = N_GROUPS + g_idx * EXPERTS_PER_GROUP
    in_group = jnp.where((col >= lo) & (col < lo + EXPERTS_PER_GROUP), logits, neg)
    top1, idx1 = first_argmax(in_group)
    top2, idx2 = first_argmax(jnp.where(col == idx1, neg, in_group))
    r = jnp.exp(top2 - top1)
    p1 = p_group / (1.0 + r)
    p2 = p_group * r / (1.0 + r)
    p_ref[...] = jnp.where(col == 0, p1, jnp.where(col == 1, p2, 0.0))

    chosen = ((col == idx1) | (col == idx2)).astype(BF16)
    earlier = (lax.broadcasted_iota(jnp.int32, (rows, rows), 0)
               > lax.broadcasted_iota(jnp.int32, (rows, rows), 1)).astype(BF16)
    before = jnp.dot(earlier, chosen, preferred_element_type=F32) + carry_ref[...]
    rank1 = jnp.sum(jnp.where(col == idx1, before, 0.0), axis=-1, keepdims=True).astype(jnp.int32)
    rank2 = jnp.sum(jnp.where(col == idx2, before, 0.0), axis=-1, keepdims=True).astype(jnp.int32)
    carry_ref[...] += jnp.sum(chosen.astype(F32), axis=0, keepdims=True)
    cnt_ref[...] = carry_ref[...]
    meta_ref[...] = jnp.where(col == 0, idx1 - N_GROUPS, jnp.where(col == 1, idx2 - N_GROUPS,
                              jnp.where(col == 2, rank1, jnp.where(col == 3, rank2, 0))))


def moe_router(h, norm_w, w_router, b_router):
    t, d = h.shape
    row = pl.BlockSpec((NORM_ROWS, d), lambda i: (i, 0))
    small = pl.BlockSpec((NORM_ROWS, LANES), lambda i: (i, 0))
    one = pl.BlockSpec((1, LANES), lambda i: (0, 0))
    return pl.pallas_call(
        _router_kernel,
        grid=(t // NORM_ROWS,),
        in_specs=[row, pl.BlockSpec((1, d), lambda i: (0, 0)), pl.BlockSpec((d, LANES), lambda i: (0, 0)), one],
        out_specs=[row, small, small, one],
        out_shape=[jax.ShapeDtypeStruct((t, d), F32), jax.ShapeDtypeStruct((t, LANES), jnp.int32),
                   jax.ShapeDtypeStruct((t, LANES), F32), jax.ShapeDtypeStruct((1, LANES), F32)],
        scratch_shapes=[pltpu.VMEM((1, LANES), F32)],
        compiler_params=_params(("arbitrary",)),
        name="moe_router",
    )(h, norm_w.reshape(1, d), w_router, b_router)


def _gather_rows(idx_ref, n_rows, src_hbm, dst_ref, sem):
    def body(r, carry):
        pltpu.make_async_copy(src_hbm.at[pl.ds(idx_ref[r], 1)], dst_ref.at[pl.ds(r, 1)], sem).start()
        return carry
    lax.fori_loop(0, n_rows, body, 0)


def _wait_rows(src_hbm, dst_ref, sem):
    pltpu.make_async_copy(src_hbm.at[pl.ds(0, dst_ref.shape[0])], dst_ref, sem).wait()


def _expert_kernel(be_ref, nused_ref, tok_cur_ref, tok_next_ref, x_hbm, wg_ref, wu_ref, wd_ref, y_ref,
                   xbuf_ref, sem_ref):
    i = pl.program_id(0)
    n_used = nused_ref[0]
    slot = i % 2

    @pl.when((i == 0) & (n_used > 0))
    def _():
        _gather_rows(tok_cur_ref, MOE_ROWS, x_hbm, xbuf_ref.at[0], sem_ref.at[0])

    @pl.when(i + 1 < n_used)
    def _():
        _gather_rows(tok_next_ref, MOE_ROWS, x_hbm, xbuf_ref.at[1 - slot], sem_ref.at[1 - slot])

    @pl.when(i < n_used)
    def _():
        _wait_rows(x_hbm, xbuf_ref.at[slot], sem_ref.at[slot])
        x = xbuf_ref[slot].astype(BF16)
        gate = jnp.dot(x, wg_ref[0], preferred_element_type=F32)
        up = jnp.dot(x, wu_ref[0], preferred_element_type=F32)
        hidden = (gate * _sigmoid(gate) * up).astype(BF16)
        y_ref[...] = jnp.dot(hidden, wd_ref[0], preferred_element_type=F32)

    @pl.when(i >= n_used)
    def _():
        y_ref[...] = jnp.zeros_like(y_ref)


def expert_mlp(x, buf_tok, block_e, n_used, w_gate, w_up, w_down):
    p = buf_tok.shape[0]
    nb = p // MOE_ROWS
    d = x.shape[1]
    smem = pltpu.SMEM
    grid_spec = pltpu.PrefetchScalarGridSpec(
        num_scalar_prefetch=2,
        grid=(nb,),
        in_specs=[pl.BlockSpec((MOE_ROWS,), lambda i, be, nu: (i,), memory_space=smem),
                  pl.BlockSpec((MOE_ROWS,), lambda i, be, nu: (jnp.minimum(i + 1, nb - 1),), memory_space=smem),
                  pl.BlockSpec(memory_space=pl.ANY),
                  pl.BlockSpec((1, d, EXPERT_HIDDEN), lambda i, be, nu: (be[i], 0, 0)),
                  pl.BlockSpec((1, d, EXPERT_HIDDEN), lambda i, be, nu: (be[i], 0, 0)),
                  pl.BlockSpec((1, EXPERT_HIDDEN, d), lambda i, be, nu: (be[i], 0, 0))],
        out_specs=pl.BlockSpec((MOE_ROWS, d), lambda i, be, nu: (i, 0)),
        scratch_shapes=[pltpu.VMEM((2, MOE_ROWS, d), F32), pltpu.SemaphoreType.DMA((2,))],
    )
    return pl.pallas_call(
        _expert_kernel,
        grid_spec=grid_spec,
        out_shape=jax.ShapeDtypeStruct((p, d), F32),
        compiler_params=_params(("arbitrary",)),
        name="expert_mlp",
    )(block_e, n_used, buf_tok, buf_tok, x, w_gate, w_up, w_down)


def _combine_kernel(pos0_ref, pos1_ref, y_hbm, h_ref, p_ref, w_ref, o_ref, n_ref, ybuf_ref, sem_ref):
    rows = COMBINE_ROWS
    _gather_rows(pos0_ref, rows, y_hbm, ybuf_ref.at[0], sem_ref.at[0])
    _gather_rows(pos1_ref, rows, y_hbm, ybuf_ref.at[1], sem_ref.at[1])
    _wait_rows(y_hbm, ybuf_ref.at[0], sem_ref.at[0])
    _wait_rows(y_hbm, ybuf_ref.at[1], sem_ref.at[1])
    p = p_ref[...]
    h = h_ref[...] + (ybuf_ref[0] * p[:, 0:1] + ybuf_ref[1] * p[:, 1:2])
    o_ref[...] = h
    n_ref[...] = _rms(h, w_ref[...]).astype(n_ref.dtype)


def moe_combine(y, pos0, pos1, probs, h, norm_w):
    t, d = h.shape
    smem = pltpu.SMEM
    row = pl.BlockSpec((COMBINE_ROWS, d), lambda i: (i, 0))
    return pl.pallas_call(
        _combine_kernel,
        grid=(t // COMBINE_ROWS,),
        in_specs=[pl.BlockSpec((COMBINE_ROWS,), lambda i: (i,), memory_space=smem),
                  pl.BlockSpec((COMBINE_ROWS,), lambda i: (i,), memory_space=smem),
                  pl.BlockSpec(memory_space=pl.ANY),
                  row,
                  pl.BlockSpec((COMBINE_ROWS, LANES), lambda i: (i, 0)),
                  pl.BlockSpec((1, d), lambda i: (0, 0))],
        out_specs=[row, row],
        out_shape=[jax.ShapeDtypeStruct((t, d), F32), jax.ShapeDtypeStruct((t, d), BF16)],
        scratch_shapes=[pltpu.VMEM((2, COMBINE_ROWS, d), F32), pltpu.SemaphoreType.DMA((2,))],
        compiler_params=_params(("arbitrary",)),
        name="moe_combine",
    )(pos0, pos1, y, h, probs, norm_w.reshape(1, d))


def _dispatch_plan(meta, counts):
    t = meta.shape[0]
    a = t * TOP_K
    expert_ids = meta[:, :TOP_K]
    ranks = meta[:, TOP_K:2 * TOP_K]
    counts = counts[0, N_GROUPS:N_GROUPS + N_EXPERTS].astype(jnp.int32)
    padded = (counts + MOE_ROWS - 1) // MOE_ROWS * MOE_ROWS
    pad_end = jnp.cumsum(padded)
    pad_start = pad_end - padded
    dest = pad_start[expert_ids] + ranks
    p = a + N_EXPERTS * MOE_ROWS
    nb = p // MOE_ROWS
    token_ids = jnp.repeat(jnp.arange(t, dtype=jnp.int32), TOP_K)
    buf_tok = jnp.zeros((p,), jnp.int32).at[dest.reshape(a)].set(token_ids)
    block_e = jnp.clip(jnp.searchsorted(pad_end, jnp.arange(nb, dtype=jnp.int32) * MOE_ROWS, side='right'),
                       0, N_EXPERTS - 1).astype(jnp.int32)
    n_used = (pad_end[-1:] // MOE_ROWS).astype(jnp.int32)
    return buf_tok, block_e, n_used, dest[:, 0], dest[:, 1]


def _pad_cols(w, n):
    return jnp.pad(w, ((0, 0), (0, n - w.shape[1])))


def _layer_weights(i, w_in, router_group_w, router_group_b, router_expert_w, router_expert_b):
    wi = w_in[i]
    w_a = jnp.concatenate([wi[:, :FOX_QKV_END], wi[:, POOL_END:SGU_END], wi[:, HGRN_END:]], axis=1).astype(BF16)
    w_b = jnp.concatenate([wi[:, FOX_F_END:POOL_END], wi[:, SGU_END:HGRN_END]], axis=1).astype(BF16)
    w_f = _pad_cols(wi[:, FOX_QKV_END:FOX_F_END], LANES).astype(BF16)
    w_r = _pad_cols(jnp.concatenate([router_group_w[i], router_expert_w[i]], axis=1), LANES)
    b_r = _pad_cols(jnp.concatenate([router_group_b[i], router_expert_b[i]])[None, :], LANES)
    return w_a, w_b, w_f, w_r, b_r


def kernel(x, p, mix_norm_w, w_in, fox_f_bias, pool_w, pool_scale, sgu_norm_w, sgu_w, sgu_b,
           hgrn_lb_logits, hgrn_norm_w, branch_proj, w_out, ffn_norm_w, router_group_w,
           router_group_b, router_expert_w, router_expert_b, expert_w_gate, expert_w_up,
           expert_w_down, ple_norm_w, ple_gate_w, ple_proj_w, final_norm_w):
    bsz, seq, d = x.shape
    t = bsz * seq
    lb_p = jax.nn.softmax(hgrn_lb_logits.astype(F32), axis=0)
    lower_bounds = jnp.cumsum(lb_p, axis=0) - lb_p[0:1]
    sgu_u_col = FOX_QKV_END
    gate_col = FOX_QKV_END + 2 * BRANCH_WIDTH
    h = x.reshape(t, d)
    n_a = FOX_QKV_END + 2 * BRANCH_WIDTH + N_BRANCHES * D_MODEL
    q_scale = jnp.where(jnp.arange(n_a) < BRANCH_WIDTH, FOX_HEAD_DIM ** -0.5 * LOG2E, 1.0).astype(F32)[None, :]
    for i in range(DEPTH):
        w_a, w_b, w_f, w_r, b_r = _layer_weights(i, w_in, router_group_w, router_group_b,
                                                 router_expert_w, router_expert_b)
        n, z_f = rms_norm_proj(h, mix_norm_w[i], w_f, BF16, exact_proj=False)
        z_a = matmul(n, w_a, q_scale, BF16)
        z_b = matmul(n, w_b, jnp.ones((1, w_b.shape[1]), F32), F32)
        z_a3 = z_a.reshape(bsz, seq, -1)
        z_b3 = z_b.reshape(bsz, seq, -1)

        f_bias = _pad_cols(fox_f_bias[i][None, :].astype(F32), LANES)
        eq, ek = forget_prep(z_f.reshape(bsz, seq, LANES), f_bias)
        v_t = jnp.transpose(z_a3[:, :, 2 * BRANCH_WIDTH:FOX_QKV_END].reshape(bsz, seq, FOX_HEADS, FOX_HEAD_DIM),
                            (0, 2, 3, 1))
        br_fox = fox_attention(z_a3, v_t, eq, ek).reshape(t, BRANCH_WIDTH)
        br_pool = multiscale_pool(z_b3, pool_w[i].astype(BF16), pool_scale[i]).reshape(t, BRANCH_WIDTH)
        br_sgu = spatial_gating(z_a, sgu_u_col, sgu_norm_w[i], sgu_w[i], sgu_b[i])
        br_hgrn = hgrn2(z_b3, BRANCH_WIDTH, lower_bounds[i], hgrn_norm_w[i]).reshape(t, BRANCH_WIDTH)

        merged = merge_branches((br_fox, br_pool, br_sgu, br_hgrn), branch_proj[i].astype(BF16), z_a, gate_col)
        h = matmul_residual(merged, w_out[i].astype(BF16), h)

        xn, meta, probs, counts = moe_router(h, ffn_norm_w[i], w_r, b_r)
        buf_tok, block_e, n_used, pos0, pos1 = _dispatch_plan(meta, counts)
        y = expert_mlp(xn, buf_tok, block_e, n_used, expert_w_gate[i].astype(BF16),
                       expert_w_up[i].astype(BF16), expert_w_down[i].astype(BF16))
        h, n_ple = moe_combine(y, pos0, pos1, probs, h, ple_norm_w[i])

        h = ple_update(n_ple, ple_gate_w[i].astype(BF16), p[i].reshape(t, PLE_DIM).astype(BF16),
                       ple_proj_w[i].astype(BF16), h)
    return rms_norm(h, final_norm_w, F32).reshape(bsz, seq, d)
```

```python
import functools

import jax
import jax.numpy as jnp
from jax import lax
from jax.experimental import pallas as pl
from jax.experimental.pallas import tpu as pltpu

F32 = jnp.float32
BF16 = jnp.bfloat16
HIGHEST = lax.Precision.HIGHEST

D_MODEL = 2048
DEPTH = 2
BRANCH_WIDTH = D_MODEL // 2
N_BRANCHES = 4
PLE_DIM = 256
EPS = 1e-6
FOX_HEAD_DIM = 128
FOX_HEADS = BRANCH_WIDTH // FOX_HEAD_DIM
POOL_WINDOWS = (2, 4, 8, 16)
POOL_GROUP = BRANCH_WIDTH // len(POOL_WINDOWS)
POOL_HALO = 16
SGU_GROUPS = 4
SGU_CHUNK = 128
SGU_GROUP_WIDTH = BRANCH_WIDTH // SGU_GROUPS
HGRN_KEY_DIM = 128
HGRN_VAL_DIM = 128
HGRN_HEADS = BRANCH_WIDTH // HGRN_VAL_DIM
HGRN_SUBCHUNK = 16
HGRN_HEADS_PER_STEP = 2
N_GROUPS = 4
EXPERTS_PER_GROUP = 8
N_EXPERTS = N_GROUPS * EXPERTS_PER_GROUP
TOP_K = 2
EXPERT_HIDDEN = D_MODEL // 2
FOX_QKV_END = 3 * BRANCH_WIDTH
FOX_F_END = FOX_QKV_END + FOX_HEADS
POOL_END = FOX_F_END + BRANCH_WIDTH
SGU_END = POOL_END + 2 * BRANCH_WIDTH
HGRN_END = SGU_END + 4 * BRANCH_WIDTH
IN_COLS = HGRN_END + N_BRANCHES * D_MODEL

LANES = 128
LOG2E = 1.4426950408889634
FOX_HEADS_PER_STEP = 2
FOX_QUERY_CHUNK = 256
VMEM_LIMIT = 56 * 1024 * 1024

NORM_ROWS = 512
MM_ROWS = 2048
MM_COLS = 512
MERGE_ROWS = 1024
ATT_BLOCK = 512
CUMSUM_BLOCK = 512
POOL_ROWS = 512
SGU_ROWS = 512
HGRN_ROWS = 256
MOE_ROWS = 256
COMBINE_ROWS = 256
PLE_ROWS = 1024


def _params(semantics, **kw):
    return pltpu.CompilerParams(dimension_semantics=semantics, vmem_limit_bytes=VMEM_LIMIT, **kw)


def _sigmoid(x):
    return 1.0 / (1.0 + jnp.exp(-x))


def _log_sigmoid(x):
    return jnp.minimum(x, 0.0) - jnp.log(1.0 + jnp.exp(-jnp.abs(x)))


def _rms(x, w):
    return x * lax.rsqrt(jnp.mean(x * x, axis=-1, keepdims=True) + EPS) * w


def _norm_kernel(x_ref, w_ref, n_ref):
    n_ref[...] = _rms(x_ref[...], w_ref[...]).astype(n_ref.dtype)


def rms_norm(x, w, out_dtype):
    t, d = x.shape
    return pl.pallas_call(
        _norm_kernel,
        grid=(t // NORM_ROWS,),
        in_specs=[pl.BlockSpec((NORM_ROWS, d), lambda i: (i, 0)),
                  pl.BlockSpec((1, d), lambda i: (0, 0))],
        out_specs=pl.BlockSpec((NORM_ROWS, d), lambda i: (i, 0)),
        out_shape=jax.ShapeDtypeStruct((t, d), out_dtype),
        compiler_params=_params(("parallel",)),
        name="rms_norm",
    )(x, w.reshape(1, d))


def _norm_proj_kernel(x_ref, w_ref, wp_ref, n_ref, p_ref, *, exact_proj):
    n = _rms(x_ref[...], w_ref[...])
    n_ref[...] = n.astype(n_ref.dtype)
    if exact_proj:
        p_ref[...] = jnp.dot(n, wp_ref[...], precision=HIGHEST, preferred_element_type=F32)
    else:
        p_ref[...] = jnp.dot(n.astype(BF16), wp_ref[...], preferred_element_type=F32)


def rms_norm_proj(x, w, wp, out_dtype, exact_proj):
    t, d = x.shape
    return pl.pallas_call(
        functools.partial(_norm_proj_kernel, exact_proj=exact_proj),
        grid=(t // NORM_ROWS,),
        in_specs=[pl.BlockSpec((NORM_ROWS, d), lambda i: (i, 0)),
                  pl.BlockSpec((1, d), lambda i: (0, 0)),
                  pl.BlockSpec((d, LANES), lambda i: (0, 0))],
        out_specs=[pl.BlockSpec((NORM_ROWS, d), lambda i: (i, 0)),
                   pl.BlockSpec((NORM_ROWS, LANES), lambda i: (i, 0))],
        out_shape=[jax.ShapeDtypeStruct((t, d), out_dtype),
                   jax.ShapeDtypeStruct((t, LANES), F32)],
        compiler_params=_params(("parallel",)),
        name="rms_norm_proj",
    )(x, w.reshape(1, d), wp)


def _mm_kernel(a_ref, w_ref, s_ref, o_ref):
    acc = jnp.dot(a_ref[...], w_ref[...], preferred_element_type=F32)
    o_ref[...] = (acc * s_ref[...]).astype(o_ref.dtype)


def matmul(a, w, col_scale, out_dtype):
    m, k = a.shape
    n = w.shape[1]
    return pl.pallas_call(
        _mm_kernel,
        grid=(m // MM_ROWS, n // MM_COLS),
        in_specs=[pl.BlockSpec((MM_ROWS, k), lambda i, j: (i, 0)),
                  pl.BlockSpec((k, MM_COLS), lambda i, j: (0, j)),
                  pl.BlockSpec((1, MM_COLS), lambda i, j: (0, j))],
        out_specs=pl.BlockSpec((MM_ROWS, MM_COLS), lambda i, j: (i, j)),
        out_shape=jax.ShapeDtypeStruct((m, n), out_dtype),
        compiler_params=_params(("parallel", "arbitrary")),
        name="matmul",
    )(a, w, col_scale)


def _mm_res_kernel(a_ref, w_ref, r_ref, o_ref):
    o_ref[...] = r_ref[...] + jnp.dot(a_ref[...], w_ref[...], preferred_element_type=F32)


def matmul_residual(a, w, res):
    m, k = a.shape
    n = w.shape[1]
    return pl.pallas_call(
        _mm_res_kernel,
        grid=(m // MM_ROWS, n // MM_COLS),
        in_specs=[pl.BlockSpec((MM_ROWS, k), lambda i, j: (i, 0)),
                  pl.BlockSpec((k, MM_COLS), lambda i, j: (0, j)),
                  pl.BlockSpec((MM_ROWS, MM_COLS), lambda i, j: (i, j))],
        out_specs=pl.BlockSpec((MM_ROWS, MM_COLS), lambda i, j: (i, j)),
        out_shape=jax.ShapeDtypeStruct((m, n), F32),
        compiler_params=_params(("parallel", "arbitrary")),
        name="matmul_residual",
    )(a, w, res)


def _merge_kernel(b0_ref, b1_ref, b2_ref, b3_ref, p_ref, g0_ref, g1_ref, g2_ref, g3_ref, o_ref):
    acc = None
    for bi, (b_ref, g_ref) in enumerate(((b0_ref, g0_ref), (b1_ref, g1_ref),
                                         (b2_ref, g2_ref), (b3_ref, g3_ref))):
        y = jnp.dot(b_ref[...], p_ref[bi], preferred_element_type=F32)
        y = _sigmoid(g_ref[...].astype(F32)) * y
        acc = y if acc is None else acc + y
    o_ref[...] = acc.astype(o_ref.dtype)


def merge_branches(branches, proj, z_a, gate_col0):
    t = branches[0].shape[0]
    g0 = gate_col0 // MM_COLS
    per = D_MODEL // MM_COLS
    b_spec = pl.BlockSpec((MERGE_ROWS, BRANCH_WIDTH), lambda i, j: (i, 0))
    g_specs = [pl.BlockSpec((MERGE_ROWS, MM_COLS), functools.partial(lambda i, j, o: (i, o + j), o=g0 + bi * per))
               for bi in range(N_BRANCHES)]
    return pl.pallas_call(
        _merge_kernel,
        grid=(t // MERGE_ROWS, per),
        in_specs=[b_spec] * N_BRANCHES
        + [pl.BlockSpec((N_BRANCHES, BRANCH_WIDTH, MM_COLS), lambda i, j: (0, 0, j))] + g_specs,
        out_specs=pl.BlockSpec((MERGE_ROWS, MM_COLS), lambda i, j: (i, j)),
        out_shape=jax.ShapeDtypeStruct((t, D_MODEL), BF16),
        compiler_params=_params(("parallel", "arbitrary")),
        name="merge_branches",
    )(*branches, proj, z_a, z_a, z_a, z_a)


def _ple_kernel(n_ref, wg_ref, p_ref, wp_ref, h_ref, o_ref):
    gate = _sigmoid(jnp.dot(n_ref[...], wg_ref[...], preferred_element_type=F32))
    emb = jnp.dot(p_ref[...], wp_ref[...], preferred_element_type=F32)
    o_ref[...] = h_ref[...] + gate * emb


def ple_update(n, wg, p, wp, h):
    t = h.shape[0]
    return pl.pallas_call(
        _ple_kernel,
        grid=(t // PLE_ROWS, D_MODEL // MM_COLS),
        in_specs=[pl.BlockSpec((PLE_ROWS, D_MODEL), lambda i, j: (i, 0)),
                  pl.BlockSpec((D_MODEL, MM_COLS), lambda i, j: (0, j)),
                  pl.BlockSpec((PLE_ROWS, PLE_DIM), lambda i, j: (i, 0)),
                  pl.BlockSpec((PLE_DIM, MM_COLS), lambda i, j: (0, j)),
                  pl.BlockSpec((PLE_ROWS, MM_COLS), lambda i, j: (i, j))],
        out_specs=pl.BlockSpec((PLE_ROWS, MM_COLS), lambda i, j: (i, j)),
        out_shape=jax.ShapeDtypeStruct((t, D_MODEL), F32),
        compiler_params=_params(("parallel", "arbitrary")),
        name="ple_update",
    )(n, wg, p, wp, h)


def _bf16_split3(x):
    hi = x.astype(BF16).astype(F32)
    r = x - hi
    mid = r.astype(BF16).astype(F32)
    lo = (r - mid).astype(BF16).astype(F32)
    return hi, mid, lo


def _forget_prep_kernel(z_ref, b_ref, eq_ref, ek_ref, carry_ref):
    @pl.when(pl.program_id(1) == 0)
    def _():
        carry_ref[...] = jnp.zeros_like(carry_ref)

    lf = _log_sigmoid(z_ref[0] + b_ref[...]) * LOG2E
    n = lf.shape[0]
    tril = (lax.broadcasted_iota(jnp.int32, (n, n), 0) >= lax.broadcasted_iota(jnp.int32, (n, n), 1)).astype(F32)
    c = jnp.dot(tril, lf, precision=HIGHEST, preferred_element_type=F32) + carry_ref[...]
    carry_ref[...] = c[n - 1:n, :]
    parts = _bf16_split3(c)
    lane = lax.broadcasted_iota(jnp.int32, (n, LANES), 1)
    for h in range(FOX_HEADS):
        hi, mid, lo = (jnp.sum(jnp.where(lane == h, part, 0.0), axis=-1, keepdims=True) for part in parts)
        eq = jnp.where(lane == 0, hi, jnp.where(lane == 1, mid, jnp.where(lane == 2, lo,
                       jnp.where(lane < 6, 1.0, 0.0))))
        ek = jnp.where(lane < 3, 1.0, jnp.where(lane == 3, -hi, jnp.where(lane == 4, -mid,
                       jnp.where(lane == 5, -lo, 0.0))))
        eq_ref[0, h] = eq.astype(BF16)
        ek_ref[0, h] = ek.astype(BF16)


def forget_prep(z_f, bias):
    b, s, _ = z_f.shape
    out = pl.BlockSpec((1, FOX_HEADS, CUMSUM_BLOCK, LANES), lambda bi, i: (bi, 0, i, 0))
    shape = jax.ShapeDtypeStruct((b, FOX_HEADS, s, LANES), BF16)
    return pl.pallas_call(
        _forget_prep_kernel,
        grid=(b, s // CUMSUM_BLOCK),
        in_specs=[pl.BlockSpec((1, CUMSUM_BLOCK, LANES), lambda bi, i: (bi, i, 0)),
                  pl.BlockSpec((1, LANES), lambda bi, i: (0, 0))],
        out_specs=[out, out],
        out_shape=[shape, shape],
        scratch_shapes=[pltpu.VMEM((1, LANES), F32)],
        compiler_params=_params(("parallel", "arbitrary")),
        name="forget_prep",
    )(z_f, bias)


def _fox_kernel(qi_ref, ki_ref, q_ref, k_ref, vt_ref, eq_ref, ek_ref, o_ref, m_ref, l_ref, acc_ref):
    step = pl.program_id(2)
    qi = qi_ref[step]
    ki = ki_ref[step]
    blk = ATT_BLOCK
    dh = FOX_HEAD_DIM
    qc = FOX_QUERY_CHUNK

    @pl.when(ki == 0)
    def _():
        m_ref[...] = jnp.full_like(m_ref, -jnp.inf)
        l_ref[...] = jnp.zeros_like(l_ref)
        acc_ref[...] = jnp.zeros_like(acc_ref)

    def update(masked):
        for hh in range(FOX_HEADS_PER_STEP):
            cols = slice(hh * dh, (hh + 1) * dh)
            k = jnp.concatenate([k_ref[0, :, cols], ek_ref[0, hh]], axis=1)
            for c in range(blk // qc):
                qs = slice(c * qc, (c + 1) * qc)
                q = jnp.concatenate([q_ref[0, qs, cols], eq_ref[0, hh, qs, :]], axis=1)
                s = lax.dot_general(k, q, (((1,), (1,)), ((), ())), preferred_element_type=F32)
                if masked:
                    key = lax.broadcasted_iota(jnp.int32, (blk, qc), 0)
                    qry = lax.broadcasted_iota(jnp.int32, (blk, qc), 1) + c * qc
                    s = jnp.where(key <= qry, s, -jnp.inf)
                m_old = m_ref[hh, :, qs]
                m_new = jnp.maximum(m_old, jnp.max(s, axis=0, keepdims=True))
                alpha = jnp.exp2(m_old - m_new)
                p = jnp.exp2(s - m_new)
                l_ref[hh, :, qs] = alpha * l_ref[hh, :, qs] + jnp.sum(p, axis=0, keepdims=True)
                acc_ref[hh, :, qs] = alpha * acc_ref[hh, :, qs] + jnp.dot(
                    vt_ref[0, hh], p.astype(BF16), preferred_element_type=F32)
                m_ref[hh, :, qs] = m_new

    @pl.when(ki < qi)
    def _():
        update(False)

    @pl.when(ki == qi)
    def _():
        update(True)
        for hh in range(FOX_HEADS_PER_STEP):
            o_ref[0, :, hh * dh:(hh + 1) * dh] = jnp.transpose(acc_ref[hh] / l_ref[hh]).astype(o_ref.dtype)


def fox_attention(z_a, v_t, eq, ek):
    b, s, _ = z_a.shape
    nb = s // ATT_BLOCK
    hp = FOX_HEADS_PER_STEP
    groups = FOX_HEADS // hp
    pairs = [(qi, ki) for qi in range(nb) for ki in range(qi + 1)]
    qi_tab = jnp.asarray([pr[0] for pr in pairs], jnp.int32)
    ki_tab = jnp.asarray([pr[1] for pr in pairs], jnp.int32)
    width = hp * FOX_HEAD_DIM
    grid_spec = pltpu.PrefetchScalarGridSpec(
        num_scalar_prefetch=2,
        grid=(b, groups, len(pairs)),
        in_specs=[pl.BlockSpec((1, ATT_BLOCK, width), lambda bi, g, st, qt, kt: (bi, qt[st], g)),
                  pl.BlockSpec((1, ATT_BLOCK, width), lambda bi, g, st, qt, kt: (bi, kt[st], groups + g)),
                  pl.BlockSpec((1, hp, FOX_HEAD_DIM, ATT_BLOCK), lambda bi, g, st, qt, kt: (bi, g, 0, kt[st])),
                  pl.BlockSpec((1, hp, ATT_BLOCK, LANES), lambda bi, g, st, qt, kt: (bi, g, qt[st], 0)),
                  pl.BlockSpec((1, hp, ATT_BLOCK, LANES), lambda bi, g, st, qt, kt: (bi, g, kt[st], 0))],
        out_specs=pl.BlockSpec((1, ATT_BLOCK, width), lambda bi, g, st, qt, kt: (bi, qt[st], g)),
        scratch_shapes=[pltpu.VMEM((hp, 1, ATT_BLOCK), F32), pltpu.VMEM((hp, 1, ATT_BLOCK), F32),
                        pltpu.VMEM((hp, FOX_HEAD_DIM, ATT_BLOCK), F32)],
    )
    return pl.pallas_call(
        _fox_kernel,
        grid_spec=grid_spec,
        out_shape=jax.ShapeDtypeStruct((b, s, BRANCH_WIDTH), BF16),
        compiler_params=_params(("parallel", "parallel", "arbitrary")),
        name="fox_attention",
    )(qi_tab, ki_tab, z_a, z_a, v_t, eq, ek)


def _pool_kernel(u_ref, halo_ref, w_ref, scale_ref, o_ref):
    i = pl.program_id(1)
    rows = POOL_ROWS
    t = i * rows + lax.broadcasted_iota(jnp.int32, (rows, 1), 0)
    keep_halo = (i > 0).astype(F32)
    for g, win in enumerate(POOL_WINDOWS):
        cols = slice(g * POOL_GROUP, (g + 1) * POOL_GROUP)
        u = u_ref[0, :, cols]
        ext = jnp.concatenate([halo_ref[0, :, cols] * keep_halo, u], axis=0)
        span = 1
        while span < win:
            n = ext.shape[0]
            ext = ext[:n - span] + ext[span:]
            span *= 2
        window_sum = ext[POOL_HALO + 1 - win:POOL_HALO + 1 - win + rows]
        count = jnp.minimum(t + 1, win).astype(F32)
        d = (window_sum / count - u).astype(BF16)
        y = jnp.dot(d, w_ref[g], preferred_element_type=F32) * scale_ref[:, cols]
        o_ref[0, :, cols] = y.astype(o_ref.dtype)


def multiscale_pool(z_b, pool_w, pool_scale):
    b, s, _ = z_b.shape
    per = POOL_ROWS // POOL_HALO
    return pl.pallas_call(
        _pool_kernel,
        grid=(b, s // POOL_ROWS),
        in_specs=[pl.BlockSpec((1, POOL_ROWS, BRANCH_WIDTH), lambda bi, i: (bi, i, 0)),
                  pl.BlockSpec((1, POOL_HALO, BRANCH_WIDTH), lambda bi, i: (bi, jnp.maximum(i * per - 1, 0), 0)),
                  pl.BlockSpec((len(POOL_WINDOWS), POOL_GROUP, POOL_GROUP), lambda bi, i: (0, 0, 0)),
                  pl.BlockSpec((1, BRANCH_WIDTH), lambda bi, i: (0, 0))],
        out_specs=pl.BlockSpec((1, POOL_ROWS, BRANCH_WIDTH), lambda bi, i: (bi, i, 0)),
        out_shape=jax.ShapeDtypeStruct((b, s, BRANCH_WIDTH), BF16),
        compiler_params=_params(("parallel", "parallel")),
        name="multiscale_pool",
    )(z_b, z_b, pool_w, pool_scale.reshape(1, BRANCH_WIDTH))


def _sgu_kernel(u_ref, v_ref, nw_ref, w_ref, b_ref, o_ref):
    c = SGU_CHUNK
    v = jax.nn.gelu(v_ref[...].astype(F32))
    vc = v - jnp.mean(v, axis=-1, keepdims=True)
    vn = (vc * lax.rsqrt(jnp.mean(vc * vc, axis=-1, keepdims=True) + EPS) * nw_ref[...]).astype(BF16)
    causal = lax.broadcasted_iota(jnp.int32, (c, c), 0) >= lax.broadcasted_iota(jnp.int32, (c, c), 1)
    for g in range(SGU_GROUPS):
        cols = slice(g * SGU_GROUP_WIDTH, (g + 1) * SGU_GROUP_WIDTH)
        w = jnp.where(causal, w_ref[g], 0.0).astype(BF16)
        bias = b_ref[g]
        for n in range(SGU_ROWS // c):
            rows = slice(n * c, (n + 1) * c)
            sv = jnp.dot(w, vn[rows, cols], preferred_element_type=F32) + bias
            u = jax.nn.gelu(u_ref[rows, cols].astype(F32))
            o_ref[rows, cols] = (u * sv).astype(o_ref.dtype)


def spatial_gating(z_a, u_col0, norm_w, w_s, b_s):
    t = z_a.shape[0]
    ub = u_col0 // BRANCH_WIDTH
    return pl.pallas_call(
        _sgu_kernel,
        grid=(t // SGU_ROWS,),
        in_specs=[pl.BlockSpec((SGU_ROWS, BRANCH_WIDTH), lambda i: (i, ub)),
                  pl.BlockSpec((SGU_ROWS, BRANCH_WIDTH), lambda i: (i, ub + 1)),
                  pl.BlockSpec((1, BRANCH_WIDTH), lambda i: (0, 0)),
                  pl.BlockSpec((SGU_GROUPS, SGU_CHUNK, SGU_CHUNK), lambda i: (0, 0, 0)),
                  pl.BlockSpec((SGU_GROUPS, SGU_CHUNK, 1), lambda i: (0, 0, 0))],
        out_specs=pl.BlockSpec((SGU_ROWS, BRANCH_WIDTH), lambda i: (i, 0)),
        out_shape=jax.ShapeDtypeStruct((t, BRANCH_WIDTH), BF16),
        compiler_params=_params(("parallel",)),
        name="spatial_gating",
    )(z_a, z_a, norm_w.reshape(1, BRANCH_WIDTH), w_s, b_s.reshape(SGU_GROUPS, SGU_CHUNK, 1))


def _hgrn_kernel(q_ref, f_ref, v_ref, g_ref, lb_ref, nw_ref, o_ref, state_ref):
    c = HGRN_SUBCHUNK
    half = c // 2
    rows_total = HGRN_ROWS
    n = rows_total // c

    @pl.when(pl.program_id(2) == 0)
    def _():
        state_ref[...] = jnp.zeros_like(state_ref)

    local = lax.broadcasted_iota(jnp.int32, (rows_total, 1), 0) & (c - 1)
    t_half = lax.broadcasted_iota(jnp.int32, (1, half, 1), 1)
    for hh in range(HGRN_HEADS_PER_STEP):
        cols = slice(hh * LANES, (hh + 1) * LANES)
        lb = lb_ref[hh]
        q = q_ref[0, :, cols]
        fl = f_ref[0, :, cols]
        v = v_ref[0, :, cols]
        e = jnp.exp(-jnp.abs(fl))
        r = 1.0 / (1.0 + e)
        sig_pos = jnp.where(fl >= 0, r, e * r)
        sig_neg = jnp.where(fl >= 0, e * r, r)
        kk = (1.0 - lb) * sig_neg
        b = jnp.log2(lb + (1.0 - lb) * sig_pos)
        shift = 1
        while shift < c:
            b = b + jnp.where(local >= shift, pltpu.roll(b, shift, 0), 0.0)
            shift *= 2
        b3 = b.reshape(n, c, LANES)
        kk3 = kk.reshape(n, c, LANES)
        q3 = q.reshape(n, c, LANES)
        v3 = v.reshape(n, c, LANES)
        tot3 = b3[:, c - 1:c, :]
        q_dec = (q * jnp.exp2(b)).astype(BF16)
        k_dec = (kk3 * jnp.exp2(tot3 - b3)).reshape(rows_total, LANES).astype(BF16)
        decay = jnp.exp2(tot3)
        v_bf = v.astype(BF16)

        updates = [lax.dot_general(v_bf[i * c:(i + 1) * c], k_dec[i * c:(i + 1) * c], (((0,), (0,)), ((), ())),
                                   preferred_element_type=F32) for i in range(n)]
        state_t = state_ref[hh]
        states = []
        for i in range(n):
            states.append(state_t.astype(BF16))
            state_t = state_t * decay[i] + updates[i]
        state_ref[hh] = state_t
        outs = [lax.dot_general(q_dec[i * c:(i + 1) * c], states[i], (((1,), (1,)), ((), ())),
                                preferred_element_type=F32) for i in range(n)]

        b_lo, b_hi = b3[:, :half, :], b3[:, half:, :]
        q_lo, q_hi = q3[:, :half, :], q3[:, half:, :]
        o_lo = jnp.zeros((n, half, LANES), F32)
        o_hi = jnp.zeros((n, half, LANES), F32)
        for s in range(c):
            ks = kk3[:, s:s + 1, :]
            bs = b3[:, s:s + 1, :]
            vs = v3[:, s:s + 1, :]
            if s < half:
                d = jnp.where(t_half >= s, b_lo - bs, -jnp.inf)
                a = jnp.sum(q_lo * ks * jnp.exp2(d), axis=-1, keepdims=True)
                o_lo = o_lo + a * vs
                d = b_hi - bs
            else:
                d = jnp.where(t_half >= s - half, b_hi - bs, -jnp.inf)
            a = jnp.sum(q_hi * ks * jnp.exp2(d), axis=-1, keepdims=True)
            o_hi = o_hi + a * vs
        o = jnp.concatenate([o_lo, o_hi], axis=1).reshape(rows_total, LANES) + jnp.concatenate(outs, axis=0)

        g = g_ref[0, :, cols]
        o = o * lax.rsqrt(jnp.mean(o * o, axis=-1, keepdims=True) + EPS) * nw_ref[hh]
        o_ref[0, :, cols] = (o * (g * _sigmoid(g))).astype(o_ref.dtype)


def hgrn2(z_b, col0, lower_bound, norm_w):
    b, s, _ = z_b.shape
    hp = HGRN_HEADS_PER_STEP
    width = hp * LANES
    c0 = col0 // width
    groups = HGRN_HEADS // hp

    def spec(part):
        return pl.BlockSpec((1, HGRN_ROWS, width), lambda bi, h, i: (bi, i, c0 + part * groups + h))

    vec = pl.BlockSpec((hp, 1, LANES), lambda bi, h, i: (h, 0, 0))
    return pl.pallas_call(
        _hgrn_kernel,
        grid=(b, groups, s // HGRN_ROWS),
        in_specs=[spec(0), spec(1), spec(2), spec(3), vec, vec],
        out_specs=pl.BlockSpec((1, HGRN_ROWS, width), lambda bi, h, i: (bi, i, h)),
        out_shape=jax.ShapeDtypeStruct((b, s, BRANCH_WIDTH), BF16),
        scratch_shapes=[pltpu.VMEM((hp, HGRN_VAL_DIM, HGRN_KEY_DIM), F32)],
        compiler_params=_params(("parallel", "parallel", "arbitrary")),
        name="hgrn2",
    )(z_b, z_b, z_b, z_b, lower_bound.reshape(HGRN_HEADS, 1, LANES), norm_w.reshape(HGRN_HEADS, 1, LANES))


def _router_kernel(x_ref, w_ref, wr_ref, br_ref, n_ref, meta_ref, p_ref, cnt_ref, carry_ref):
    @pl.when(pl.program_id(0) == 0)
    def _():
        carry_ref[...] = jnp.zeros_like(carry_ref)

    n = _rms(x_ref[...], w_ref[...])
    n_ref[...] = n
    logits = jnp.dot(n, wr_ref[...], precision=HIGHEST, preferred_element_type=F32) + br_ref[...]
    rows = logits.shape[0]
    col = lax.broadcasted_iota(jnp.int32, logits.shape, 1)
    big = jnp.int32(LANES)
    neg = -jnp.inf

    def first_argmax(vals):
        top = jnp.max(vals, axis=-1, keepdims=True)
        return top, jnp.min(jnp.where(vals == top, col, big), axis=-1, keepdims=True)

    group_logits = jnp.where(col < N_GROUPS, logits, neg)
    g_top, g_idx = first_argmax(group_logits)
    p_group = 1.0 / jnp.sum(jnp.exp(group_logits - g_top), axis=-1, keepdims=True)
    lo = N_GROUPS + g_idx * EXPERTS_PER_GROUP
    in_group = jnp.where((col >= lo) & (col < lo + EXPERTS_PER_GROUP), logits, neg)
    top1, idx1 = first_argmax(in_group)
    top2, idx2 = first_argmax(jnp.where(col == idx1, neg, in_group))
    r = jnp.exp(top2 - top1)
    p1 = p_group / (1.0 + r)
    p2 = p_group * r / (1.0 + r)
    p_ref[...] = jnp.where(col == 0, p1, jnp.where(col == 1, p2, 0.0))

    chosen = ((col == idx1) | (col == idx2)).astype(BF16)
    earlier = (lax.broadcasted_iota(jnp.int32, (rows, rows), 0)
               > lax.broadcasted_iota(jnp.int32, (rows, rows), 1)).astype(BF16)
    before = jnp.dot(earlier, chosen, preferred_element_type=F32) + carry_ref[...]
    rank1 = jnp.sum(jnp.where(col == idx1, before, 0.0), axis=-1, keepdims=True).astype(jnp.int32)
    rank2 = jnp.sum(jnp.where(col == idx2, before, 0.0), axis=-1, keepdims=True).astype(jnp.int32)
    carry_ref[...] += jnp.sum(chosen.astype(F32), axis=0, keepdims=True)
    cnt_ref[...] = carry_ref[...]
    meta_ref[...] = jnp.where(col == 0, idx1 - N_GROUPS, jnp.where(col == 1, idx2 - N_GROUPS,
                              jnp.where(col == 2, rank1, jnp.where(col == 3, rank2, 0))))


def moe_router(h, norm_w, w_router, b_router):
    t, d = h.shape
    row = pl.BlockSpec((NORM_ROWS, d), lambda i: (i, 0))
    small = pl.BlockSpec((NORM_ROWS, LANES), lambda i: (i, 0))
    one = pl.BlockSpec((1, LANES), lambda i: (0, 0))
    return pl.pallas_call(
        _router_kernel,
        grid=(t // NORM_ROWS,),
        in_specs=[row, pl.BlockSpec((1, d), lambda i: (0, 0)), pl.BlockSpec((d, LANES), lambda i: (0, 0)), one],
        out_specs=[row, small, small, one],
        out_shape=[jax.ShapeDtypeStruct((t, d), F32), jax.ShapeDtypeStruct((t, LANES), jnp.int32),
                   jax.ShapeDtypeStruct((t, LANES), F32), jax.ShapeDtypeStruct((1, LANES), F32)],
        scratch_shapes=[pltpu.VMEM((1, LANES), F32)],
        compiler_params=_params(("arbitrary",)),
        name="moe_router",
    )(h, norm_w.reshape(1, d), w_router, b_router)


def _gather_rows(idx_ref, src_hbm, dst_ref, sem, unrolled):
    n_rows = dst_ref.shape[0]

    def start(r):
        pltpu.make_async_copy(src_hbm.at[pl.ds(idx_ref[r], 1)], dst_ref.at[pl.ds(r, 1)], sem).start()

    if unrolled:
        for r in range(n_rows):
            start(r)
    else:
        def body(r, carry):
            start(r)
            return carry
        lax.fori_loop(0, n_rows, body, 0)


def _wait_rows(src_hbm, dst_ref, sem):
    pltpu.make_async_copy(src_hbm.at[pl.ds(0, dst_ref.shape[0])], dst_ref, sem).wait()


def _expert_kernel(be_ref, nused_ref, tok_cur_ref, tok_next_ref, x_hbm, wg_ref, wu_ref, wd_ref, y_ref,
                   xbuf_ref, sem_ref):
    i = pl.program_id(0)
    last = pl.num_programs(0) - 1
    n_used = nused_ref[0]
    slot = i % 2

    @pl.when((i == 0) & (n_used > 0))
    def _():
        _gather_rows(tok_cur_ref, x_hbm, xbuf_ref.at[0], sem_ref.at[0], unrolled=False)

    @pl.when(i < n_used)
    def _():
        _gather_rows(tok_next_ref, x_hbm, xbuf_ref.at[1 - slot], sem_ref.at[1 - slot], unrolled=True)
        _wait_rows(x_hbm, xbuf_ref.at[slot], sem_ref.at[slot])
        x = xbuf_ref[slot].astype(BF16)
        gate = jnp.dot(x, wg_ref[0], preferred_element_type=F32)
        up = jnp.dot(x, wu_ref[0], preferred_element_type=F32)
        hidden = (gate * _sigmoid(gate) * up).astype(BF16)
        y_ref[...] = jnp.dot(hidden, wd_ref[0], preferred_element_type=F32)

    @pl.when((i == n_used) & (n_used > 0))
    def _():
        _wait_rows(x_hbm, xbuf_ref.at[slot], sem_ref.at[slot])

    @pl.when((i == last) & (n_used > last))
    def _():
        _wait_rows(x_hbm, xbuf_ref.at[1 - slot], sem_ref.at[1 - slot])

    @pl.when(i >= n_used)
    def _():
        y_ref[...] = jnp.zeros_like(y_ref)


def expert_mlp(x, buf_tok, block_e, n_used, w_gate, w_up, w_down):
    p = buf_tok.shape[0]
    nb = p // MOE_ROWS
    d = x.shape[1]
    smem = pltpu.SMEM
    grid_spec = pltpu.PrefetchScalarGridSpec(
        num_scalar_prefetch=2,
        grid=(nb,),
        in_specs=[pl.BlockSpec((MOE_ROWS,), lambda i, be, nu: (i,), memory_space=smem),
                  pl.BlockSpec((MOE_ROWS,), lambda i, be, nu: (jnp.minimum(i + 1, nb - 1),), memory_space=smem),
                  pl.BlockSpec(memory_space=pl.ANY),
                  pl.BlockSpec((1, d, EXPERT_HIDDEN), lambda i, be, nu: (be[i], 0, 0)),
                  pl.BlockSpec((1, d, EXPERT_HIDDEN), lambda i, be, nu: (be[i], 0, 0)),
                  pl.BlockSpec((1, EXPERT_HIDDEN, d), lambda i, be, nu: (be[i], 0, 0))],
        out_specs=pl.BlockSpec((MOE_ROWS, d), lambda i, be, nu: (i, 0)),
        scratch_shapes=[pltpu.VMEM((2, MOE_ROWS, d), F32), pltpu.SemaphoreType.DMA((2,))],
    )
    return pl.pallas_call(
        _expert_kernel,
        grid_spec=grid_spec,
        out_shape=jax.ShapeDtypeStruct((p, d), F32),
        compiler_params=_params(("arbitrary",)),
        name="expert_mlp",
    )(block_e, n_used, buf_tok, buf_tok, x, w_gate, w_up, w_down)


def _combine_kernel(pos0_ref, pos1_ref, pos0_next_ref, pos1_next_ref, y_hbm, h_ref, p_ref, w_ref, o_ref, n_ref,
                    ybuf_ref, sem_ref):
    i = pl.program_id(0)
    last = pl.num_programs(0) - 1
    slot = i % 2

    @pl.when(i == 0)
    def _():
        _gather_rows(pos0_ref, y_hbm, ybuf_ref.at[0, 0], sem_ref.at[0, 0], unrolled=False)
        _gather_rows(pos1_ref, y_hbm, ybuf_ref.at[0, 1], sem_ref.at[0, 1], unrolled=False)

    _gather_rows(pos0_next_ref, y_hbm, ybuf_ref.at[1 - slot, 0], sem_ref.at[1 - slot, 0], unrolled=True)
    _gather_rows(pos1_next_ref, y_hbm, ybuf_ref.at[1 - slot, 1], sem_ref.at[1 - slot, 1], unrolled=True)
    _wait_rows(y_hbm, ybuf_ref.at[slot, 0], sem_ref.at[slot, 0])
    _wait_rows(y_hbm, ybuf_ref.at[slot, 1], sem_ref.at[slot, 1])
    p = p_ref[...]
    h = h_ref[...] + (ybuf_ref[slot, 0] * p[:, 0:1] + ybuf_ref[slot, 1] * p[:, 1:2])
    o_ref[...] = h
    n_ref[...] = _rms(h, w_ref[...]).astype(n_ref.dtype)

    @pl.when(i == last)
    def _():
        _wait_rows(y_hbm, ybuf_ref.at[1 - slot, 0], sem_ref.at[1 - slot, 0])
        _wait_rows(y_hbm, ybuf_ref.at[1 - slot, 1], sem_ref.at[1 - slot, 1])


def moe_combine(y, pos0, pos1, probs, h, norm_w):
    t, d = h.shape
    nb = t // COMBINE_ROWS
    smem = pltpu.SMEM
    row = pl.BlockSpec((COMBINE_ROWS, d), lambda i: (i, 0))
    cur = pl.BlockSpec((COMBINE_ROWS,), lambda i: (i,), memory_space=smem)
    nxt = pl.BlockSpec((COMBINE_ROWS,), lambda i: (jnp.minimum(i + 1, nb - 1),), memory_space=smem)
    return pl.pallas_call(
        _combine_kernel,
        grid=(nb,),
        in_specs=[cur, cur, nxt, nxt,
                  pl.BlockSpec(memory_space=pl.ANY),
                  row,
                  pl.BlockSpec((COMBINE_ROWS, LANES), lambda i: (i, 0)),
                  pl.BlockSpec((1, d), lambda i: (0, 0))],
        out_specs=[row, row],
        out_shape=[jax.ShapeDtypeStruct((t, d), F32), jax.ShapeDtypeStruct((t, d), BF16)],
        scratch_shapes=[pltpu.VMEM((2, TOP_K, COMBINE_ROWS, d), F32), pltpu.SemaphoreType.DMA((2, TOP_K))],
        compiler_params=_params(("arbitrary",)),
        name="moe_combine",
    )(pos0, pos1, pos0, pos1, y, h, probs, norm_w.reshape(1, d))


def _dispatch_plan(meta, counts):
    t = meta.shape[0]
    a = t * TOP_K
    expert_ids = meta[:, :TOP_K]
    ranks = meta[:, TOP_K:2 * TOP_K]
    counts = counts[0, N_GROUPS:N_GROUPS + N_EXPERTS].astype(jnp.int32)
    padded = (counts + MOE_ROWS - 1) // MOE_ROWS * MOE_ROWS
    pad_end = jnp.cumsum(padded)
    pad_start = pad_end - padded
    dest = pad_start[expert_ids] + ranks
    p = a + N_EXPERTS * MOE_ROWS
    nb = p // MOE_ROWS
    token_ids = jnp.repeat(jnp.arange(t, dtype=jnp.int32), TOP_K)
    buf_tok = jnp.zeros((p,), jnp.int32).at[dest.reshape(a)].set(token_ids)
    block_start = jnp.arange(nb, dtype=jnp.int32) * MOE_ROWS
    block_e = jnp.minimum(jnp.sum((pad_end[None, :] <= block_start[:, None]).astype(jnp.int32), axis=1),
                          N_EXPERTS - 1)
    n_used = (pad_end[-1:] // MOE_ROWS).astype(jnp.int32)
    return buf_tok, block_e, n_used, dest[:, 0], dest[:, 1]


def _pad_cols(w, n):
    return jnp.pad(w, ((0, 0), (0, n - w.shape[1])))


def _layer_weights(i, w_in, router_group_w, router_group_b, router_expert_w, router_expert_b):
    wi = w_in[i]
    w_a = jnp.concatenate([wi[:, :FOX_QKV_END], wi[:, POOL_END:SGU_END], wi[:, HGRN_END:]], axis=1).astype(BF16)
    w_b = jnp.concatenate([wi[:, FOX_F_END:POOL_END], wi[:, SGU_END:HGRN_END]], axis=1).astype(BF16)
    w_f = _pad_cols(wi[:, FOX_QKV_END:FOX_F_END], LANES).astype(BF16)
    w_r = _pad_cols(jnp.concatenate([router_group_w[i], router_expert_w[i]], axis=1), LANES)
    b_r = _pad_cols(jnp.concatenate([router_group_b[i], router_expert_b[i]])[None, :], LANES)
    return w_a, w_b, w_f, w_r, b_r


def kernel(x, p, mix_norm_w, w_in, fox_f_bias, pool_w, pool_scale, sgu_norm_w, sgu_w, sgu_b,
           hgrn_lb_logits, hgrn_norm_w, branch_proj, w_out, ffn_norm_w, router_group_w,
           router_group_b, router_expert_w, router_expert_b, expert_w_gate, expert_w_up,
           expert_w_down, ple_norm_w, ple_gate_w, ple_proj_w, final_norm_w):
    bsz, seq, d = x.shape
    t = bsz * seq
    lb_p = jax.nn.softmax(hgrn_lb_logits.astype(F32), axis=0)
    lower_bounds = jnp.cumsum(lb_p, axis=0) - lb_p[0:1]
    sgu_u_col = FOX_QKV_END
    gate_col = FOX_QKV_END + 2 * BRANCH_WIDTH
    h = x.reshape(t, d)
    n_a = FOX_QKV_END + 2 * BRANCH_WIDTH + N_BRANCHES * D_MODEL
    q_scale = jnp.where(jnp.arange(n_a) < BRANCH_WIDTH, FOX_HEAD_DIM ** -0.5 * LOG2E, 1.0).astype(F32)[None, :]
    for i in range(DEPTH):
        w_a, w_b, w_f, w_r, b_r = _layer_weights(i, w_in, router_group_w, router_group_b,
                                                 router_expert_w, router_expert_b)
        n, z_f = rms_norm_proj(h, mix_norm_w[i], w_f, BF16, exact_proj=False)
        z_a = matmul(n, w_a, q_scale, BF16)
        z_b = matmul(n, w_b, jnp.ones((1, w_b.shape[1]), F32), F32)
        z_a3 = z_a.reshape(bsz, seq, -1)
        z_b3 = z_b.reshape(bsz, seq, -1)

        f_bias = _pad_cols(fox_f_bias[i][None, :].astype(F32), LANES)
        eq, ek = forget_prep(z_f.reshape(bsz, seq, LANES), f_bias)
        v_t = jnp.transpose(z_a3[:, :, 2 * BRANCH_WIDTH:FOX_QKV_END].reshape(bsz, seq, FOX_HEADS, FOX_HEAD_DIM),
                            (0, 2, 3, 1))
        br_fox = fox_attention(z_a3, v_t, eq, ek).reshape(t, BRANCH_WIDTH)
        br_pool = multiscale_pool(z_b3, pool_w[i].astype(BF16), pool_scale[i]).reshape(t, BRANCH_WIDTH)
        br_sgu = spatial_gating(z_a, sgu_u_col, sgu_norm_w[i], sgu_w[i], sgu_b[i])
        br_hgrn = hgrn2(z_b3, BRANCH_WIDTH, lower_bounds[i], hgrn_norm_w[i]).reshape(t, BRANCH_WIDTH)

        merged = merge_branches((br_fox, br_pool, br_sgu, br_hgrn), branch_proj[i].astype(BF16), z_a, gate_col)
        h = matmul_residual(merged, w_out[i].astype(BF16), h)

        xn, meta, probs, counts = moe_router(h, ffn_norm_w[i], w_r, b_r)
        buf_tok, block_e, n_used, pos0, pos1 = _dispatch_plan(meta, counts)
        y = expert_mlp(xn, buf_tok, block_e, n_used, expert_w_gate[i].astype(BF16),
                       expert_w_up[i].astype(BF16), expert_w_down[i].astype(BF16))
        h, n_ple = moe_combine(y, pos0, pos1, probs, h, ple_norm_w[i])

        h = ple_update(n_ple, ple_gate_w[i].astype(BF16), p[i].reshape(t, PLE_DIM).astype(BF16),
                       ple_proj_w[i].astype(BF16), h)
    return rms_norm(h, final_norm_w, F32).reshape(bsz, seq, d)
```

```python
import functools

import jax
import jax.numpy as jnp
from jax import lax
from jax.experimental import pallas as pl
from jax.experimental.pallas import tpu as pltpu

F32 = jnp.float32
BF16 = jnp.bfloat16
HIGHEST = lax.Precision.HIGHEST

D_MODEL = 2048
DEPTH = 2
BRANCH_WIDTH = D_MODEL // 2
N_BRANCHES = 4
PLE_DIM = 256
EPS = 1e-6
FOX_HEAD_DIM = 128
FOX_HEADS = BRANCH_WIDTH // FOX_HEAD_DIM
POOL_WINDOWS = (2, 4, 8, 16)
POOL_GROUP = BRANCH_WIDTH // len(POOL_WINDOWS)
POOL_HALO = 16
SGU_GROUPS = 4
SGU_CHUNK = 128
SGU_GROUP_WIDTH = BRANCH_WIDTH // SGU_GROUPS
HGRN_KEY_DIM = 128
HGRN_VAL_DIM = 128
HGRN_HEADS = BRANCH_WIDTH // HGRN_VAL_DIM
HGRN_SUBCHUNK = 16
HGRN_HEADS_PER_STEP = 2
N_GROUPS = 4
EXPERTS_PER_GROUP = 8
N_EXPERTS = N_GROUPS * EXPERTS_PER_GROUP
TOP_K = 2
EXPERT_HIDDEN = D_MODEL // 2
FOX_QKV_END = 3 * BRANCH_WIDTH
FOX_F_END = FOX_QKV_END + FOX_HEADS
POOL_END = FOX_F_END + BRANCH_WIDTH
SGU_END = POOL_END + 2 * BRANCH_WIDTH
HGRN_END = SGU_END + 4 * BRANCH_WIDTH
IN_COLS = HGRN_END + N_BRANCHES * D_MODEL

LANES = 128
LOG2E = 1.4426950408889634
FOX_HEADS_PER_STEP = 8
FOX_QUERY_CHUNK = 256
VMEM_LIMIT = 56 * 1024 * 1024

NORM_ROWS = 512
MM_ROWS = 2048
MM_COLS = 512
PACK_ROWS = 1024
IN_COL_SHIFT = FOX_HEADS % LANES
MERGE_ROWS = 1024
ATT_BLOCK = 512
CUMSUM_BLOCK = 512
POOL_ROWS = 512
SGU_ROWS = 512
HGRN_ROWS = 256
MOE_ROWS = 256
COMBINE_ROWS = 256
PLE_ROWS = 1024


def _params(semantics, **kw):
    return pltpu.CompilerParams(dimension_semantics=semantics, vmem_limit_bytes=VMEM_LIMIT, **kw)


def _sigmoid(x):
    return 1.0 / (1.0 + jnp.exp(-x))


def _log_sigmoid(x):
    return jnp.minimum(x, 0.0) - jnp.log(1.0 + jnp.exp(-jnp.abs(x)))


def _rms(x, w):
    return x * lax.rsqrt(jnp.mean(x * x, axis=-1, keepdims=True) + EPS) * w


def _norm_kernel(x_ref, w_ref, n_ref):
    n_ref[...] = _rms(x_ref[...], w_ref[...]).astype(n_ref.dtype)


def rms_norm(x, w, out_dtype):
    t, d = x.shape
    return pl.pallas_call(
        _norm_kernel,
        grid=(t // NORM_ROWS,),
        in_specs=[pl.BlockSpec((NORM_ROWS, d), lambda i: (i, 0)),
                  pl.BlockSpec((1, d), lambda i: (0, 0))],
        out_specs=pl.BlockSpec((NORM_ROWS, d), lambda i: (i, 0)),
        out_shape=jax.ShapeDtypeStruct((t, d), out_dtype),
        compiler_params=_params(("parallel",)),
        name="rms_norm",
    )(x, w.reshape(1, d))


def _norm_proj_kernel(x_ref, w_ref, wp_ref, n_ref, p_ref, *, exact_proj):
    n = _rms(x_ref[...], w_ref[...])
    n_ref[...] = n.astype(n_ref.dtype)
    if exact_proj:
        p_ref[...] = jnp.dot(n, wp_ref[...], precision=HIGHEST, preferred_element_type=F32)
    else:
        p_ref[...] = jnp.dot(n.astype(BF16), wp_ref[...], preferred_element_type=F32)


def rms_norm_proj(x, w, wp, out_dtype, exact_proj):
    t, d = x.shape
    return pl.pallas_call(
        functools.partial(_norm_proj_kernel, exact_proj=exact_proj),
        grid=(t // NORM_ROWS,),
        in_specs=[pl.BlockSpec((NORM_ROWS, d), lambda i: (i, 0)),
                  pl.BlockSpec((1, d), lambda i: (0, 0)),
                  pl.BlockSpec((d, LANES), lambda i: (0, 0))],
        out_specs=[pl.BlockSpec((NORM_ROWS, d), lambda i: (i, 0)),
                   pl.BlockSpec((NORM_ROWS, LANES), lambda i: (i, 0))],
        out_shape=[jax.ShapeDtypeStruct((t, d), out_dtype),
                   jax.ShapeDtypeStruct((t, LANES), F32)],
        compiler_params=_params(("parallel",)),
        name="rms_norm_proj",
    )(x, w.reshape(1, d), wp)


def _mm_kernel(a_ref, w_ref, s_ref, o_ref):
    acc = jnp.dot(a_ref[...], w_ref[...], preferred_element_type=F32)
    o_ref[...] = (acc * s_ref[...]).astype(o_ref.dtype)


def matmul(a, w, col0, n, col_scale, out_dtype):
    m, k = a.shape
    j0 = col0 // MM_COLS
    return pl.pallas_call(
        _mm_kernel,
        grid=(m // MM_ROWS, n // MM_COLS),
        in_specs=[pl.BlockSpec((MM_ROWS, k), lambda i, j: (i, 0)),
                  pl.BlockSpec((k, MM_COLS), lambda i, j: (0, j0 + j)),
                  pl.BlockSpec((1, MM_COLS), lambda i, j: (0, j))],
        out_specs=pl.BlockSpec((MM_ROWS, MM_COLS), lambda i, j: (i, j)),
        out_shape=jax.ShapeDtypeStruct((m, n), out_dtype),
        compiler_params=_params(("parallel", "arbitrary")),
        name="matmul",
    )(a, w, col_scale)


def _mm_res_kernel(a_ref, w_ref, r_ref, o_ref):
    o_ref[...] = r_ref[...] + jnp.dot(a_ref[...], w_ref[...], preferred_element_type=F32)


def matmul_residual(a, w, res):
    m, k = a.shape
    n = w.shape[1]
    return pl.pallas_call(
        _mm_res_kernel,
        grid=(m // MM_ROWS, n // MM_COLS),
        in_specs=[pl.BlockSpec((MM_ROWS, k), lambda i, j: (i, 0)),
                  pl.BlockSpec((k, MM_COLS), lambda i, j: (0, j)),
                  pl.BlockSpec((MM_ROWS, MM_COLS), lambda i, j: (i, j))],
        out_specs=pl.BlockSpec((MM_ROWS, MM_COLS), lambda i, j: (i, j)),
        out_shape=jax.ShapeDtypeStruct((m, n), F32),
        compiler_params=_params(("parallel", "arbitrary")),
        name="matmul_residual",
    )(a, w, res)


def _merge_kernel(b0_ref, b1_ref, b2_ref, b3_ref, p_ref, g0_ref, g1_ref, g2_ref, g3_ref, o_ref):
    acc = None
    for bi, (b_ref, g_ref) in enumerate(((b0_ref, g0_ref), (b1_ref, g1_ref),
                                         (b2_ref, g2_ref), (b3_ref, g3_ref))):
        y = jnp.dot(b_ref[...], p_ref[bi], preferred_element_type=F32)
        y = _sigmoid(g_ref[...].astype(F32)) * y
        acc = y if acc is None else acc + y
    o_ref[...] = acc.astype(o_ref.dtype)


def merge_branches(branches, proj, z_a, gate_col0):
    t = branches[0].shape[0]
    g0 = gate_col0 // MM_COLS
    per = D_MODEL // MM_COLS
    b_spec = pl.BlockSpec((MERGE_ROWS, BRANCH_WIDTH), lambda i, j: (i, 0))
    g_specs = [pl.BlockSpec((MERGE_ROWS, MM_COLS), functools.partial(lambda i, j, o: (i, o + j), o=g0 + bi * per))
               for bi in range(N_BRANCHES)]
    return pl.pallas_call(
        _merge_kernel,
        grid=(t // MERGE_ROWS, per),
        in_specs=[b_spec] * N_BRANCHES
        + [pl.BlockSpec((N_BRANCHES, BRANCH_WIDTH, MM_COLS), lambda i, j: (0, 0, j))] + g_specs,
        out_specs=pl.BlockSpec((MERGE_ROWS, MM_COLS), lambda i, j: (i, j)),
        out_shape=jax.ShapeDtypeStruct((t, D_MODEL), BF16),
        compiler_params=_params(("parallel", "arbitrary")),
        name="merge_branches",
    )(*branches, proj, z_a, z_a, z_a, z_a)


def _ple_kernel(n_ref, wg_ref, p_ref, wp_ref, h_ref, o_ref):
    gate = _sigmoid(jnp.dot(n_ref[...], wg_ref[...], preferred_element_type=F32))
    emb = jnp.dot(p_ref[...], wp_ref[...], preferred_element_type=F32)
    o_ref[...] = h_ref[...] + gate * emb


def ple_update(n, wg, p, wp, h):
    t = h.shape[0]
    return pl.pallas_call(
        _ple_kernel,
        grid=(t // PLE_ROWS, D_MODEL // MM_COLS),
        in_specs=[pl.BlockSpec((PLE_ROWS, D_MODEL), lambda i, j: (i, 0)),
                  pl.BlockSpec((D_MODEL, MM_COLS), lambda i, j: (0, j)),
                  pl.BlockSpec((PLE_ROWS, PLE_DIM), lambda i, j: (i, 0)),
                  pl.BlockSpec((PLE_DIM, MM_COLS), lambda i, j: (0, j)),
                  pl.BlockSpec((PLE_ROWS, MM_COLS), lambda i, j: (i, j))],
        out_specs=pl.BlockSpec((PLE_ROWS, MM_COLS), lambda i, j: (i, j)),
        out_shape=jax.ShapeDtypeStruct((t, D_MODEL), F32),
        compiler_params=_params(("parallel", "arbitrary")),
        name="ple_update",
    )(n, wg, p, wp, h)


def _bf16_split3(x):
    hi = x.astype(BF16).astype(F32)
    r = x - hi
    mid = r.astype(BF16).astype(F32)
    lo = (r - mid).astype(BF16).astype(F32)
    return hi, mid, lo


def _forget_prep_kernel(z_ref, b_ref, eq_ref, ek_ref, carry_ref):
    @pl.when(pl.program_id(1) == 0)
    def _():
        carry_ref[...] = jnp.zeros_like(carry_ref)

    lf = _log_sigmoid(z_ref[0] + b_ref[...]) * LOG2E
    n = lf.shape[0]
    tril = (lax.broadcasted_iota(jnp.int32, (n, n), 0) >= lax.broadcasted_iota(jnp.int32, (n, n), 1)).astype(F32)
    c = jnp.dot(tril, lf, precision=HIGHEST, preferred_element_type=F32) + carry_ref[...]
    carry_ref[...] = c[n - 1:n, :]
    parts = _bf16_split3(c)
    lane = lax.broadcasted_iota(jnp.int32, (n, LANES), 1)
    for h in range(FOX_HEADS):
        hi, mid, lo = (jnp.sum(jnp.where(lane == h, part, 0.0), axis=-1, keepdims=True) for part in parts)
        eq = jnp.where(lane == 0, hi, jnp.where(lane == 1, mid, jnp.where(lane == 2, lo,
                       jnp.where(lane < 6, 1.0, 0.0))))
        ek = jnp.where(lane < 3, 1.0, jnp.where(lane == 3, -hi, jnp.where(lane == 4, -mid,
                       jnp.where(lane == 5, -lo, 0.0))))
        eq_ref[0, h] = eq.astype(BF16)
        ek_ref[0, h] = ek.astype(BF16)


def forget_prep(z_f, bias):
    b, s, _ = z_f.shape
    out = pl.BlockSpec((1, FOX_HEADS, CUMSUM_BLOCK, LANES), lambda bi, i: (bi, 0, i, 0))
    shape = jax.ShapeDtypeStruct((b, FOX_HEADS, s, LANES), BF16)
    return pl.pallas_call(
        _forget_prep_kernel,
        grid=(b, s // CUMSUM_BLOCK),
        in_specs=[pl.BlockSpec((1, CUMSUM_BLOCK, LANES), lambda bi, i: (bi, i, 0)),
                  pl.BlockSpec((1, LANES), lambda bi, i: (0, 0))],
        out_specs=[out, out],
        out_shape=[shape, shape],
        scratch_shapes=[pltpu.VMEM((1, LANES), F32)],
        compiler_params=_params(("parallel", "arbitrary")),
        name="forget_prep",
    )(z_f, bias)


def _fox_kernel(qi_ref, ki_ref, q_ref, k_ref, v_ref, eq_ref, ek_ref, o_ref, m_ref, l_ref, acc_ref):
    step = pl.program_id(2)
    qi = qi_ref[step]
    ki = ki_ref[step]
    blk = ATT_BLOCK
    dh = FOX_HEAD_DIM
    qc = FOX_QUERY_CHUNK

    @pl.when(ki == 0)
    def _():
        m_ref[...] = jnp.full_like(m_ref, -jnp.inf)
        l_ref[...] = jnp.zeros_like(l_ref)
        acc_ref[...] = jnp.zeros_like(acc_ref)

    def update(masked):
        chains = [(hh, c) for hh in range(FOX_HEADS_PER_STEP) for c in range(blk // qc)]
        scores = []
        for hh, c in chains:
            cols = slice(hh * dh, (hh + 1) * dh)
            qs = slice(c * qc, (c + 1) * qc)
            k = jnp.concatenate([k_ref[0, :, cols], ek_ref[0, hh]], axis=1)
            q = jnp.concatenate([q_ref[0, qs, cols], eq_ref[0, hh, qs, :]], axis=1)
            s = lax.dot_general(k, q, (((1,), (1,)), ((), ())), preferred_element_type=F32)
            if masked:
                key = lax.broadcasted_iota(jnp.int32, (blk, qc), 0)
                qry = lax.broadcasted_iota(jnp.int32, (blk, qc), 1) + c * qc
                s = jnp.where(key <= qry, s, -jnp.inf)
            scores.append(s)
        probs = []
        for (hh, c), s in zip(chains, scores):
            qs = slice(c * qc, (c + 1) * qc)
            m_old = m_ref[hh, :, qs]
            m_new = jnp.maximum(m_old, jnp.max(s, axis=0, keepdims=True))
            alpha = jnp.exp2(m_old - m_new)
            p = jnp.exp2(s - m_new)
            l_ref[hh, :, qs] = alpha * l_ref[hh, :, qs] + jnp.sum(p, axis=0, keepdims=True)
            m_ref[hh, :, qs] = m_new
            probs.append((alpha, p.astype(BF16)))
        v_t = [jnp.transpose(v_ref[0, :, hh * dh:(hh + 1) * dh]) for hh in range(FOX_HEADS_PER_STEP)]
        for (hh, c), (alpha, p) in zip(chains, probs):
            qs = slice(c * qc, (c + 1) * qc)
            acc_ref[hh, :, qs] = alpha * acc_ref[hh, :, qs] + jnp.dot(v_t[hh], p,
                                                                      preferred_element_type=F32)

    @pl.when(ki < qi)
    def _():
        update(False)

    @pl.when(ki == qi)
    def _():
        update(True)
        for hh in range(FOX_HEADS_PER_STEP):
            o_ref[0, :, hh * dh:(hh + 1) * dh] = jnp.transpose(acc_ref[hh] / l_ref[hh]).astype(o_ref.dtype)


def fox_attention(z_a, eq, ek):
    b, s, _ = z_a.shape
    nb = s // ATT_BLOCK
    hp = FOX_HEADS_PER_STEP
    groups = FOX_HEADS // hp
    pairs = [(qi, ki) for qi in range(nb) for ki in range(qi + 1)]
    qi_tab = jnp.asarray([pr[0] for pr in pairs], jnp.int32)
    ki_tab = jnp.asarray([pr[1] for pr in pairs], jnp.int32)
    width = hp * FOX_HEAD_DIM
    grid_spec = pltpu.PrefetchScalarGridSpec(
        num_scalar_prefetch=2,
        grid=(b, groups, len(pairs)),
        in_specs=[pl.BlockSpec((1, ATT_BLOCK, width), lambda bi, g, st, qt, kt: (bi, qt[st], g)),
                  pl.BlockSpec((1, ATT_BLOCK, width), lambda bi, g, st, qt, kt: (bi, kt[st], groups + g)),
                  pl.BlockSpec((1, ATT_BLOCK, width), lambda bi, g, st, qt, kt: (bi, kt[st], 2 * groups + g)),
                  pl.BlockSpec((1, hp, ATT_BLOCK, LANES), lambda bi, g, st, qt, kt: (bi, g, qt[st], 0)),
                  pl.BlockSpec((1, hp, ATT_BLOCK, LANES), lambda bi, g, st, qt, kt: (bi, g, kt[st], 0))],
        out_specs=pl.BlockSpec((1, ATT_BLOCK, width), lambda bi, g, st, qt, kt: (bi, qt[st], g)),
        scratch_shapes=[pltpu.VMEM((hp, 1, ATT_BLOCK), F32), pltpu.VMEM((hp, 1, ATT_BLOCK), F32),
                        pltpu.VMEM((hp, FOX_HEAD_DIM, ATT_BLOCK), F32)],
    )
    return pl.pallas_call(
        _fox_kernel,
        grid_spec=grid_spec,
        out_shape=jax.ShapeDtypeStruct((b, s, BRANCH_WIDTH), BF16),
        compiler_params=_params(("parallel", "parallel", "arbitrary")),
        name="fox_attention",
    )(qi_tab, ki_tab, z_a, z_a, z_a, eq, ek)


def _pool_kernel(u_ref, halo_ref, w_ref, scale_ref, o_ref):
    i = pl.program_id(1)
    rows = POOL_ROWS
    t = i * rows + lax.broadcasted_iota(jnp.int32, (rows, 1), 0)
    keep_halo = (i > 0).astype(F32)
    for g, win in enumerate(POOL_WINDOWS):
        cols = slice(g * POOL_GROUP, (g + 1) * POOL_GROUP)
        u = u_ref[0, :, cols]
        ext = jnp.concatenate([halo_ref[0, :, cols] * keep_halo, u], axis=0)
        span = 1
        while span < win:
            n = ext.shape[0]
            ext = ext[:n - span] + ext[span:]
            span *= 2
        window_sum = ext[POOL_HALO + 1 - win:POOL_HALO + 1 - win + rows]
        count = jnp.minimum(t + 1, win).astype(F32)
        d = (window_sum / count - u).astype(BF16)
        y = jnp.dot(d, w_ref[g], preferred_element_type=F32) * scale_ref[:, cols]
        o_ref[0, :, cols] = y.astype(o_ref.dtype)


def multiscale_pool(z_b, pool_w, pool_scale):
    b, s, _ = z_b.shape
    per = POOL_ROWS // POOL_HALO
    return pl.pallas_call(
        _pool_kernel,
        grid=(b, s // POOL_ROWS),
        in_specs=[pl.BlockSpec((1, POOL_ROWS, BRANCH_WIDTH), lambda bi, i: (bi, i, 0)),
                  pl.BlockSpec((1, POOL_HALO, BRANCH_WIDTH), lambda bi, i: (bi, jnp.maximum(i * per - 1, 0), 0)),
                  pl.BlockSpec((len(POOL_WINDOWS), POOL_GROUP, POOL_GROUP), lambda bi, i: (0, 0, 0)),
                  pl.BlockSpec((1, BRANCH_WIDTH), lambda bi, i: (0, 0))],
        out_specs=pl.BlockSpec((1, POOL_ROWS, BRANCH_WIDTH), lambda bi, i: (bi, i, 0)),
        out_shape=jax.ShapeDtypeStruct((b, s, BRANCH_WIDTH), BF16),
        compiler_params=_params(("parallel", "parallel")),
        name="multiscale_pool",
    )(z_b, z_b, pool_w, pool_scale.reshape(1, BRANCH_WIDTH))


def _sgu_kernel(u_ref, v_ref, nw_ref, w_ref, b_ref, o_ref):
    c = SGU_CHUNK
    v = jax.nn.gelu(v_ref[...].astype(F32))
    vc = v - jnp.mean(v, axis=-1, keepdims=True)
    vn = (vc * lax.rsqrt(jnp.mean(vc * vc, axis=-1, keepdims=True) + EPS) * nw_ref[...]).astype(BF16)
    causal = lax.broadcasted_iota(jnp.int32, (c, c), 0) >= lax.broadcasted_iota(jnp.int32, (c, c), 1)
    for g in range(SGU_GROUPS):
        cols = slice(g * SGU_GROUP_WIDTH, (g + 1) * SGU_GROUP_WIDTH)
        w = jnp.where(causal, w_ref[g], 0.0).astype(BF16)
        bias = b_ref[g]
        for n in range(SGU_ROWS // c):
            rows = slice(n * c, (n + 1) * c)
            sv = jnp.dot(w, vn[rows, cols], preferred_element_type=F32) + bias
            u = jax.nn.gelu(u_ref[rows, cols].astype(F32))
            o_ref[rows, cols] = (u * sv).astype(o_ref.dtype)


def spatial_gating(z_a, u_col0, norm_w, w_s, b_s):
    t = z_a.shape[0]
    ub = u_col0 // BRANCH_WIDTH
    return pl.pallas_call(
        _sgu_kernel,
        grid=(t // SGU_ROWS,),
        in_specs=[pl.BlockSpec((SGU_ROWS, BRANCH_WIDTH), lambda i: (i, ub)),
                  pl.BlockSpec((SGU_ROWS, BRANCH_WIDTH), lambda i: (i, ub + 1)),
                  pl.BlockSpec((1, BRANCH_WIDTH), lambda i: (0, 0)),
                  pl.BlockSpec((SGU_GROUPS, SGU_CHUNK, SGU_CHUNK), lambda i: (0, 0, 0)),
                  pl.BlockSpec((SGU_GROUPS, SGU_CHUNK, 1), lambda i: (0, 0, 0))],
        out_specs=pl.BlockSpec((SGU_ROWS, BRANCH_WIDTH), lambda i: (i, 0)),
        out_shape=jax.ShapeDtypeStruct((t, BRANCH_WIDTH), BF16),
        compiler_params=_params(("parallel",)),
        name="spatial_gating",
    )(z_a, z_a, norm_w.reshape(1, BRANCH_WIDTH), w_s, b_s.reshape(SGU_GROUPS, SGU_CHUNK, 1))


def _hgrn_kernel(q_ref, f_ref, v_ref, g_ref, lb_ref, nw_ref, o_ref, state_ref):
    c = HGRN_SUBCHUNK
    half = c // 2
    rows_total = HGRN_ROWS
    n = rows_total // c

    @pl.when(pl.program_id(2) == 0)
    def _():
        state_ref[...] = jnp.zeros_like(state_ref)

    local = lax.broadcasted_iota(jnp.int32, (rows_total, 1), 0) & (c - 1)
    t_half = lax.broadcasted_iota(jnp.int32, (1, half, 1), 1)
    for hh in range(HGRN_HEADS_PER_STEP):
        cols = slice(hh * LANES, (hh + 1) * LANES)
        lb = lb_ref[hh]
        q = q_ref[0, :, cols]
        fl = f_ref[0, :, cols]
        v = v_ref[0, :, cols]
        e = jnp.exp(-jnp.abs(fl))
        r = 1.0 / (1.0 + e)
        sig_pos = jnp.where(fl >= 0, r, e * r)
        sig_neg = jnp.where(fl >= 0, e * r, r)
        kk = (1.0 - lb) * sig_neg
        b = jnp.log2(lb + (1.0 - lb) * sig_pos)
        shift = 1
        while shift < c:
            b = b + jnp.where(local >= shift, pltpu.roll(b, shift, 0), 0.0)
            shift *= 2
        b3 = b.reshape(n, c, LANES)
        kk3 = kk.reshape(n, c, LANES)
        q3 = q.reshape(n, c, LANES)
        v3 = v.reshape(n, c, LANES)
        tot3 = b3[:, c - 1:c, :]
        q_dec = (q * jnp.exp2(b)).astype(BF16)
        k_dec = (kk3 * jnp.exp2(tot3 - b3)).reshape(rows_total, LANES).astype(BF16)
        decay = jnp.exp2(tot3)
        v_bf = v.astype(BF16)

        updates = [lax.dot_general(v_bf[i * c:(i + 1) * c], k_dec[i * c:(i + 1) * c], (((0,), (0,)), ((), ())),
                                   preferred_element_type=F32) for i in range(n)]
        state_t = state_ref[hh]
        states = []
        for i in range(n):
            states.append(state_t.astype(BF16))
            state_t = state_t * decay[i] + updates[i]
        state_ref[hh] = state_t
        outs = [lax.dot_general(q_dec[i * c:(i + 1) * c], states[i], (((1,), (1,)), ((), ())),
                                preferred_element_type=F32) for i in range(n)]

        b_lo, b_hi = b3[:, :half, :], b3[:, half:, :]
        q_lo, q_hi = q3[:, :half, :], q3[:, half:, :]
        o_lo = jnp.zeros((n, half, LANES), F32)
        o_hi = jnp.zeros((n, half, LANES), F32)
        for s in range(c):
            ks = kk3[:, s:s + 1, :]
            bs = b3[:, s:s + 1, :]
            vs = v3[:, s:s + 1, :]
            if s < half:
                d = jnp.where(t_half >= s, b_lo - bs, -jnp.inf)
                a = jnp.sum(q_lo * ks * jnp.exp2(d), axis=-1, keepdims=True)
                o_lo = o_lo + a * vs
                d = b_hi - bs
            else:
                d = jnp.where(t_half >= s - half, b_hi - bs, -jnp.inf)
            a = jnp.sum(q_hi * ks * jnp.exp2(d), axis=-1, keepdims=True)
            o_hi = o_hi + a * vs
        o = jnp.concatenate([o_lo, o_hi], axis=1).reshape(rows_total, LANES) + jnp.concatenate(outs, axis=0)

        g = g_ref[0, :, cols]
        o = o * lax.rsqrt(jnp.mean(o * o, axis=-1, keepdims=True) + EPS) * nw_ref[hh]
        o_ref[0, :, cols] = (o * (g * _sigmoid(g))).astype(o_ref.dtype)


def hgrn2(z_b, col0, lower_bound, norm_w):
    b, s, _ = z_b.shape
    hp = HGRN_HEADS_PER_STEP
    width = hp * LANES
    c0 = col0 // width
    groups = HGRN_HEADS // hp

    def spec(part):
        return pl.BlockSpec((1, HGRN_ROWS, width), lambda bi, h, i: (bi, i, c0 + part * groups + h))

    vec = pl.BlockSpec((hp, 1, LANES), lambda bi, h, i: (h, 0, 0))
    return pl.pallas_call(
        _hgrn_kernel,
        grid=(b, groups, s // HGRN_ROWS),
        in_specs=[spec(0), spec(1), spec(2), spec(3), vec, vec],
        out_specs=pl.BlockSpec((1, HGRN_ROWS, width), lambda bi, h, i: (bi, i, h)),
        out_shape=jax.ShapeDtypeStruct((b, s, BRANCH_WIDTH), BF16),
        scratch_shapes=[pltpu.VMEM((hp, HGRN_VAL_DIM, HGRN_KEY_DIM), F32)],
        compiler_params=_params(("parallel", "parallel", "arbitrary")),
        name="hgrn2",
    )(z_b, z_b, z_b, z_b, lower_bound.reshape(HGRN_HEADS, 1, LANES), norm_w.reshape(HGRN_HEADS, 1, LANES))


def _router_kernel(x_ref, w_ref, wr_ref, br_ref, n_ref, meta_ref, p_ref, cnt_ref, carry_ref):
    @pl.when(pl.program_id(0) == 0)
    def _():
        carry_ref[...] = jnp.zeros_like(carry_ref)

    n = _rms(x_ref[...], w_ref[...])
    n_ref[...] = n
    logits = jnp.dot(n, wr_ref[...], precision=HIGHEST, preferred_element_type=F32) + br_ref[...]
    rows = logits.shape[0]
    col = lax.broadcasted_iota(jnp.int32, logits.shape, 1)
    big = jnp.int32(LANES)
    neg = -jnp.inf

    def first_argmax(vals):
        top = jnp.max(vals, axis=-1, keepdims=True)
        return top, jnp.min(jnp.where(vals == top, col, big), axis=-1, keepdims=True)

    group_logits = jnp.where(col < N_GROUPS, logits, neg)
    g_top, g_idx = first_argmax(group_logits)
    p_group = 1.0 / jnp.sum(jnp.exp(group_logits - g_top), axis=-1, keepdims=True)
    lo = N_GROUPS + g_idx * EXPERTS_PER_GROUP
    in_group = jnp.where((col >= lo) & (col < lo + EXPERTS_PER_GROUP), logits, neg)
    top1, idx1 = first_argmax(in_group)
    top2, idx2 = first_argmax(jnp.where(col == idx1, neg, in_group))
    r = jnp.exp(top2 - top1)
    p1 = p_group / (1.0 + r)
    p2 = p_group * r / (1.0 + r)
    p_ref[...] = jnp.where(col == 0, p1, jnp.where(col == 1, p2, 0.0))

    chosen = ((col == idx1) | (col == idx2)).astype(BF16)
    earlier = (lax.broadcasted_iota(jnp.int32, (rows, rows), 0)
               > lax.broadcasted_iota(jnp.int32, (rows, rows), 1)).astype(BF16)
    before = jnp.dot(earlier, chosen, preferred_element_type=F32) + carry_ref[...]
    rank1 = jnp.sum(jnp.where(col == idx1, before, 0.0), axis=-1, keepdims=True).astype(jnp.int32)
    rank2 = jnp.sum(jnp.where(col == idx2, before, 0.0), axis=-1, keepdims=True).astype(jnp.int32)
    carry_ref[...] += jnp.sum(chosen.astype(F32), axis=0, keepdims=True)
    cnt_ref[...] = carry_ref[...]
    meta_ref[...] = jnp.where(col == 0, idx1 - N_GROUPS, jnp.where(col == 1, idx2 - N_GROUPS,
                              jnp.where(col == 2, rank1, jnp.where(col == 3, rank2, 0))))


def moe_router(h, norm_w, w_router, b_router):
    t, d = h.shape
    row = pl.BlockSpec((NORM_ROWS, d), lambda i: (i, 0))
    small = pl.BlockSpec((NORM_ROWS, LANES), lambda i: (i, 0))
    one = pl.BlockSpec((1, LANES), lambda i: (0, 0))
    return pl.pallas_call(
        _router_kernel,
        grid=(t // NORM_ROWS,),
        in_specs=[row, pl.BlockSpec((1, d), lambda i: (0, 0)), pl.BlockSpec((d, LANES), lambda i: (0, 0)), one],
        out_specs=[row, small, small, one],
        out_shape=[jax.ShapeDtypeStruct((t, d), F32), jax.ShapeDtypeStruct((t, LANES), jnp.int32),
                   jax.ShapeDtypeStruct((t, LANES), F32), jax.ShapeDtypeStruct((1, LANES), F32)],
        scratch_shapes=[pltpu.VMEM((1, LANES), F32)],
        compiler_params=_params(("arbitrary",)),
        name="moe_router",
    )(h, norm_w.reshape(1, d), w_router, b_router)


def _gather_rows(idx_ref, src_hbm, dst_ref, sem, unrolled):
    n_rows = dst_ref.shape[0]

    def start(r):
        pltpu.make_async_copy(src_hbm.at[pl.ds(idx_ref[r], 1)], dst_ref.at[pl.ds(r, 1)], sem).start()

    if unrolled:
        for r in range(n_rows):
            start(r)
    else:
        def body(r, carry):
            start(r)
            return carry
        lax.fori_loop(0, n_rows, body, 0)


def _wait_rows(src_hbm, dst_ref, sem):
    pltpu.make_async_copy(src_hbm.at[pl.ds(0, dst_ref.shape[0])], dst_ref, sem).wait()


def _expert_kernel(be_ref, nused_ref, tok_cur_ref, tok_next_ref, x_hbm, wg_ref, wu_ref, wd_ref, y_ref,
                   xbuf_ref, sem_ref):
    i = pl.program_id(0)
    last = pl.num_programs(0) - 1
    n_used = nused_ref[0]
    slot = i % 2

    @pl.when((i == 0) & (n_used > 0))
    def _():
        _gather_rows(tok_cur_ref, x_hbm, xbuf_ref.at[0], sem_ref.at[0], unrolled=False)

    @pl.when(i < n_used)
    def _():
        _gather_rows(tok_next_ref, x_hbm, xbuf_ref.at[1 - slot], sem_ref.at[1 - slot], unrolled=True)
        _wait_rows(x_hbm, xbuf_ref.at[slot], sem_ref.at[slot])
        x = xbuf_ref[slot].astype(BF16)
        gate = jnp.dot(x, wg_ref[0], preferred_element_type=F32)
        up = jnp.dot(x, wu_ref[0], preferred_element_type=F32)
        hidden = (gate * _sigmoid(gate) * up).astype(BF16)
        y_ref[...] = jnp.dot(hidden, wd_ref[0], preferred_element_type=F32)

    @pl.when((i == n_used) & (n_used > 0))
    def _():
        _wait_rows(x_hbm, xbuf_ref.at[slot], sem_ref.at[slot])

    @pl.when((i == last) & (n_used > last))
    def _():
        _wait_rows(x_hbm, xbuf_ref.at[1 - slot], sem_ref.at[1 - slot])

    @pl.when(i >= n_used)
    def _():
        y_ref[...] = jnp.zeros_like(y_ref)


def expert_mlp(x, buf_tok, block_e, n_used, w_gate, w_up, w_down):
    p = buf_tok.shape[0]
    nb = p // MOE_ROWS
    d = x.shape[1]
    smem = pltpu.SMEM
    grid_spec = pltpu.PrefetchScalarGridSpec(
        num_scalar_prefetch=2,
        grid=(nb,),
        in_specs=[pl.BlockSpec((MOE_ROWS,), lambda i, be, nu: (i,), memory_space=smem),
                  pl.BlockSpec((MOE_ROWS,), lambda i, be, nu: (jnp.minimum(i + 1, nb - 1),), memory_space=smem),
                  pl.BlockSpec(memory_space=pl.ANY),
                  pl.BlockSpec((1, d, EXPERT_HIDDEN), lambda i, be, nu: (be[i], 0, 0)),
                  pl.BlockSpec((1, d, EXPERT_HIDDEN), lambda i, be, nu: (be[i], 0, 0)),
                  pl.BlockSpec((1, EXPERT_HIDDEN, d), lambda i, be, nu: (be[i], 0, 0))],
        out_specs=pl.BlockSpec((MOE_ROWS, d), lambda i, be, nu: (i, 0)),
        scratch_shapes=[pltpu.VMEM((2, MOE_ROWS, d), F32), pltpu.SemaphoreType.DMA((2,))],
    )
    return pl.pallas_call(
        _expert_kernel,
        grid_spec=grid_spec,
        out_shape=jax.ShapeDtypeStruct((p, d), F32),
        compiler_params=_params(("arbitrary",)),
        name="expert_mlp",
    )(block_e, n_used, buf_tok, buf_tok, x, w_gate, w_up, w_down)


def _combine_kernel(pos0_ref, pos1_ref, pos0_next_ref, pos1_next_ref, y_hbm, h_ref, p_ref, w_ref, o_ref, n_ref,
                    ybuf_ref, sem_ref):
    i = pl.program_id(0)
    last = pl.num_programs(0) - 1
    slot = i % 2

    @pl.when(i == 0)
    def _():
        _gather_rows(pos0_ref, y_hbm, ybuf_ref.at[0, 0], sem_ref.at[0, 0], unrolled=False)
        _gather_rows(pos1_ref, y_hbm, ybuf_ref.at[0, 1], sem_ref.at[0, 1], unrolled=False)

    _gather_rows(pos0_next_ref, y_hbm, ybuf_ref.at[1 - slot, 0], sem_ref.at[1 - slot, 0], unrolled=True)
    _gather_rows(pos1_next_ref, y_hbm, ybuf_ref.at[1 - slot, 1], sem_ref.at[1 - slot, 1], unrolled=True)
    _wait_rows(y_hbm, ybuf_ref.at[slot, 0], sem_ref.at[slot, 0])
    _wait_rows(y_hbm, ybuf_ref.at[slot, 1], sem_ref.at[slot, 1])
    p = p_ref[...]
    h = h_ref[...] + (ybuf_ref[slot, 0] * p[:, 0:1] + ybuf_ref[slot, 1] * p[:, 1:2])
    o_ref[...] = h
    n_ref[...] = _rms(h, w_ref[...]).astype(n_ref.dtype)

    @pl.when(i == last)
    def _():
        _wait_rows(y_hbm, ybuf_ref.at[1 - slot, 0], sem_ref.at[1 - slot, 0])
        _wait_rows(y_hbm, ybuf_ref.at[1 - slot, 1], sem_ref.at[1 - slot, 1])


def moe_combine(y, pos0, pos1, probs, h, norm_w):
    t, d = h.shape
    nb = t // COMBINE_ROWS
    smem = pltpu.SMEM
    row = pl.BlockSpec((COMBINE_ROWS, d), lambda i: (i, 0))
    cur = pl.BlockSpec((COMBINE_ROWS,), lambda i: (i,), memory_space=smem)
    nxt = pl.BlockSpec((COMBINE_ROWS,), lambda i: (jnp.minimum(i + 1, nb - 1),), memory_space=smem)
    return pl.pallas_call(
        _combine_kernel,
        grid=(nb,),
        in_specs=[cur, cur, nxt, nxt,
                  pl.BlockSpec(memory_space=pl.ANY),
                  row,
                  pl.BlockSpec((COMBINE_ROWS, LANES), lambda i: (i, 0)),
                  pl.BlockSpec((1, d), lambda i: (0, 0))],
        out_specs=[row, row],
        out_shape=[jax.ShapeDtypeStruct((t, d), F32), jax.ShapeDtypeStruct((t, d), BF16)],
        scratch_shapes=[pltpu.VMEM((2, TOP_K, COMBINE_ROWS, d), F32), pltpu.SemaphoreType.DMA((2, TOP_K))],
        compiler_params=_params(("arbitrary",)),
        name="moe_combine",
    )(pos0, pos1, pos0, pos1, y, h, probs, norm_w.reshape(1, d))


def _dispatch_plan(meta, counts):
    t = meta.shape[0]
    a = t * TOP_K
    expert_ids = meta[:, :TOP_K]
    ranks = meta[:, TOP_K:2 * TOP_K]
    counts = counts[0, N_GROUPS:N_GROUPS + N_EXPERTS].astype(jnp.int32)
    padded = (counts + MOE_ROWS - 1) // MOE_ROWS * MOE_ROWS
    pad_end = jnp.cumsum(padded)
    pad_start = pad_end - padded
    dest = pad_start[expert_ids] + ranks
    p = a + N_EXPERTS * MOE_ROWS
    nb = p // MOE_ROWS
    token_ids = jnp.repeat(jnp.arange(t, dtype=jnp.int32), TOP_K)
    buf_tok = jnp.zeros((p,), jnp.int32).at[dest.reshape(a)].set(token_ids)
    block_start = jnp.arange(nb, dtype=jnp.int32) * MOE_ROWS
    block_e = jnp.minimum(jnp.sum((pad_end[None, :] <= block_start[:, None]).astype(jnp.int32), axis=1),
                          N_EXPERTS - 1)
    n_used = (pad_end[-1:] // MOE_ROWS).astype(jnp.int32)
    return buf_tok, block_e, n_used, dest[:, 0], dest[:, 1]


def _pad_cols(w, n):
    return jnp.pad(w, ((0, 0), (0, n - w.shape[1])))


def _layer_weights(i, w_in, router_group_w, router_group_b, router_expert_w, router_expert_b):
    w_f = _pad_cols(w_in[i, :, FOX_QKV_END:FOX_F_END], LANES).astype(BF16)
    w_r = _pad_cols(jnp.concatenate([router_group_w[i], router_expert_w[i]], axis=1), LANES)
    b_r = _pad_cols(jnp.concatenate([router_group_b[i], router_expert_b[i]])[None, :], LANES)
    return w_f, w_r, b_r


def _pack_kernel(src_ref, shifted_ref, a0_ref, a1_ref, a2_ref, a3_ref, a4_ref, o_ref):
    t = pl.program_id(1)
    aligned = jnp.concatenate([a0_ref[0], a1_ref[0], a2_ref[0], a3_ref[0]], axis=1)

    @pl.when(shifted_ref[t] == 0)
    def _():
        o_ref[...] = aligned.astype(o_ref.dtype)

    @pl.when(shifted_ref[t] != 0)
    def _():
        wide = jnp.concatenate([aligned, a4_ref[0]], axis=1)
        o_ref[...] = wide[:, IN_COL_SHIFT:IN_COL_SHIFT + MM_COLS].astype(o_ref.dtype)


def pack_in_weights(w_in, layer):
    d = w_in.shape[1]
    groups = ((0, FOX_QKV_END), (POOL_END, SGU_END), (HGRN_END, IN_COLS), (FOX_F_END, POOL_END), (SGU_END, HGRN_END))
    starts = [c for lo, hi in groups for c in range(lo, hi, MM_COLS)]
    assert all(c % LANES in (0, IN_COL_SHIFT) for c in starts)
    src_blk = jnp.asarray([c // LANES for c in starts], jnp.int32)
    shifted = jnp.asarray([int(c % LANES != 0) for c in starts], jnp.int32)
    n_tiles = len(starts)

    def piece(k):
        return pl.BlockSpec((1, PACK_ROWS, LANES), lambda r, t, sb, sh: (layer, r, sb[t] + k))

    grid_spec = pltpu.PrefetchScalarGridSpec(
        num_scalar_prefetch=2,
        grid=(d // PACK_ROWS, n_tiles),
        in_specs=[piece(k) for k in range(MM_COLS // LANES + 1)],
        out_specs=pl.BlockSpec((PACK_ROWS, MM_COLS), lambda r, t, sb, sh: (r, t)),
    )
    return pl.pallas_call(
        _pack_kernel,
        grid_spec=grid_spec,
        out_shape=jax.ShapeDtypeStruct((d, n_tiles * MM_COLS), BF16),
        compiler_params=_params(("parallel", "arbitrary")),
        name="pack_in_weights",
    )(src_blk, shifted, w_in, w_in, w_in, w_in, w_in)


def kernel(x, p, mix_norm_w, w_in, fox_f_bias, pool_w, pool_scale, sgu_norm_w, sgu_w, sgu_b,
           hgrn_lb_logits, hgrn_norm_w, branch_proj, w_out, ffn_norm_w, router_group_w,
           router_group_b, router_expert_w, router_expert_b, expert_w_gate, expert_w_up,
           expert_w_down, ple_norm_w, ple_gate_w, ple_proj_w, final_norm_w):
    bsz, seq, d = x.shape
    t = bsz * seq
    lb_p = jax.nn.softmax(hgrn_lb_logits.astype(F32), axis=0)
    lower_bounds = jnp.cumsum(lb_p, axis=0) - lb_p[0:1]
    sgu_u_col = FOX_QKV_END
    gate_col = FOX_QKV_END + 2 * BRANCH_WIDTH
    h = x.reshape(t, d)
    n_a = FOX_QKV_END + 2 * BRANCH_WIDTH + N_BRANCHES * D_MODEL
    n_b = BRANCH_WIDTH + 4 * BRANCH_WIDTH
    q_scale = jnp.where(jnp.arange(n_a) < BRANCH_WIDTH, FOX_HEAD_DIM ** -0.5 * LOG2E, 1.0).astype(F32)[None, :]
    for i in range(DEPTH):
        w_f, w_r, b_r = _layer_weights(i, w_in, router_group_w, router_group_b, router_expert_w, router_expert_b)
        w_all = pack_in_weights(w_in, i)
        n, z_f = rms_norm_proj(h, mix_norm_w[i], w_f, BF16, exact_proj=False)
        z_a = matmul(n, w_all, 0, n_a, q_scale, BF16)
        z_b = matmul(n, w_all, n_a, n_b, jnp.ones((1, n_b), F32), F32)
        z_a3 = z_a.reshape(bsz, seq, -1)
        z_b3 = z_b.reshape(bsz, seq, -1)

        f_bias = _pad_cols(fox_f_bias[i][None, :].astype(F32), LANES)
        eq, ek = forget_prep(z_f.reshape(bsz, seq, LANES), f_bias)
        br_fox = fox_attention(z_a3, eq, ek).reshape(t, BRANCH_WIDTH)
        br_pool = multiscale_pool(z_b3, pool_w[i].astype(BF16), pool_scale[i]).reshape(t, BRANCH_WIDTH)
        br_sgu = spatial_gating(z_a, sgu_u_col, sgu_norm_w[i], sgu_w[i], sgu_b[i])
        br_hgrn = hgrn2(z_b3, BRANCH_WIDTH, lower_bounds[i], hgrn_norm_w[i]).reshape(t, BRANCH_WIDTH)

        merged = merge_branches((br_fox, br_pool, br_sgu, br_hgrn), branch_proj[i].astype(BF16), z_a, gate_col)
        h = matmul_residual(merged, w_out[i].astype(BF16), h)

        xn, meta, probs, counts = moe_router(h, ffn_norm_w[i], w_r, b_r)
        buf_tok, block_e, n_used, pos0, pos1 = _dispatch_plan(meta, counts)
        y = expert_mlp(xn, buf_tok, block_e, n_used, expert_w_gate[i].astype(BF16),
                       expert_w_up[i].astype(BF16), expert_w_down[i].astype(BF16))
        h, n_ple = moe_combine(y, pos0, pos1, probs, h, ple_norm_w[i])

        h = ple_update(n_ple, ple_gate_w[i].astype(BF16), p[i].reshape(t, PLE_DIM).astype(BF16),
                       ple_proj_w[i].astype(BF16), h)
    return rms_norm(h, final_norm_w, F32).reshape(bsz, seq, d)
```

```python
import functools

import jax
import jax.numpy as jnp
from jax import lax
from jax.experimental import pallas as pl
from jax.experimental.pallas import tpu as pltpu

F32 = jnp.float32
BF16 = jnp.bfloat16
HIGHEST = lax.Precision.HIGHEST

D_MODEL = 2048
DEPTH = 2
BRANCH_WIDTH = D_MODEL // 2
N_BRANCHES = 4
PLE_DIM = 256
EPS = 1e-6
FOX_HEAD_DIM = 128
FOX_HEADS = BRANCH_WIDTH // FOX_HEAD_DIM
POOL_WINDOWS = (2, 4, 8, 16)
POOL_GROUP = BRANCH_WIDTH // len(POOL_WINDOWS)
POOL_HALO = 16
SGU_GROUPS = 4
SGU_CHUNK = 128
SGU_GROUP_WIDTH = BRANCH_WIDTH // SGU_GROUPS
HGRN_KEY_DIM = 128
HGRN_VAL_DIM = 128
HGRN_HEADS = BRANCH_WIDTH // HGRN_VAL_DIM
HGRN_SUBCHUNK = 16
HGRN_HEADS_PER_STEP = 2
N_GROUPS = 4
EXPERTS_PER_GROUP = 8
N_EXPERTS = N_GROUPS * EXPERTS_PER_GROUP
TOP_K = 2
EXPERT_HIDDEN = D_MODEL // 2
FOX_QKV_END = 3 * BRANCH_WIDTH
FOX_F_END = FOX_QKV_END + FOX_HEADS
POOL_END = FOX_F_END + BRANCH_WIDTH
SGU_END = POOL_END + 2 * BRANCH_WIDTH
HGRN_END = SGU_END + 4 * BRANCH_WIDTH
IN_COLS = HGRN_END + N_BRANCHES * D_MODEL

LANES = 128
SUBLANES = 8
LOG2E = 1.4426950408889634
FOX_HEADS_PER_STEP = 8
FOX_QUERY_CHUNK = 256
VMEM_LIMIT = 56 * 1024 * 1024

NORM_ROWS = 512
MM_ROWS = 2048
MM_COLS = 512
MERGE_ROWS = 1024
ATT_BLOCK = 512
CUMSUM_BLOCK = 512
POOL_ROWS = 512
SGU_ROWS = 512
HGRN_ROWS = 256
MOE_ROWS = 256
COMBINE_ROWS = 256
PLE_ROWS = 1024


def _params(semantics, **kw):
    return pltpu.CompilerParams(dimension_semantics=semantics, vmem_limit_bytes=VMEM_LIMIT, **kw)


def _sigmoid(x):
    return 1.0 / (1.0 + jnp.exp(-x))


def _log_sigmoid(x):
    return jnp.minimum(x, 0.0) - jnp.log(1.0 + jnp.exp(-jnp.abs(x)))


def _rms(x, w):
    return x * lax.rsqrt(jnp.mean(x * x, axis=-1, keepdims=True) + EPS) * w


def _norm_kernel(x_ref, w_ref, n_ref):
    n_ref[...] = _rms(x_ref[...], w_ref[...]).astype(n_ref.dtype)


def rms_norm(x, w, out_dtype):
    t, d = x.shape
    return pl.pallas_call(
        _norm_kernel,
        grid=(t // NORM_ROWS,),
        in_specs=[pl.BlockSpec((NORM_ROWS, d), lambda i: (i, 0)),
                  pl.BlockSpec((1, d), lambda i: (0, 0))],
        out_specs=pl.BlockSpec((NORM_ROWS, d), lambda i: (i, 0)),
        out_shape=jax.ShapeDtypeStruct((t, d), out_dtype),
        compiler_params=_params(("parallel",)),
        name="rms_norm",
    )(x, w.reshape(1, d))


def _norm_proj_kernel(x_ref, w_ref, wpt_ref, n_ref, p_ref):
    n = _rms(x_ref[...], w_ref[...]).astype(BF16)
    n_ref[...] = n.astype(n_ref.dtype)
    p_ref[...] = lax.dot_general(n, wpt_ref[0].astype(BF16), (((1,), (1,)), ((), ())), preferred_element_type=F32)


def rms_norm_proj(x, w, wt, layer, row0, out_dtype):
    t, d = x.shape
    jp = row0 // LANES
    return pl.pallas_call(
        _norm_proj_kernel,
        grid=(t // NORM_ROWS,),
        in_specs=[pl.BlockSpec((NORM_ROWS, d), lambda i: (i, 0)),
                  pl.BlockSpec((1, d), lambda i: (0, 0)),
                  pl.BlockSpec((1, LANES, d), lambda i: (layer, jp, 0))],
        out_specs=[pl.BlockSpec((NORM_ROWS, d), lambda i: (i, 0)),
                   pl.BlockSpec((NORM_ROWS, LANES), lambda i: (i, 0))],
        out_shape=[jax.ShapeDtypeStruct((t, d), out_dtype),
                   jax.ShapeDtypeStruct((t, LANES), F32)],
        compiler_params=_params(("parallel",)),
        name="rms_norm_proj",
    )(x, w.reshape(1, d), wt)


def _in_proj_kernel(start_tiles_ref, a_ref, wt_ref, s_ref, o_ref):
    w = wt_ref[0].astype(BF16)
    acc = lax.dot_general(a_ref[...], w, (((1,), (1,)), ((), ())), preferred_element_type=F32)
    o_ref[...] = (acc * s_ref[...]).astype(o_ref.dtype)


def in_projection(a, wt, layer, starts, col_scale, out_dtype):
    m, k = a.shape
    assert all(c % SUBLANES == 0 for c in starts)
    n = len(starts) * MM_COLS
    grid_spec = pltpu.PrefetchScalarGridSpec(
        num_scalar_prefetch=1,
        grid=(m // MM_ROWS, len(starts)),
        in_specs=[pl.BlockSpec((MM_ROWS, k), lambda i, j, st: (i, 0)),
                  pl.BlockSpec((pl.Element(1), pl.Element(MM_COLS), pl.Element(k)),
                               lambda i, j, st: (layer, st[j] * SUBLANES, 0)),
                  pl.BlockSpec((1, MM_COLS), lambda i, j, st: (0, j))],
        out_specs=pl.BlockSpec((MM_ROWS, MM_COLS), lambda i, j, st: (i, j)),
    )
    return pl.pallas_call(
        _in_proj_kernel,
        grid_spec=grid_spec,
        out_shape=jax.ShapeDtypeStruct((m, n), out_dtype),
        compiler_params=_params(("parallel", "arbitrary")),
        name="in_projection",
    )(jnp.asarray([c // SUBLANES for c in starts], jnp.int32), a, wt, col_scale)


def _mm_res_kernel(a_ref, w_ref, r_ref, o_ref):
    o_ref[...] = r_ref[...] + jnp.dot(a_ref[...], w_ref[0], preferred_element_type=F32)


def matmul_residual(a, w, layer, res):
    m, k = a.shape
    n = w.shape[2]
    return pl.pallas_call(
        _mm_res_kernel,
        grid=(m // MM_ROWS, n // MM_COLS),
        in_specs=[pl.BlockSpec((MM_ROWS, k), lambda i, j: (i, 0)),
                  pl.BlockSpec((1, k, MM_COLS), lambda i, j: (layer, 0, j)),
                  pl.BlockSpec((MM_ROWS, MM_COLS), lambda i, j: (i, j))],
        out_specs=pl.BlockSpec((MM_ROWS, MM_COLS), lambda i, j: (i, j)),
        out_shape=jax.ShapeDtypeStruct((m, n), F32),
        compiler_params=_params(("parallel", "arbitrary")),
        name="matmul_residual",
    )(a, w, res)


def _merge_kernel(b0_ref, b1_ref, b2_ref, b3_ref, p_ref, g0_ref, g1_ref, g2_ref, g3_ref, o_ref):
    acc = None
    for bi, (b_ref, g_ref) in enumerate(((b0_ref, g0_ref), (b1_ref, g1_ref),
                                         (b2_ref, g2_ref), (b3_ref, g3_ref))):
        y = jnp.dot(b_ref[...], p_ref[0, bi], preferred_element_type=F32)
        y = _sigmoid(g_ref[...].astype(F32)) * y
        acc = y if acc is None else acc + y
    o_ref[...] = acc.astype(o_ref.dtype)


def merge_branches(branches, proj, layer, z_a, gate_col0):
    t = branches[0].shape[0]
    g0 = gate_col0 // MM_COLS
    per = D_MODEL // MM_COLS
    b_spec = pl.BlockSpec((MERGE_ROWS, BRANCH_WIDTH), lambda i, j: (i, 0))
    g_specs = [pl.BlockSpec((MERGE_ROWS, MM_COLS), functools.partial(lambda i, j, o: (i, o + j), o=g0 + bi * per))
               for bi in range(N_BRANCHES)]
    return pl.pallas_call(
        _merge_kernel,
        grid=(t // MERGE_ROWS, per),
        in_specs=[b_spec] * N_BRANCHES
        + [pl.BlockSpec((1, N_BRANCHES, BRANCH_WIDTH, MM_COLS), lambda i, j: (layer, 0, 0, j))] + g_specs,
        out_specs=pl.BlockSpec((MERGE_ROWS, MM_COLS), lambda i, j: (i, j)),
        out_shape=jax.ShapeDtypeStruct((t, D_MODEL), BF16),
        compiler_params=_params(("parallel", "arbitrary")),
        name="merge_branches",
    )(*branches, proj, z_a, z_a, z_a, z_a)


def _ple_kernel(n_ref, wg_ref, p_ref, wp_ref, h_ref, o_ref):
    gate = _sigmoid(jnp.dot(n_ref[...], wg_ref[0], preferred_element_type=F32))
    emb = jnp.dot(p_ref[0], wp_ref[0], preferred_element_type=F32)
    o_ref[...] = h_ref[...] + gate * emb


def ple_update(n, wg, p, wp, layer, h):
    t = h.shape[0]
    return pl.pallas_call(
        _ple_kernel,
        grid=(t // PLE_ROWS, D_MODEL // MM_COLS),
        in_specs=[pl.BlockSpec((PLE_ROWS, D_MODEL), lambda i, j: (i, 0)),
                  pl.BlockSpec((1, D_MODEL, MM_COLS), lambda i, j: (layer, 0, j)),
                  pl.BlockSpec((1, PLE_ROWS, PLE_DIM), lambda i, j: (layer, i, 0)),
                  pl.BlockSpec((1, PLE_DIM, MM_COLS), lambda i, j: (layer, 0, j)),
                  pl.BlockSpec((PLE_ROWS, MM_COLS), lambda i, j: (i, j))],
        out_specs=pl.BlockSpec((PLE_ROWS, MM_COLS), lambda i, j: (i, j)),
        out_shape=jax.ShapeDtypeStruct((t, D_MODEL), F32),
        compiler_params=_params(("parallel", "arbitrary")),
        name="ple_update",
    )(n, wg, p, wp, h)


def _bf16_split3(x):
    hi = x.astype(BF16).astype(F32)
    r = x - hi
    mid = r.astype(BF16).astype(F32)
    lo = (r - mid).astype(BF16).astype(F32)
    return hi, mid, lo


def _forget_prep_kernel(z_ref, b_ref, eq_ref, ek_ref, carry_ref):
    @pl.when(pl.program_id(1) == 0)
    def _():
        carry_ref[...] = jnp.zeros_like(carry_ref)

    lf = _log_sigmoid(z_ref[0] + b_ref[...]) * LOG2E
    n = lf.shape[0]
    tril = (lax.broadcasted_iota(jnp.int32, (n, n), 0) >= lax.broadcasted_iota(jnp.int32, (n, n), 1)).astype(F32)
    c = jnp.dot(tril, lf, precision=HIGHEST, preferred_element_type=F32) + carry_ref[...]
    carry_ref[...] = c[n - 1:n, :]
    parts = _bf16_split3(c)
    lane = lax.broadcasted_iota(jnp.int32, (n, LANES), 1)
    for h in range(FOX_HEADS):
        hi, mid, lo = (jnp.sum(jnp.where(lane == h, part, 0.0), axis=-1, keepdims=True) for part in parts)
        eq = jnp.where(lane == 0, hi, jnp.where(lane == 1, mid, jnp.where(lane == 2, lo,
                       jnp.where(lane < 6, 1.0, 0.0))))
        ek = jnp.where(lane < 3, 1.0, jnp.where(lane == 3, -hi, jnp.where(lane == 4, -mid,
                       jnp.where(lane == 5, -lo, 0.0))))
        eq_ref[0, h] = eq.astype(BF16)
        ek_ref[0, h] = ek.astype(BF16)


def forget_prep(z_f, bias):
    b, s, _ = z_f.shape
    out = pl.BlockSpec((1, FOX_HEADS, CUMSUM_BLOCK, LANES), lambda bi, i: (bi, 0, i, 0))
    shape = jax.ShapeDtypeStruct((b, FOX_HEADS, s, LANES), BF16)
    return pl.pallas_call(
        _forget_prep_kernel,
        grid=(b, s // CUMSUM_BLOCK),
        in_specs=[pl.BlockSpec((1, CUMSUM_BLOCK, LANES), lambda bi, i: (bi, i, 0)),
                  pl.BlockSpec((1, LANES), lambda bi, i: (0, 0))],
        out_specs=[out, out],
        out_shape=[shape, shape],
        scratch_shapes=[pltpu.VMEM((1, LANES), F32)],
        compiler_params=_params(("parallel", "arbitrary")),
        name="forget_prep",
    )(z_f, bias)


def _fox_kernel(qi_ref, ki_ref, q_ref, k_ref, v_ref, eq_ref, ek_ref, o_ref, m_ref, l_ref, acc_ref):
    step = pl.program_id(2)
    qi = qi_ref[step]
    ki = ki_ref[step]
    blk = ATT_BLOCK
    dh = FOX_HEAD_DIM
    qc = FOX_QUERY_CHUNK

    @pl.when(ki == 0)
    def _():
        m_ref[...] = jnp.full_like(m_ref, -jnp.inf)
        l_ref[...] = jnp.zeros_like(l_ref)
        acc_ref[...] = jnp.zeros_like(acc_ref)

    def update(masked):
        chains = [(hh, c) for hh in range(FOX_HEADS_PER_STEP) for c in range(blk // qc)]
        scores = []
        for hh, c in chains:
            cols = slice(hh * dh, (hh + 1) * dh)
            qs = slice(c * qc, (c + 1) * qc)
            k = jnp.concatenate([k_ref[0, :, cols], ek_ref[0, hh]], axis=1)
            q = jnp.concatenate([q_ref[0, qs, cols], eq_ref[0, hh, qs, :]], axis=1)
            s = lax.dot_general(k, q, (((1,), (1,)), ((), ())), preferred_element_type=F32)
            if masked:
                key = lax.broadcasted_iota(jnp.int32, (blk, qc), 0)
                qry = lax.broadcasted_iota(jnp.int32, (blk, qc), 1) + c * qc
                s = jnp.where(key <= qry, s, -jnp.inf)
            scores.append(s)
        probs = []
        for (hh, c), s in zip(chains, scores):
            qs = slice(c * qc, (c + 1) * qc)
            m_old = m_ref[hh, :, qs]
            m_new = jnp.maximum(m_old, jnp.max(s, axis=0, keepdims=True))
            alpha = jnp.exp2(m_old - m_new)
            p = jnp.exp2(s - m_new)
            l_ref[hh, :, qs] = alpha * l_ref[hh, :, qs] + jnp.sum(p, axis=0, keepdims=True)
            m_ref[hh, :, qs] = m_new
            probs.append((alpha, p.astype(BF16)))
        v_t = [jnp.transpose(v_ref[0, :, hh * dh:(hh + 1) * dh]) for hh in range(FOX_HEADS_PER_STEP)]
        for (hh, c), (alpha, p) in zip(chains, probs):
            qs = slice(c * qc, (c + 1) * qc)
            acc_ref[hh, :, qs] = alpha * acc_ref[hh, :, qs] + jnp.dot(v_t[hh], p,
                                                                      preferred_element_type=F32)

    @pl.when(ki < qi)
    def _():
        update(False)

    @pl.when(ki == qi)
    def _():
        update(True)
        for hh in range(FOX_HEADS_PER_STEP):
            o_ref[0, :, hh * dh:(hh + 1) * dh] = jnp.transpose(acc_ref[hh] / l_ref[hh]).astype(o_ref.dtype)


def fox_attention(z_a, eq, ek):
    b, s, _ = z_a.shape
    nb = s // ATT_BLOCK
    hp = FOX_HEADS_PER_STEP
    groups = FOX_HEADS // hp
    pairs = [(qi, ki) for qi in range(nb) for ki in range(qi + 1)]
    qi_tab = jnp.asarray([pr[0] for pr in pairs], jnp.int32)
    ki_tab = jnp.asarray([pr[1] for pr in pairs], jnp.int32)
    width = hp * FOX_HEAD_DIM
    grid_spec = pltpu.PrefetchScalarGridSpec(
        num_scalar_prefetch=2,
        grid=(b, groups, len(pairs)),
        in_specs=[pl.BlockSpec((1, ATT_BLOCK, width), lambda bi, g, st, qt, kt: (bi, qt[st], g)),
                  pl.BlockSpec((1, ATT_BLOCK, width), lambda bi, g, st, qt, kt: (bi, kt[st], groups + g)),
                  pl.BlockSpec((1, ATT_BLOCK, width), lambda bi, g, st, qt, kt: (bi, kt[st], 2 * groups + g)),
                  pl.BlockSpec((1, hp, ATT_BLOCK, LANES), lambda bi, g, st, qt, kt: (bi, g, qt[st], 0)),
                  pl.BlockSpec((1, hp, ATT_BLOCK, LANES), lambda bi, g, st, qt, kt: (bi, g, kt[st], 0))],
        out_specs=pl.BlockSpec((1, ATT_BLOCK, width), lambda bi, g, st, qt, kt: (bi, qt[st], g)),
        scratch_shapes=[pltpu.VMEM((hp, 1, ATT_BLOCK), F32), pltpu.VMEM((hp, 1, ATT_BLOCK), F32),
                        pltpu.VMEM((hp, FOX_HEAD_DIM, ATT_BLOCK), F32)],
    )
    return pl.pallas_call(
        _fox_kernel,
        grid_spec=grid_spec,
        out_shape=jax.ShapeDtypeStruct((b, s, BRANCH_WIDTH), BF16),
        compiler_params=_params(("parallel", "parallel", "arbitrary")),
        name="fox_attention",
    )(qi_tab, ki_tab, z_a, z_a, z_a, eq, ek)


def _pool_kernel(u_ref, halo_ref, w_ref, scale_ref, o_ref):
    i = pl.program_id(1)
    rows = POOL_ROWS
    t = i * rows + lax.broadcasted_iota(jnp.int32, (rows, 1), 0)
    keep_halo = (i > 0).astype(F32)
    for g, win in enumerate(POOL_WINDOWS):
        cols = slice(g * POOL_GROUP, (g + 1) * POOL_GROUP)
        u = u_ref[0, :, cols]
        ext = jnp.concatenate([halo_ref[0, :, cols] * keep_halo, u], axis=0)
        span = 1
        while span < win:
            n = ext.shape[0]
            ext = ext[:n - span] + ext[span:]
            span *= 2
        window_sum = ext[POOL_HALO + 1 - win:POOL_HALO + 1 - win + rows]
        count = jnp.minimum(t + 1, win).astype(F32)
        d = (window_sum / count - u).astype(BF16)
        y = jnp.dot(d, w_ref[0, g], preferred_element_type=F32) * scale_ref[:, cols]
        o_ref[0, :, cols] = y.astype(o_ref.dtype)


def multiscale_pool(z_b, pool_w, layer, pool_scale):
    b, s, _ = z_b.shape
    per = POOL_ROWS // POOL_HALO
    return pl.pallas_call(
        _pool_kernel,
        grid=(b, s // POOL_ROWS),
        in_specs=[pl.BlockSpec((1, POOL_ROWS, BRANCH_WIDTH), lambda bi, i: (bi, i, 0)),
                  pl.BlockSpec((1, POOL_HALO, BRANCH_WIDTH), lambda bi, i: (bi, jnp.maximum(i * per - 1, 0), 0)),
                  pl.BlockSpec((1, len(POOL_WINDOWS), POOL_GROUP, POOL_GROUP), lambda bi, i: (layer, 0, 0, 0)),
                  pl.BlockSpec((1, BRANCH_WIDTH), lambda bi, i: (0, 0))],
        out_specs=pl.BlockSpec((1, POOL_ROWS, BRANCH_WIDTH), lambda bi, i: (bi, i, 0)),
        out_shape=jax.ShapeDtypeStruct((b, s, BRANCH_WIDTH), BF16),
        compiler_params=_params(("parallel", "parallel")),
        name="multiscale_pool",
    )(z_b, z_b, pool_w, pool_scale.reshape(1, BRANCH_WIDTH))


def _sgu_kernel(u_ref, v_ref, nw_ref, w_ref, b_ref, o_ref):
    c = SGU_CHUNK
    v = jax.nn.gelu(v_ref[...].astype(F32))
    vc = v - jnp.mean(v, axis=-1, keepdims=True)
    vn = (vc * lax.rsqrt(jnp.mean(vc * vc, axis=-1, keepdims=True) + EPS) * nw_ref[...]).astype(BF16)
    causal = lax.broadcasted_iota(jnp.int32, (c, c), 0) >= lax.broadcasted_iota(jnp.int32, (c, c), 1)
    for g in range(SGU_GROUPS):
        cols = slice(g * SGU_GROUP_WIDTH, (g + 1) * SGU_GROUP_WIDTH)
        w = jnp.where(causal, w_ref[g], 0.0).astype(BF16)
        bias = b_ref[g]
        for n in range(SGU_ROWS // c):
            rows = slice(n * c, (n + 1) * c)
            sv = jnp.dot(w, vn[rows, cols], preferred_element_type=F32) + bias
            u = jax.nn.gelu(u_ref[rows, cols].astype(F32))
            o_ref[rows, cols] = (u * sv).astype(o_ref.dtype)


def spatial_gating(z_a, u_col0, norm_w, w_s, b_s):
    t = z_a.shape[0]
    ub = u_col0 // BRANCH_WIDTH
    return pl.pallas_call(
        _sgu_kernel,
        grid=(t // SGU_ROWS,),
        in_specs=[pl.BlockSpec((SGU_ROWS, BRANCH_WIDTH), lambda i: (i, ub)),
                  pl.BlockSpec((SGU_ROWS, BRANCH_WIDTH), lambda i: (i, ub + 1)),
                  pl.BlockSpec((1, BRANCH_WIDTH), lambda i: (0, 0)),
                  pl.BlockSpec((SGU_GROUPS, SGU_CHUNK, SGU_CHUNK), lambda i: (0, 0, 0)),
                  pl.BlockSpec((SGU_GROUPS, SGU_CHUNK, 1), lambda i: (0, 0, 0))],
        out_specs=pl.BlockSpec((SGU_ROWS, BRANCH_WIDTH), lambda i: (i, 0)),
        out_shape=jax.ShapeDtypeStruct((t, BRANCH_WIDTH), BF16),
        compiler_params=_params(("parallel",)),
        name="spatial_gating",
    )(z_a, z_a, norm_w.reshape(1, BRANCH_WIDTH), w_s, b_s.reshape(SGU_GROUPS, SGU_CHUNK, 1))


def _hgrn_kernel(q_ref, f_ref, v_ref, g_ref, lb_ref, nw_ref, o_ref, state_ref):
    c = HGRN_SUBCHUNK
    half = c // 2
    rows_total = HGRN_ROWS
    n = rows_total // c

    @pl.when(pl.program_id(2) == 0)
    def _():
        state_ref[...] = jnp.zeros_like(state_ref)

    local = lax.broadcasted_iota(jnp.int32, (rows_total, 1), 0) & (c - 1)
    t_half = lax.broadcasted_iota(jnp.int32, (1, half, 1), 1)
    for hh in range(HGRN_HEADS_PER_STEP):
        cols = slice(hh * LANES, (hh + 1) * LANES)
        lb = lb_ref[hh]
        q = q_ref[0, :, cols]
        fl = f_ref[0, :, cols]
        v = v_ref[0, :, cols]
        e = jnp.exp(-jnp.abs(fl))
        r = 1.0 / (1.0 + e)
        sig_pos = jnp.where(fl >= 0, r, e * r)
        sig_neg = jnp.where(fl >= 0, e * r, r)
        kk = (1.0 - lb) * sig_neg
        b = jnp.log2(lb + (1.0 - lb) * sig_pos)
        shift = 1
        while shift < c:
            b = b + jnp.where(local >= shift, pltpu.roll(b, shift, 0), 0.0)
            shift *= 2
        b3 = b.reshape(n, c, LANES)
        kk3 = kk.reshape(n, c, LANES)
        q3 = q.reshape(n, c, LANES)
        v3 = v.reshape(n, c, LANES)
        tot3 = b3[:, c - 1:c, :]
        q_dec = (q * jnp.exp2(b)).astype(BF16)
        k_dec = (kk3 * jnp.exp2(tot3 - b3)).reshape(rows_total, LANES).astype(BF16)
        decay = jnp.exp2(tot3)
        v_bf = v.astype(BF16)

        updates = [lax.dot_general(v_bf[i * c:(i + 1) * c], k_dec[i * c:(i + 1) * c], (((0,), (0,)), ((), ())),
                                   preferred_element_type=F32) for i in range(n)]
        state_t = state_ref[hh]
        states = []
        for i in range(n):
            states.append(state_t.astype(BF16))
            state_t = state_t * decay[i] + updates[i]
        state_ref[hh] = state_t
        outs = [lax.dot_general(q_dec[i * c:(i + 1) * c], states[i], (((1,), (1,)), ((), ())),
                                preferred_element_type=F32) for i in range(n)]

        b_lo, b_hi = b3[:, :half, :], b3[:, half:, :]
        q_lo, q_hi = q3[:, :half, :], q3[:, half:, :]
        o_lo = jnp.zeros((n, half, LANES), F32)
        o_hi = jnp.zeros((n, half, LANES), F32)
        for s in range(c):
            ks = kk3[:, s:s + 1, :]
            bs = b3[:, s:s + 1, :]
            vs = v3[:, s:s + 1, :]
            if s < half:
                d = jnp.where(t_half >= s, b_lo - bs, -jnp.inf)
                a = jnp.sum(q_lo * ks * jnp.exp2(d), axis=-1, keepdims=True)
                o_lo = o_lo + a * vs
                d = b_hi - bs
            else:
                d = jnp.where(t_half >= s - half, b_hi - bs, -jnp.inf)
            a = jnp.sum(q_hi * ks * jnp.exp2(d), axis=-1, keepdims=True)
            o_hi = o_hi + a * vs
        o = jnp.concatenate([o_lo, o_hi], axis=1).reshape(rows_total, LANES) + jnp.concatenate(outs, axis=0)

        g = g_ref[0, :, cols]
        o = o * lax.rsqrt(jnp.mean(o * o, axis=-1, keepdims=True) + EPS) * nw_ref[hh]
        o_ref[0, :, cols] = (o * (g * _sigmoid(g))).astype(o_ref.dtype)


def hgrn2(z_b, col0, lower_bound, norm_w):
    b, s, _ = z_b.shape
    hp = HGRN_HEADS_PER_STEP
    width = hp * LANES
    c0 = col0 // width
    groups = HGRN_HEADS // hp

    def spec(part):
        return pl.BlockSpec((1, HGRN_ROWS, width), lambda bi, h, i: (bi, i, c0 + part * groups + h))

    vec = pl.BlockSpec((hp, 1, LANES), lambda bi, h, i: (h, 0, 0))
    return pl.pallas_call(
        _hgrn_kernel,
        grid=(b, groups, s // HGRN_ROWS),
        in_specs=[spec(0), spec(1), spec(2), spec(3), vec, vec],
        out_specs=pl.BlockSpec((1, HGRN_ROWS, width), lambda bi, h, i: (bi, i, h)),
        out_shape=jax.ShapeDtypeStruct((b, s, BRANCH_WIDTH), BF16),
        scratch_shapes=[pltpu.VMEM((hp, HGRN_VAL_DIM, HGRN_KEY_DIM), F32)],
        compiler_params=_params(("parallel", "parallel", "arbitrary")),
        name="hgrn2",
    )(z_b, z_b, z_b, z_b, lower_bound.reshape(HGRN_HEADS, 1, LANES), norm_w.reshape(HGRN_HEADS, 1, LANES))


def _router_kernel(x_ref, w_ref, wr_ref, br_ref, n_ref, meta_ref, p_ref, cnt_ref, carry_ref):
    @pl.when(pl.program_id(0) == 0)
    def _():
        carry_ref[...] = jnp.zeros_like(carry_ref)

    n = _rms(x_ref[...], w_ref[...])
    n_ref[...] = n
    logits = jnp.dot(n, wr_ref[...], precision=HIGHEST, preferred_element_type=F32) + br_ref[...]
    rows = logits.shape[0]
    col = lax.broadcasted_iota(jnp.int32, logits.shape, 1)
    big = jnp.int32(LANES)
    neg = -jnp.inf

    def first_argmax(vals):
        top = jnp.max(vals, axis=-1, keepdims=True)
        return top, jnp.min(jnp.where(vals == top, col, big), axis=-1, keepdims=True)

    group_logits = jnp.where(col < N_GROUPS, logits, neg)
    g_top, g_idx = first_argmax(group_logits)
    p_group = 1.0 / jnp.sum(jnp.exp(group_logits - g_top), axis=-1, keepdims=True)
    lo = N_GROUPS + g_idx * EXPERTS_PER_GROUP
    in_group = jnp.where((col >= lo) & (col < lo + EXPERTS_PER_GROUP), logits, neg)
    top1, idx1 = first_argmax(in_group)
    top2, idx2 = first_argmax(jnp.where(col == idx1, neg, in_group))
    r = jnp.exp(top2 - top1)
    p1 = p_group / (1.0 + r)
    p2 = p_group * r / (1.0 + r)
    p_ref[...] = jnp.where(col == 0, p1, jnp.where(col == 1, p2, 0.0))

    chosen = ((col == idx1) | (col == idx2)).astype(BF16)
    earlier = (lax.broadcasted_iota(jnp.int32, (rows, rows), 0)
               > lax.broadcasted_iota(jnp.int32, (rows, rows), 1)).astype(BF16)
    before = jnp.dot(earlier, chosen, preferred_element_type=F32) + carry_ref[...]
    rank1 = jnp.sum(jnp.where(col == idx1, before, 0.0), axis=-1, keepdims=True).astype(jnp.int32)
    rank2 = jnp.sum(jnp.where(col == idx2, before, 0.0), axis=-1, keepdims=True).astype(jnp.int32)
    carry_ref[...] += jnp.sum(chosen.astype(F32), axis=0, keepdims=True)
    cnt_ref[...] = carry_ref[...]
    meta_ref[...] = jnp.where(col == 0, idx1 - N_GROUPS, jnp.where(col == 1, idx2 - N_GROUPS,
                              jnp.where(col == 2, rank1, jnp.where(col == 3, rank2, 0))))


def moe_router(h, norm_w, w_router, b_router):
    t, d = h.shape
    row = pl.BlockSpec((NORM_ROWS, d), lambda i: (i, 0))
    small = pl.BlockSpec((NORM_ROWS, LANES), lambda i: (i, 0))
    one = pl.BlockSpec((1, LANES), lambda i: (0, 0))
    return pl.pallas_call(
        _router_kernel,
        grid=(t // NORM_ROWS,),
        in_specs=[row, pl.BlockSpec((1, d), lambda i: (0, 0)), pl.BlockSpec((d, LANES), lambda i: (0, 0)), one],
        out_specs=[row, small, small, one],
        out_shape=[jax.ShapeDtypeStruct((t, d), F32), jax.ShapeDtypeStruct((t, LANES), jnp.int32),
                   jax.ShapeDtypeStruct((t, LANES), F32), jax.ShapeDtypeStruct((1, LANES), F32)],
        scratch_shapes=[pltpu.VMEM((1, LANES), F32)],
        compiler_params=_params(("arbitrary",)),
        name="moe_router",
    )(h, norm_w.reshape(1, d), w_router, b_router)


def _gather_rows(idx_ref, src_hbm, dst_ref, sem, unrolled):
    n_rows = dst_ref.shape[0]

    def start(r):
        pltpu.make_async_copy(src_hbm.at[pl.ds(idx_ref[r], 1)], dst_ref.at[pl.ds(r, 1)], sem).start()

    if unrolled:
        for r in range(n_rows):
            start(r)
    else:
        def body(r, carry):
            start(r)
            return carry
        lax.fori_loop(0, n_rows, body, 0)


def _wait_rows(src_hbm, dst_ref, sem):
    pltpu.make_async_copy(src_hbm.at[pl.ds(0, dst_ref.shape[0])], dst_ref, sem).wait()


def _expert_kernel(be_ref, nused_ref, tok_cur_ref, tok_next_ref, x_hbm, wg_ref, wu_ref, wd_ref, y_ref,
                   xbuf_ref, sem_ref):
    i = pl.program_id(0)
    last = pl.num_programs(0) - 1
    n_used = nused_ref[0]
    slot = i % 2

    @pl.when((i == 0) & (n_used > 0))
    def _():
        _gather_rows(tok_cur_ref, x_hbm, xbuf_ref.at[0], sem_ref.at[0], unrolled=False)

    @pl.when(i < n_used)
    def _():
        _gather_rows(tok_next_ref, x_hbm, xbuf_ref.at[1 - slot], sem_ref.at[1 - slot], unrolled=True)
        _wait_rows(x_hbm, xbuf_ref.at[slot], sem_ref.at[slot])
        x = xbuf_ref[slot].astype(BF16)
        gate = jnp.dot(x, wg_ref[0, 0], preferred_element_type=F32)
        up = jnp.dot(x, wu_ref[0, 0], preferred_element_type=F32)
        hidden = (gate * _sigmoid(gate) * up).astype(BF16)
        y_ref[...] = jnp.dot(hidden, wd_ref[0, 0], preferred_element_type=F32)

    @pl.when((i == n_used) & (n_used > 0))
    def _():
        _wait_rows(x_hbm, xbuf_ref.at[slot], sem_ref.at[slot])

    @pl.when((i == last) & (n_used > last))
    def _():
        _wait_rows(x_hbm, xbuf_ref.at[1 - slot], sem_ref.at[1 - slot])

    @pl.when(i >= n_used)
    def _():
        y_ref[...] = jnp.zeros_like(y_ref)


def expert_mlp(x, buf_tok, block_e, n_used, w_gate, w_up, w_down, layer):
    p = buf_tok.shape[0]
    nb = p // MOE_ROWS
    d = x.shape[1]
    smem = pltpu.SMEM
    grid_spec = pltpu.PrefetchScalarGridSpec(
        num_scalar_prefetch=2,
        grid=(nb,),
        in_specs=[pl.BlockSpec((MOE_ROWS,), lambda i, be, nu: (i,), memory_space=smem),
                  pl.BlockSpec((MOE_ROWS,), lambda i, be, nu: (jnp.minimum(i + 1, nb - 1),), memory_space=smem),
                  pl.BlockSpec(memory_space=pl.ANY),
                  pl.BlockSpec((1, 1, d, EXPERT_HIDDEN), lambda i, be, nu: (layer, be[i], 0, 0)),
                  pl.BlockSpec((1, 1, d, EXPERT_HIDDEN), lambda i, be, nu: (layer, be[i], 0, 0)),
                  pl.BlockSpec((1, 1, EXPERT_HIDDEN, d), lambda i, be, nu: (layer, be[i], 0, 0))],
        out_specs=pl.BlockSpec((MOE_ROWS, d), lambda i, be, nu: (i, 0)),
        scratch_shapes=[pltpu.VMEM((2, MOE_ROWS, d), F32), pltpu.SemaphoreType.DMA((2,))],
    )
    return pl.pallas_call(
        _expert_kernel,
        grid_spec=grid_spec,
        out_shape=jax.ShapeDtypeStruct((p, d), F32),
        compiler_params=_params(("arbitrary",)),
        name="expert_mlp",
    )(block_e, n_used, buf_tok, buf_tok, x, w_gate, w_up, w_down)


def _combine_kernel(pos0_ref, pos1_ref, pos0_next_ref, pos1_next_ref, y_hbm, h_ref, p_ref, w_ref, o_ref, n_ref,
                    ybuf_ref, sem_ref):
    i = pl.program_id(0)
    last = pl.num_programs(0) - 1
    slot = i % 2

    @pl.when(i == 0)
    def _():
        _gather_rows(pos0_ref, y_hbm, ybuf_ref.at[0, 0], sem_ref.at[0, 0], unrolled=False)
        _gather_rows(pos1_ref, y_hbm, ybuf_ref.at[0, 1], sem_ref.at[0, 1], unrolled=False)

    _gather_rows(pos0_next_ref, y_hbm, ybuf_ref.at[1 - slot, 0], sem_ref.at[1 - slot, 0], unrolled=True)
    _gather_rows(pos1_next_ref, y_hbm, ybuf_ref.at[1 - slot, 1], sem_ref.at[1 - slot, 1], unrolled=True)
    _wait_rows(y_hbm, ybuf_ref.at[slot, 0], sem_ref.at[slot, 0])
    _wait_rows(y_hbm, ybuf_ref.at[slot, 1], sem_ref.at[slot, 1])
    p = p_ref[...]
    h = h_ref[...] + (ybuf_ref[slot, 0] * p[:, 0:1] + ybuf_ref[slot, 1] * p[:, 1:2])
    o_ref[...] = h
    n_ref[...] = _rms(h, w_ref[...]).astype(n_ref.dtype)

    @pl.when(i == last)
    def _():
        _wait_rows(y_hbm, ybuf_ref.at[1 - slot, 0], sem_ref.at[1 - slot, 0])
        _wait_rows(y_hbm, ybuf_ref.at[1 - slot, 1], sem_ref.at[1 - slot, 1])


def moe_combine(y, pos0, pos1, probs, h, norm_w):
    t, d = h.shape
    nb = t // COMBINE_ROWS
    smem = pltpu.SMEM
    row = pl.BlockSpec((COMBINE_ROWS, d), lambda i: (i, 0))
    cur = pl.BlockSpec((COMBINE_ROWS,), lambda i: (i,), memory_space=smem)
    nxt = pl.BlockSpec((COMBINE_ROWS,), lambda i: (jnp.minimum(i + 1, nb - 1),), memory_space=smem)
    return pl.pallas_call(
        _combine_kernel,
        grid=(nb,),
        in_specs=[cur, cur, nxt, nxt,
                  pl.BlockSpec(memory_space=pl.ANY),
                  row,
                  pl.BlockSpec((COMBINE_ROWS, LANES), lambda i: (i, 0)),
                  pl.BlockSpec((1, d), lambda i: (0, 0))],
        out_specs=[row, row],
        out_shape=[jax.ShapeDtypeStruct((t, d), F32), jax.ShapeDtypeStruct((t, d), BF16)],
        scratch_shapes=[pltpu.VMEM((2, TOP_K, COMBINE_ROWS, d), F32), pltpu.SemaphoreType.DMA((2, TOP_K))],
        compiler_params=_params(("arbitrary",)),
        name="moe_combine",
    )(pos0, pos1, pos0, pos1, y, h, probs, norm_w.reshape(1, d))


def _dispatch_plan(meta, counts):
    t = meta.shape[0]
    a = t * TOP_K
    expert_ids = meta[:, :TOP_K]
    ranks = meta[:, TOP_K:2 * TOP_K]
    counts = counts[0, N_GROUPS:N_GROUPS + N_EXPERTS].astype(jnp.int32)
    padded = (counts + MOE_ROWS - 1) // MOE_ROWS * MOE_ROWS
    pad_end = jnp.cumsum(padded)
    pad_start = pad_end - padded
    dest = pad_start[expert_ids] + ranks
    p = a + N_EXPERTS * MOE_ROWS
    nb = p // MOE_ROWS
    token_ids = jnp.repeat(jnp.arange(t, dtype=jnp.int32), TOP_K)
    buf_tok = jnp.zeros((p,), jnp.int32).at[dest.reshape(a)].set(token_ids, unique_indices=True)
    block_start = jnp.arange(nb, dtype=jnp.int32) * MOE_ROWS
    block_e = jnp.minimum(jnp.sum((pad_end[None, :] <= block_start[:, None]).astype(jnp.int32), axis=1),
                          N_EXPERTS - 1)
    n_used = (pad_end[-1:] // MOE_ROWS).astype(jnp.int32)
    return buf_tok, block_e, n_used, dest[:, 0], dest[:, 1]


def _pad_cols(w, n):
    return jnp.pad(w, ((0, 0), (0, n - w.shape[1])))


def _router_weights(i, router_group_w, router_group_b, router_expert_w, router_expert_b):
    w_r = _pad_cols(jnp.concatenate([router_group_w[i], router_expert_w[i]], axis=1), LANES)
    b_r = _pad_cols(jnp.concatenate([router_group_b[i], router_expert_b[i]])[None, :], LANES)
    return w_r, b_r


def kernel(x, p, mix_norm_w, w_in, fox_f_bias, pool_w, pool_scale, sgu_norm_w, sgu_w, sgu_b,
           hgrn_lb_logits, hgrn_norm_w, branch_proj, w_out, ffn_norm_w, router_group_w,
           router_group_b, router_expert_w, router_expert_b, expert_w_gate, expert_w_up,
           expert_w_down, ple_norm_w, ple_gate_w, ple_proj_w, final_norm_w):
    bsz, seq, d = x.shape
    t = bsz * seq
    lb_p = jax.nn.softmax(hgrn_lb_logits.astype(F32), axis=0)
    lower_bounds = jnp.cumsum(lb_p, axis=0) - lb_p[0:1]
    sgu_u_col = FOX_QKV_END
    gate_col = FOX_QKV_END + 2 * BRANCH_WIDTH
    h = x.reshape(t, d)
    w_in_t = jnp.transpose(w_in, (0, 2, 1))

    def tiles(*ranges):
        return [c for lo, hi in ranges for c in range(lo, hi, MM_COLS)]

    starts_a = tiles((0, FOX_QKV_END), (POOL_END, SGU_END), (HGRN_END, IN_COLS))
    starts_b = tiles((FOX_F_END, POOL_END), (SGU_END, HGRN_END))
    n_a = len(starts_a) * MM_COLS
    n_b = len(starts_b) * MM_COLS
    q_scale = jnp.where(jnp.arange(n_a) < BRANCH_WIDTH, FOX_HEAD_DIM ** -0.5 * LOG2E, 1.0).astype(F32)[None, :]
    w_gate_bf, w_up_bf, w_down_bf = (w.astype(BF16) for w in (expert_w_gate, expert_w_up, expert_w_down))
    pool_w_bf, branch_proj_bf, w_out_bf, ple_gate_bf, ple_proj_bf = (
        w.astype(BF16) for w in (pool_w, branch_proj, w_out, ple_gate_w, ple_proj_w))
    p_bf = p.reshape(DEPTH, t, PLE_DIM).astype(BF16)
    for i in range(DEPTH):
        w_r, b_r = _router_weights(i, router_group_w, router_group_b, router_expert_w, router_expert_b)
        n, z_f = rms_norm_proj(h, mix_norm_w[i], w_in_t, i, FOX_QKV_END, BF16)
        z_a = in_projection(n, w_in_t, i, starts_a, q_scale, BF16)
        z_b = in_projection(n, w_in_t, i, starts_b, jnp.ones((1, n_b), F32), F32)
        z_a3 = z_a.reshape(bsz, seq, -1)
        z_b3 = z_b.reshape(bsz, seq, -1)

        f_bias = _pad_cols(fox_f_bias[i][None, :].astype(F32), LANES)
        eq, ek = forget_prep(z_f.reshape(bsz, seq, LANES), f_bias)
        br_fox = fox_attention(z_a3, eq, ek).reshape(t, BRANCH_WIDTH)
        br_pool = multiscale_pool(z_b3, pool_w_bf, i, pool_scale[i]).reshape(t, BRANCH_WIDTH)
        br_sgu = spatial_gating(z_a, sgu_u_col, sgu_norm_w[i], sgu_w[i], sgu_b[i])
        br_hgrn = hgrn2(z_b3, BRANCH_WIDTH, lower_bounds[i], hgrn_norm_w[i]).reshape(t, BRANCH_WIDTH)

        merged = merge_branches((br_fox, br_pool, br_sgu, br_hgrn), branch_proj_bf, i, z_a, gate_col)
        h = matmul_residual(merged, w_out_bf, i, h)

        xn, meta, probs, counts = moe_router(h, ffn_norm_w[i], w_r, b_r)
        buf_tok, block_e, n_used, pos0, pos1 = _dispatch_plan(meta, counts)
        y = expert_mlp(xn, buf_tok, block_e, n_used, w_gate_bf, w_up_bf, w_down_bf, i)
        h, n_ple = moe_combine(y, pos0, pos1, probs, h, ple_norm_w[i])

        h = ple_update(n_ple, ple_gate_bf, p_bf, ple_proj_bf, i, h)
    return rms_norm(h, final_norm_w, F32).reshape(bsz, seq, d)
```

```python
import functools

import jax
import jax.numpy as jnp
from jax import lax
from jax.experimental import pallas as pl
from jax.experimental.pallas import tpu as pltpu

F32 = jnp.float32
BF16 = jnp.bfloat16
HIGHEST = lax.Precision.HIGHEST

D_MODEL = 2048
DEPTH = 2
BRANCH_WIDTH = D_MODEL // 2
N_BRANCHES = 4
PLE_DIM = 256
EPS = 1e-6
FOX_HEAD_DIM = 128
FOX_HEADS = BRANCH_WIDTH // FOX_HEAD_DIM
POOL_WINDOWS = (2, 4, 8, 16)
POOL_GROUP = BRANCH_WIDTH // len(POOL_WINDOWS)
POOL_HALO = 16
SGU_GROUPS = 4
SGU_CHUNK = 128
SGU_GROUP_WIDTH = BRANCH_WIDTH // SGU_GROUPS
HGRN_KEY_DIM = 128
HGRN_VAL_DIM = 128
HGRN_HEADS = BRANCH_WIDTH // HGRN_VAL_DIM
HGRN_SUBCHUNK = 16
HGRN_HEADS_PER_STEP = 2
N_GROUPS = 4
EXPERTS_PER_GROUP = 8
N_EXPERTS = N_GROUPS * EXPERTS_PER_GROUP
TOP_K = 2
EXPERT_HIDDEN = D_MODEL // 2
FOX_QKV_END = 3 * BRANCH_WIDTH
FOX_F_END = FOX_QKV_END + FOX_HEADS
POOL_END = FOX_F_END + BRANCH_WIDTH
SGU_END = POOL_END + 2 * BRANCH_WIDTH
HGRN_END = SGU_END + 4 * BRANCH_WIDTH
IN_COLS = HGRN_END + N_BRANCHES * D_MODEL

LANES = 128
SUBLANES = 8
LOG2E = 1.4426950408889634
FOX_HEADS_PER_STEP = 8
FOX_QUERY_CHUNK = 256
VMEM_LIMIT = 56 * 1024 * 1024

NORM_ROWS = 512
MM_ROWS = 2048
MM_COLS = 512
CAST_ROWS = 512
MERGE_ROWS = 1024
ATT_BLOCK = 512
CUMSUM_BLOCK = 512
POOL_ROWS = 512
SGU_ROWS = 512
HGRN_ROWS = 256
MOE_ROWS = 256
COMBINE_ROWS = 256
PLE_ROWS = 1024


def _params(semantics, **kw):
    return pltpu.CompilerParams(dimension_semantics=semantics, vmem_limit_bytes=VMEM_LIMIT, **kw)


def _sigmoid(x):
    return 1.0 / (1.0 + jnp.exp(-x))


def _log_sigmoid(x):
    return jnp.minimum(x, 0.0) - jnp.log(1.0 + jnp.exp(-jnp.abs(x)))


def _rms(x, w):
    return x * lax.rsqrt(jnp.mean(x * x, axis=-1, keepdims=True) + EPS) * w


def _norm_kernel(x_ref, w_ref, n_ref):
    n_ref[...] = _rms(x_ref[...], w_ref[...]).astype(n_ref.dtype)


def rms_norm(x, w, out_dtype):
    t, d = x.shape
    return pl.pallas_call(
        _norm_kernel,
        grid=(t // NORM_ROWS,),
        in_specs=[pl.BlockSpec((NORM_ROWS, d), lambda i: (i, 0)),
                  pl.BlockSpec((1, d), lambda i: (0, 0))],
        out_specs=pl.BlockSpec((NORM_ROWS, d), lambda i: (i, 0)),
        out_shape=jax.ShapeDtypeStruct((t, d), out_dtype),
        compiler_params=_params(("parallel",)),
        name="rms_norm",
    )(x, w.reshape(1, d))


def _norm_proj_kernel(x_ref, w_ref, wpt_ref, n_ref, p_ref):
    n = _rms(x_ref[...], w_ref[...]).astype(BF16)
    n_ref[...] = n.astype(n_ref.dtype)
    p_ref[...] = lax.dot_general(n, wpt_ref[0].astype(BF16), (((1,), (1,)), ((), ())), preferred_element_type=F32)


def rms_norm_proj(x, w, wt, layer, row0, out_dtype):
    t, d = x.shape
    jp = row0 // LANES
    return pl.pallas_call(
        _norm_proj_kernel,
        grid=(t // NORM_ROWS,),
        in_specs=[pl.BlockSpec((NORM_ROWS, d), lambda i: (i, 0)),
                  pl.BlockSpec((1, d), lambda i: (0, 0)),
                  pl.BlockSpec((1, LANES, d), lambda i: (layer, jp, 0))],
        out_specs=[pl.BlockSpec((NORM_ROWS, d), lambda i: (i, 0)),
                   pl.BlockSpec((NORM_ROWS, LANES), lambda i: (i, 0))],
        out_shape=[jax.ShapeDtypeStruct((t, d), out_dtype),
                   jax.ShapeDtypeStruct((t, LANES), F32)],
        compiler_params=_params(("parallel",)),
        name="rms_norm_proj",
    )(x, w.reshape(1, d), wt)


def _in_proj_kernel(start_tiles_ref, a_ref, wt_ref, s_ref, *rest, n_cast):
    cast_in, o_ref, cast_out = rest[:n_cast], rest[n_cast], rest[n_cast + 1:]
    w = wt_ref[0].astype(BF16)
    acc = lax.dot_general(a_ref[...], w, (((1,), (1,)), ((), ())), preferred_element_type=F32)
    o_ref[...] = (acc * s_ref[...]).astype(o_ref.dtype)
    for src_ref, dst_ref in zip(cast_in, cast_out):
        dst_ref[...] = src_ref[0].astype(dst_ref.dtype)


def in_projection(a, wt, layer, starts, col_scale, out_dtype, cast=()):
    m, k = a.shape
    assert all(c % SUBLANES == 0 for c in starts)
    n = len(starts) * MM_COLS
    nj = len(starts)
    n_steps = (m // MM_ROWS) * nj
    cast_specs_in, cast_specs_out, cast_shapes = [], [], []
    for arr, rows in cast:
        _, e, r, c = arr.shape
        per = r // rows
        n_blocks = e * per
        assert n_blocks <= n_steps

        def block(i, j, st, per=per, n_blocks=n_blocks):
            kk = jnp.minimum(i * nj + j, n_blocks - 1)
            return kk // per, kk % per

        cast_specs_in.append(pl.BlockSpec((1, 1, rows, c), lambda i, j, st, block=block: (layer, *block(i, j, st), 0)))
        cast_specs_out.append(pl.BlockSpec((1, rows, c), lambda i, j, st, block=block: (*block(i, j, st), 0)))
        cast_shapes.append(jax.ShapeDtypeStruct((e, r, c), BF16))
    grid_spec = pltpu.PrefetchScalarGridSpec(
        num_scalar_prefetch=1,
        grid=(m // MM_ROWS, nj),
        in_specs=[pl.BlockSpec((MM_ROWS, k), lambda i, j, st: (i, 0)),
                  pl.BlockSpec((pl.Element(1), pl.Element(MM_COLS), pl.Element(k)),
                               lambda i, j, st: (layer, st[j] * SUBLANES, 0)),
                  pl.BlockSpec((1, MM_COLS), lambda i, j, st: (0, j))] + cast_specs_in,
        out_specs=[pl.BlockSpec((MM_ROWS, MM_COLS), lambda i, j, st: (i, j))] + cast_specs_out,
    )
    outs = pl.pallas_call(
        functools.partial(_in_proj_kernel, n_cast=len(cast)),
        grid_spec=grid_spec,
        out_shape=[jax.ShapeDtypeStruct((m, n), out_dtype)] + cast_shapes,
        compiler_params=_params(("arbitrary", "arbitrary")),
        name="in_projection",
    )(jnp.asarray([c // SUBLANES for c in starts], jnp.int32), a, wt, col_scale, *[arr for arr, _ in cast])
    return outs[0], outs[1:]


def _mm_res_kernel(a_ref, w_ref, r_ref, o_ref):
    o_ref[...] = r_ref[...] + jnp.dot(a_ref[...], w_ref[0], preferred_element_type=F32)


def matmul_residual(a, w, layer, res):
    m, k = a.shape
    n = w.shape[2]
    return pl.pallas_call(
        _mm_res_kernel,
        grid=(m // MM_ROWS, n // MM_COLS),
        in_specs=[pl.BlockSpec((MM_ROWS, k), lambda i, j: (i, 0)),
                  pl.BlockSpec((1, k, MM_COLS), lambda i, j: (layer, 0, j)),
                  pl.BlockSpec((MM_ROWS, MM_COLS), lambda i, j: (i, j))],
        out_specs=pl.BlockSpec((MM_ROWS, MM_COLS), lambda i, j: (i, j)),
        out_shape=jax.ShapeDtypeStruct((m, n), F32),
        compiler_params=_params(("parallel", "arbitrary")),
        name="matmul_residual",
    )(a, w, res)


def _merge_kernel(b0_ref, b1_ref, b2_ref, b3_ref, p_ref, g0_ref, g1_ref, g2_ref, g3_ref, o_ref):
    acc = None
    for bi, (b_ref, g_ref) in enumerate(((b0_ref, g0_ref), (b1_ref, g1_ref),
                                         (b2_ref, g2_ref), (b3_ref, g3_ref))):
        y = jnp.dot(b_ref[...], p_ref[0, bi], preferred_element_type=F32)
        y = _sigmoid(g_ref[...].astype(F32)) * y
        acc = y if acc is None else acc + y
    o_ref[...] = acc.astype(o_ref.dtype)


def merge_branches(branches, proj, layer, z_a, gate_col0):
    t = branches[0].shape[0]
    g0 = gate_col0 // MM_COLS
    per = D_MODEL // MM_COLS
    b_spec = pl.BlockSpec((MERGE_ROWS, BRANCH_WIDTH), lambda i, j: (i, 0))
    g_specs = [pl.BlockSpec((MERGE_ROWS, MM_COLS), functools.partial(lambda i, j, o: (i, o + j), o=g0 + bi * per))
               for bi in range(N_BRANCHES)]
    return pl.pallas_call(
        _merge_kernel,
        grid=(t // MERGE_ROWS, per),
        in_specs=[b_spec] * N_BRANCHES
        + [pl.BlockSpec((1, N_BRANCHES, BRANCH_WIDTH, MM_COLS), lambda i, j: (layer, 0, 0, j))] + g_specs,
        out_specs=pl.BlockSpec((MERGE_ROWS, MM_COLS), lambda i, j: (i, j)),
        out_shape=jax.ShapeDtypeStruct((t, D_MODEL), BF16),
        compiler_params=_params(("parallel", "arbitrary")),
        name="merge_branches",
    )(*branches, proj, z_a, z_a, z_a, z_a)


def _ple_kernel(n_ref, wg_ref, p_ref, wp_ref, h_ref, o_ref):
    gate = _sigmoid(jnp.dot(n_ref[...], wg_ref[0], preferred_element_type=F32))
    emb = jnp.dot(p_ref[0], wp_ref[0], preferred_element_type=F32)
    o_ref[...] = h_ref[...] + gate * emb


def ple_update(n, wg, p, wp, layer, h):
    t = h.shape[0]
    return pl.pallas_call(
        _ple_kernel,
        grid=(t // PLE_ROWS, D_MODEL // MM_COLS),
        in_specs=[pl.BlockSpec((PLE_ROWS, D_MODEL), lambda i, j: (i, 0)),
                  pl.BlockSpec((1, D_MODEL, MM_COLS), lambda i, j: (layer, 0, j)),
                  pl.BlockSpec((1, PLE_ROWS, PLE_DIM), lambda i, j: (layer, i, 0)),
                  pl.BlockSpec((1, PLE_DIM, MM_COLS), lambda i, j: (layer, 0, j)),
                  pl.BlockSpec((PLE_ROWS, MM_COLS), lambda i, j: (i, j))],
        out_specs=pl.BlockSpec((PLE_ROWS, MM_COLS), lambda i, j: (i, j)),
        out_shape=jax.ShapeDtypeStruct((t, D_MODEL), F32),
        compiler_params=_params(("parallel", "arbitrary")),
        name="ple_update",
    )(n, wg, p, wp, h)


def _bf16_split3(x):
    hi = x.astype(BF16).astype(F32)
    r = x - hi
    mid = r.astype(BF16).astype(F32)
    lo = (r - mid).astype(BF16).astype(F32)
    return hi, mid, lo


def _forget_prep_kernel(z_ref, b_ref, eq_ref, ek_ref, carry_ref):
    @pl.when(pl.program_id(1) == 0)
    def _():
        carry_ref[...] = jnp.zeros_like(carry_ref)

    lf = _log_sigmoid(z_ref[0] + b_ref[...]) * LOG2E
    n = lf.shape[0]
    tril = (lax.broadcasted_iota(jnp.int32, (n, n), 0) >= lax.broadcasted_iota(jnp.int32, (n, n), 1)).astype(F32)
    c = jnp.dot(tril, lf, precision=HIGHEST, preferred_element_type=F32) + carry_ref[...]
    carry_ref[...] = c[n - 1:n, :]
    parts = _bf16_split3(c)
    lane = lax.broadcasted_iota(jnp.int32, (n, LANES), 1)
    for h in range(FOX_HEADS):
        hi, mid, lo = (jnp.sum(jnp.where(lane == h, part, 0.0), axis=-1, keepdims=True) for part in parts)
        eq = jnp.where(lane == 0, hi, jnp.where(lane == 1, mid, jnp.where(lane == 2, lo,
                       jnp.where(lane < 6, 1.0, 0.0))))
        ek = jnp.where(lane < 3, 1.0, jnp.where(lane == 3, -hi, jnp.where(lane == 4, -mid,
                       jnp.where(lane == 5, -lo, 0.0))))
        eq_ref[0, h] = eq.astype(BF16)
        ek_ref[0, h] = ek.astype(BF16)


def forget_prep(z_f, bias):
    b, s, _ = z_f.shape
    out = pl.BlockSpec((1, FOX_HEADS, CUMSUM_BLOCK, LANES), lambda bi, i: (bi, 0, i, 0))
    shape = jax.ShapeDtypeStruct((b, FOX_HEADS, s, LANES), BF16)
    return pl.pallas_call(
        _forget_prep_kernel,
        grid=(b, s // CUMSUM_BLOCK),
        in_specs=[pl.BlockSpec((1, CUMSUM_BLOCK, LANES), lambda bi, i: (bi, i, 0)),
                  pl.BlockSpec((1, LANES), lambda bi, i: (0, 0))],
        out_specs=[out, out],
        out_shape=[shape, shape],
        scratch_shapes=[pltpu.VMEM((1, LANES), F32)],
        compiler_params=_params(("parallel", "arbitrary")),
        name="forget_prep",
    )(z_f, bias)


def _fox_kernel(qi_ref, ki_ref, q_ref, k_ref, v_ref, eq_ref, ek_ref, o_ref, m_ref, l_ref, acc_ref):
    step = pl.program_id(2)
    qi = qi_ref[step]
    ki = ki_ref[step]
    blk = ATT_BLOCK
    dh = FOX_HEAD_DIM
    qc = FOX_QUERY_CHUNK

    @pl.when(ki == 0)
    def _():
        m_ref[...] = jnp.full_like(m_ref, -jnp.inf)
        l_ref[...] = jnp.zeros_like(l_ref)
        acc_ref[...] = jnp.zeros_like(acc_ref)

    def update(masked):
        chains = [(hh, c) for hh in range(FOX_HEADS_PER_STEP) for c in range(blk // qc)]
        scores = []
        for hh, c in chains:
            cols = slice(hh * dh, (hh + 1) * dh)
            qs = slice(c * qc, (c + 1) * qc)
            k = jnp.concatenate([k_ref[0, :, cols], ek_ref[0, hh]], axis=1)
            q = jnp.concatenate([q_ref[0, qs, cols], eq_ref[0, hh, qs, :]], axis=1)
            s = lax.dot_general(k, q, (((1,), (1,)), ((), ())), preferred_element_type=F32)
            if masked:
                key = lax.broadcasted_iota(jnp.int32, (blk, qc), 0)
                qry = lax.broadcasted_iota(jnp.int32, (blk, qc), 1) + c * qc
                s = jnp.where(key <= qry, s, -jnp.inf)
            scores.append(s)
        probs = []
        for (hh, c), s in zip(chains, scores):
            qs = slice(c * qc, (c + 1) * qc)
            m_old = m_ref[hh, :, qs]
            m_new = jnp.maximum(m_old, jnp.max(s, axis=0, keepdims=True))
            alpha = jnp.exp2(m_old - m_new)
            p = jnp.exp2(s - m_new)
            l_ref[hh, :, qs] = alpha * l_ref[hh, :, qs] + jnp.sum(p, axis=0, keepdims=True)
            m_ref[hh, :, qs] = m_new
            probs.append((alpha, p.astype(BF16)))
        v_t = [jnp.transpose(v_ref[0, :, hh * dh:(hh + 1) * dh]) for hh in range(FOX_HEADS_PER_STEP)]
        for (hh, c), (alpha, p) in zip(chains, probs):
            qs = slice(c * qc, (c + 1) * qc)
            acc_ref[hh, :, qs] = alpha * acc_ref[hh, :, qs] + jnp.dot(v_t[hh], p,
                                                                      preferred_element_type=F32)

    @pl.when(ki < qi)
    def _():
        update(False)

    @pl.when(ki == qi)
    def _():
        update(True)
        for hh in range(FOX_HEADS_PER_STEP):
            o_ref[0, :, hh * dh:(hh + 1) * dh] = jnp.transpose(acc_ref[hh] / l_ref[hh]).astype(o_ref.dtype)


def fox_attention(z_a, eq, ek):
    b, s, _ = z_a.shape
    nb = s // ATT_BLOCK
    hp = FOX_HEADS_PER_STEP
    groups = FOX_HEADS // hp
    pairs = [(qi, ki) for qi in range(nb) for ki in range(qi + 1)]
    qi_tab = jnp.asarray([pr[0] for pr in pairs], jnp.int32)
    ki_tab = jnp.asarray([pr[1] for pr in pairs], jnp.int32)
    width = hp * FOX_HEAD_DIM
    grid_spec = pltpu.PrefetchScalarGridSpec(
        num_scalar_prefetch=2,
        grid=(b, groups, len(pairs)),
        in_specs=[pl.BlockSpec((1, ATT_BLOCK, width), lambda bi, g, st, qt, kt: (bi, qt[st], g)),
                  pl.BlockSpec((1, ATT_BLOCK, width), lambda bi, g, st, qt, kt: (bi, kt[st], groups + g)),
                  pl.BlockSpec((1, ATT_BLOCK, width), lambda bi, g, st, qt, kt: (bi, kt[st], 2 * groups + g)),
                  pl.BlockSpec((1, hp, ATT_BLOCK, LANES), lambda bi, g, st, qt, kt: (bi, g, qt[st], 0)),
                  pl.BlockSpec((1, hp, ATT_BLOCK, LANES), lambda bi, g, st, qt, kt: (bi, g, kt[st], 0))],
        out_specs=pl.BlockSpec((1, ATT_BLOCK, width), lambda bi, g, st, qt, kt: (bi, qt[st], g)),
        scratch_shapes=[pltpu.VMEM((hp, 1, ATT_BLOCK), F32), pltpu.VMEM((hp, 1, ATT_BLOCK), F32),
                        pltpu.VMEM((hp, FOX_HEAD_DIM, ATT_BLOCK), F32)],
    )
    return pl.pallas_call(
        _fox_kernel,
        grid_spec=grid_spec,
        out_shape=jax.ShapeDtypeStruct((b, s, BRANCH_WIDTH), BF16),
        compiler_params=_params(("parallel", "parallel", "arbitrary")),
        name="fox_attention",
    )(qi_tab, ki_tab, z_a, z_a, z_a, eq, ek)


def _pool_kernel(u_ref, halo_ref, w_ref, scale_ref, o_ref):
    i = pl.program_id(1)
    rows = POOL_ROWS
    t = i * rows + lax.broadcasted_iota(jnp.int32, (rows, 1), 0)
    keep_halo = (i > 0).astype(F32)
    for g, win in enumerate(POOL_WINDOWS):
        cols = slice(g * POOL_GROUP, (g + 1) * POOL_GROUP)
        u = u_ref[0, :, cols]
        ext = jnp.concatenate([halo_ref[0, :, cols] * keep_halo, u], axis=0)
        span = 1
        while span < win:
            n = ext.shape[0]
            ext = ext[:n - span] + ext[span:]
            span *= 2
        window_sum = ext[POOL_HALO + 1 - win:POOL_HALO + 1 - win + rows]
        count = jnp.minimum(t + 1, win).astype(F32)
        d = (window_sum / count - u).astype(BF16)
        y = jnp.dot(d, w_ref[0, g], preferred_element_type=F32) * scale_ref[:, cols]
        o_ref[0, :, cols] = y.astype(o_ref.dtype)


def multiscale_pool(z_b, pool_w, layer, pool_scale):
    b, s, _ = z_b.shape
    per = POOL_ROWS // POOL_HALO
    return pl.pallas_call(
        _pool_kernel,
        grid=(b, s // POOL_ROWS),
        in_specs=[pl.BlockSpec((1, POOL_ROWS, BRANCH_WIDTH), lambda bi, i: (bi, i, 0)),
                  pl.BlockSpec((1, POOL_HALO, BRANCH_WIDTH), lambda bi, i: (bi, jnp.maximum(i * per - 1, 0), 0)),
                  pl.BlockSpec((1, len(POOL_WINDOWS), POOL_GROUP, POOL_GROUP), lambda bi, i: (layer, 0, 0, 0)),
                  pl.BlockSpec((1, BRANCH_WIDTH), lambda bi, i: (0, 0))],
        out_specs=pl.BlockSpec((1, POOL_ROWS, BRANCH_WIDTH), lambda bi, i: (bi, i, 0)),
        out_shape=jax.ShapeDtypeStruct((b, s, BRANCH_WIDTH), BF16),
        compiler_params=_params(("parallel", "parallel")),
        name="multiscale_pool",
    )(z_b, z_b, pool_w, pool_scale.reshape(1, BRANCH_WIDTH))


def _sgu_kernel(u_ref, v_ref, nw_ref, w_ref, b_ref, o_ref):
    c = SGU_CHUNK
    v = jax.nn.gelu(v_ref[...].astype(F32))
    vc = v - jnp.mean(v, axis=-1, keepdims=True)
    vn = (vc * lax.rsqrt(jnp.mean(vc * vc, axis=-1, keepdims=True) + EPS) * nw_ref[...]).astype(BF16)
    causal = lax.broadcasted_iota(jnp.int32, (c, c), 0) >= lax.broadcasted_iota(jnp.int32, (c, c), 1)
    for g in range(SGU_GROUPS):
        cols = slice(g * SGU_GROUP_WIDTH, (g + 1) * SGU_GROUP_WIDTH)
        w = jnp.where(causal, w_ref[g], 0.0).astype(BF16)
        bias = b_ref[g]
        for n in range(SGU_ROWS // c):
            rows = slice(n * c, (n + 1) * c)
            sv = jnp.dot(w, vn[rows, cols], preferred_element_type=F32) + bias
            u = jax.nn.gelu(u_ref[rows, cols].astype(F32))
            o_ref[rows, cols] = (u * sv).astype(o_ref.dtype)


def spatial_gating(z_a, u_col0, norm_w, w_s, b_s):
    t = z_a.shape[0]
    ub = u_col0 // BRANCH_WIDTH
    return pl.pallas_call(
        _sgu_kernel,
        grid=(t // SGU_ROWS,),
        in_specs=[pl.BlockSpec((SGU_ROWS, BRANCH_WIDTH), lambda i: (i, ub)),
                  pl.BlockSpec((SGU_ROWS, BRANCH_WIDTH), lambda i: (i, ub + 1)),
                  pl.BlockSpec((1, BRANCH_WIDTH), lambda i: (0, 0)),
                  pl.BlockSpec((SGU_GROUPS, SGU_CHUNK, SGU_CHUNK), lambda i: (0, 0, 0)),
                  pl.BlockSpec((SGU_GROUPS, SGU_CHUNK, 1), lambda i: (0, 0, 0))],
        out_specs=pl.BlockSpec((SGU_ROWS, BRANCH_WIDTH), lambda i: (i, 0)),
        out_shape=jax.ShapeDtypeStruct((t, BRANCH_WIDTH), BF16),
        compiler_params=_params(("parallel",)),
        name="spatial_gating",
    )(z_a, z_a, norm_w.reshape(1, BRANCH_WIDTH), w_s, b_s.reshape(SGU_GROUPS, SGU_CHUNK, 1))


def _hgrn_kernel(q_ref, f_ref, v_ref, g_ref, lb_ref, nw_ref, o_ref, state_ref):
    c = HGRN_SUBCHUNK
    half = c // 2
    rows_total = HGRN_ROWS
    n = rows_total // c

    @pl.when(pl.program_id(2) == 0)
    def _():
        state_ref[...] = jnp.zeros_like(state_ref)

    local = lax.broadcasted_iota(jnp.int32, (rows_total, 1), 0) & (c - 1)
    t_half = lax.broadcasted_iota(jnp.int32, (1, half, 1), 1)
    for hh in range(HGRN_HEADS_PER_STEP):
        cols = slice(hh * LANES, (hh + 1) * LANES)
        lb = lb_ref[hh]
        q = q_ref[0, :, cols]
        fl = f_ref[0, :, cols]
        v = v_ref[0, :, cols]
        e = jnp.exp(-jnp.abs(fl))
        r = 1.0 / (1.0 + e)
        sig_pos = jnp.where(fl >= 0, r, e * r)
        sig_neg = jnp.where(fl >= 0, e * r, r)
        kk = (1.0 - lb) * sig_neg
        b = jnp.log2(lb + (1.0 - lb) * sig_pos)
        shift = 1
        while shift < c:
            b = b + jnp.where(local >= shift, pltpu.roll(b, shift, 0), 0.0)
            shift *= 2
        b3 = b.reshape(n, c, LANES)
        kk3 = kk.reshape(n, c, LANES)
        q3 = q.reshape(n, c, LANES)
        v3 = v.reshape(n, c, LANES)
        tot3 = b3[:, c - 1:c, :]
        q_dec = (q * jnp.exp2(b)).astype(BF16)
        k_dec = (kk3 * jnp.exp2(tot3 - b3)).reshape(rows_total, LANES).astype(BF16)
        decay = jnp.exp2(tot3)
        v_bf = v.astype(BF16)

        updates = [lax.dot_general(v_bf[i * c:(i + 1) * c], k_dec[i * c:(i + 1) * c], (((0,), (0,)), ((), ())),
                                   preferred_element_type=F32) for i in range(n)]
        state_t = state_ref[hh]
        states = []
        for i in range(n):
            states.append(state_t.astype(BF16))
            state_t = state_t * decay[i] + updates[i]
        state_ref[hh] = state_t
        outs = [lax.dot_general(q_dec[i * c:(i + 1) * c], states[i], (((1,), (1,)), ((), ())),
                                preferred_element_type=F32) for i in range(n)]

        b_lo, b_hi = b3[:, :half, :], b3[:, half:, :]
        q_lo, q_hi = q3[:, :half, :], q3[:, half:, :]
        o_lo = jnp.zeros((n, half, LANES), F32)
        o_hi = jnp.zeros((n, half, LANES), F32)
        for s in range(c):
            ks = kk3[:, s:s + 1, :]
            bs = b3[:, s:s + 1, :]
            vs = v3[:, s:s + 1, :]
            if s < half:
                d = jnp.where(t_half >= s, b_lo - bs, -jnp.inf)
                a = jnp.sum(q_lo * ks * jnp.exp2(d), axis=-1, keepdims=True)
                o_lo = o_lo + a * vs
                d = b_hi - bs
            else:
                d = jnp.where(t_half >= s - half, b_hi - bs, -jnp.inf)
            a = jnp.sum(q_hi * ks * jnp.exp2(d), axis=-1, keepdims=True)
            o_hi = o_hi + a * vs
        o = jnp.concatenate([o_lo, o_hi], axis=1).reshape(rows_total, LANES) + jnp.concatenate(outs, axis=0)

        g = g_ref[0, :, cols]
        o = o * lax.rsqrt(jnp.mean(o * o, axis=-1, keepdims=True) + EPS) * nw_ref[hh]
        o_ref[0, :, cols] = (o * (g * _sigmoid(g))).astype(o_ref.dtype)


def hgrn2(z_b, col0, lower_bound, norm_w):
    b, s, _ = z_b.shape
    hp = HGRN_HEADS_PER_STEP
    width = hp * LANES
    c0 = col0 // width
    groups = HGRN_HEADS // hp

    def spec(part):
        return pl.BlockSpec((1, HGRN_ROWS, width), lambda bi, h, i: (bi, i, c0 + part * groups + h))

    vec = pl.BlockSpec((hp, 1, LANES), lambda bi, h, i: (h, 0, 0))
    return pl.pallas_call(
        _hgrn_kernel,
        grid=(b, groups, s // HGRN_ROWS),
        in_specs=[spec(0), spec(1), spec(2), spec(3), vec, vec],
        out_specs=pl.BlockSpec((1, HGRN_ROWS, width), lambda bi, h, i: (bi, i, h)),
        out_shape=jax.ShapeDtypeStruct((b, s, BRANCH_WIDTH), BF16),
        scratch_shapes=[pltpu.VMEM((hp, HGRN_VAL_DIM, HGRN_KEY_DIM), F32)],
        compiler_params=_params(("parallel", "parallel", "arbitrary")),
        name="hgrn2",
    )(z_b, z_b, z_b, z_b, lower_bound.reshape(HGRN_HEADS, 1, LANES), norm_w.reshape(HGRN_HEADS, 1, LANES))


def _bf16_split2(x):
    hi = x.astype(BF16)
    return hi, (x - hi.astype(F32)).astype(BF16)


def _router_kernel(x_ref, w_ref, wr_ref, br_ref, n_ref, meta_ref, p_ref, cnt_ref, carry_ref):
    @pl.when(pl.program_id(0) == 0)
    def _():
        carry_ref[...] = jnp.zeros_like(carry_ref)

    n = _rms(x_ref[...], w_ref[...])
    n_ref[...] = n
    n_hi, n_lo = _bf16_split2(n)
    w_hi, w_lo = _bf16_split2(wr_ref[...])
    logits = (jnp.dot(n_hi, w_hi, preferred_element_type=F32) + jnp.dot(n_lo, w_hi, preferred_element_type=F32)
              + jnp.dot(n_hi, w_lo, preferred_element_type=F32)) + br_ref[...]
    rows = logits.shape[0]
    col = lax.broadcasted_iota(jnp.int32, logits.shape, 1)
    big = jnp.int32(LANES)
    neg = -jnp.inf

    def first_argmax(vals):
        top = jnp.max(vals, axis=-1, keepdims=True)
        return top, jnp.min(jnp.where(vals == top, col, big), axis=-1, keepdims=True)

    group_logits = jnp.where(col < N_GROUPS, logits, neg)
    g_top, g_idx = first_argmax(group_logits)
    p_group = 1.0 / jnp.sum(jnp.exp(group_logits - g_top), axis=-1, keepdims=True)
    lo = N_GROUPS + g_idx * EXPERTS_PER_GROUP
    in_group = jnp.where((col >= lo) & (col < lo + EXPERTS_PER_GROUP), logits, neg)
    top1, idx1 = first_argmax(in_group)
    top2, idx2 = first_argmax(jnp.where(col == idx1, neg, in_group))
    r = jnp.exp(top2 - top1)
    p1 = p_group / (1.0 + r)
    p2 = p_group * r / (1.0 + r)
    p_ref[...] = jnp.where(col == 0, p1, jnp.where(col == 1, p2, 0.0))

    chosen = ((col == idx1) | (col == idx2)).astype(BF16)
    earlier = (lax.broadcasted_iota(jnp.int32, (rows, rows), 0)
               > lax.broadcasted_iota(jnp.int32, (rows, rows), 1)).astype(BF16)
    before = jnp.dot(earlier, chosen, preferred_element_type=F32) + carry_ref[...]
    rank1 = jnp.sum(jnp.where(col == idx1, before, 0.0), axis=-1, keepdims=True).astype(jnp.int32)
    rank2 = jnp.sum(jnp.where(col == idx2, before, 0.0), axis=-1, keepdims=True).astype(jnp.int32)
    carry_ref[...] += jnp.sum(chosen.astype(F32), axis=0, keepdims=True)
    cnt_ref[...] = carry_ref[...]
    meta_ref[...] = jnp.where(col == 0, idx1 - N_GROUPS, jnp.where(col == 1, idx2 - N_GROUPS,
                              jnp.where(col == 2, rank1, jnp.where(col == 3, rank2, 0))))


def moe_router(h, norm_w, w_router, b_router):
    t, d = h.shape
    row = pl.BlockSpec((NORM_ROWS, d), lambda i: (i, 0))
    small = pl.BlockSpec((NORM_ROWS, LANES), lambda i: (i, 0))
    one = pl.BlockSpec((1, LANES), lambda i: (0, 0))
    return pl.pallas_call(
        _router_kernel,
        grid=(t // NORM_ROWS,),
        in_specs=[row, pl.BlockSpec((1, d), lambda i: (0, 0)), pl.BlockSpec((d, LANES), lambda i: (0, 0)), one],
        out_specs=[row, small, small, one],
        out_shape=[jax.ShapeDtypeStruct((t, d), F32), jax.ShapeDtypeStruct((t, LANES), jnp.int32),
                   jax.ShapeDtypeStruct((t, LANES), F32), jax.ShapeDtypeStruct((1, LANES), F32)],
        scratch_shapes=[pltpu.VMEM((1, LANES), F32)],
        compiler_params=_params(("arbitrary",)),
        name="moe_router",
    )(h, norm_w.reshape(1, d), w_router, b_router)


def _gather_rows(idx_ref, src_hbm, dst_ref, sem, unrolled):
    n_rows = dst_ref.shape[0]

    def start(r):
        pltpu.make_async_copy(src_hbm.at[pl.ds(idx_ref[r], 1)], dst_ref.at[pl.ds(r, 1)], sem).start()

    if unrolled:
        for r in range(n_rows):
            start(r)
    else:
        def body(r, carry):
            start(r)
            return carry
        lax.fori_loop(0, n_rows, body, 0)


def _wait_rows(src_hbm, dst_ref, sem):
    pltpu.make_async_copy(src_hbm.at[pl.ds(0, dst_ref.shape[0])], dst_ref, sem).wait()


def _expert_kernel(be_ref, nused_ref, tok_cur_ref, tok_next_ref, x_hbm, wg_ref, wu_ref, wd_ref, y_ref,
                   xbuf_ref, sem_ref):
    i = pl.program_id(0)
    last = pl.num_programs(0) - 1
    n_used = nused_ref[0]
    slot = i % 2

    @pl.when((i == 0) & (n_used > 0))
    def _():
        _gather_rows(tok_cur_ref, x_hbm, xbuf_ref.at[0], sem_ref.at[0], unrolled=False)

    @pl.when(i < n_used)
    def _():
        _gather_rows(tok_next_ref, x_hbm, xbuf_ref.at[1 - slot], sem_ref.at[1 - slot], unrolled=True)
        _wait_rows(x_hbm, xbuf_ref.at[slot], sem_ref.at[slot])
        x = xbuf_ref[slot].astype(BF16)
        gate = jnp.dot(x, wg_ref[0], preferred_element_type=F32)
        up = jnp.dot(x, wu_ref[0], preferred_element_type=F32)
        hidden = (gate * _sigmoid(gate) * up).astype(BF16)
        y_ref[...] = jnp.dot(hidden, wd_ref[0], preferred_element_type=F32)

    @pl.when((i == n_used) & (n_used > 0))
    def _():
        _wait_rows(x_hbm, xbuf_ref.at[slot], sem_ref.at[slot])

    @pl.when((i == last) & (n_used > last))
    def _():
        _wait_rows(x_hbm, xbuf_ref.at[1 - slot], sem_ref.at[1 - slot])

    @pl.when(i >= n_used)
    def _():
        y_ref[...] = jnp.zeros_like(y_ref)


def expert_mlp(x, buf_tok, block_e, n_used, w_gate, w_up, w_down):
    p = buf_tok.shape[0]
    nb = p // MOE_ROWS
    d = x.shape[1]
    smem = pltpu.SMEM
    grid_spec = pltpu.PrefetchScalarGridSpec(
        num_scalar_prefetch=2,
        grid=(nb,),
        in_specs=[pl.BlockSpec((MOE_ROWS,), lambda i, be, nu: (i,), memory_space=smem),
                  pl.BlockSpec((MOE_ROWS,), lambda i, be, nu: (jnp.minimum(i + 1, nb - 1),), memory_space=smem),
                  pl.BlockSpec(memory_space=pl.ANY),
                  pl.BlockSpec((1, d, EXPERT_HIDDEN), lambda i, be, nu: (be[i], 0, 0)),
                  pl.BlockSpec((1, d, EXPERT_HIDDEN), lambda i, be, nu: (be[i], 0, 0)),
                  pl.BlockSpec((1, EXPERT_HIDDEN, d), lambda i, be, nu: (be[i], 0, 0))],
        out_specs=pl.BlockSpec((MOE_ROWS, d), lambda i, be, nu: (i, 0)),
        scratch_shapes=[pltpu.VMEM((2, MOE_ROWS, d), F32), pltpu.SemaphoreType.DMA((2,))],
    )
    return pl.pallas_call(
        _expert_kernel,
        grid_spec=grid_spec,
        out_shape=jax.ShapeDtypeStruct((p, d), F32),
        compiler_params=_params(("arbitrary",)),
        name="expert_mlp",
    )(block_e, n_used, buf_tok, buf_tok, x, w_gate, w_up, w_down)


def _combine_kernel(pos0_ref, pos1_ref, pos0_next_ref, pos1_next_ref, y_hbm, h_ref, p_ref, w_ref, o_ref, n_ref,
                    ybuf_ref, sem_ref):
    i = pl.program_id(0)
    last = pl.num_programs(0) - 1
    slot = i % 2

    @pl.when(i == 0)
    def _():
        _gather_rows(pos0_ref, y_hbm, ybuf_ref.at[0, 0], sem_ref.at[0, 0], unrolled=False)
        _gather_rows(pos1_ref, y_hbm, ybuf_ref.at[0, 1], sem_ref.at[0, 1], unrolled=False)

    _gather_rows(pos0_next_ref, y_hbm, ybuf_ref.at[1 - slot, 0], sem_ref.at[1 - slot, 0], unrolled=True)
    _gather_rows(pos1_next_ref, y_hbm, ybuf_ref.at[1 - slot, 1], sem_ref.at[1 - slot, 1], unrolled=True)
    _wait_rows(y_hbm, ybuf_ref.at[slot, 0], sem_ref.at[slot, 0])
    _wait_rows(y_hbm, ybuf_ref.at[slot, 1], sem_ref.at[slot, 1])
    p = p_ref[...]
    h = h_ref[...] + (ybuf_ref[slot, 0] * p[:, 0:1] + ybuf_ref[slot, 1] * p[:, 1:2])
    o_ref[...] = h
    n_ref[...] = _rms(h, w_ref[...]).astype(n_ref.dtype)

    @pl.when(i == last)
    def _():
        _wait_rows(y_hbm, ybuf_ref.at[1 - slot, 0], sem_ref.at[1 - slot, 0])
        _wait_rows(y_hbm, ybuf_ref.at[1 - slot, 1], sem_ref.at[1 - slot, 1])


def moe_combine(y, pos0, pos1, probs, h, norm_w):
    t, d = h.shape
    nb = t // COMBINE_ROWS
    smem = pltpu.SMEM
    row = pl.BlockSpec((COMBINE_ROWS, d), lambda i: (i, 0))
    cur = pl.BlockSpec((COMBINE_ROWS,), lambda i: (i,), memory_space=smem)
    nxt = pl.BlockSpec((COMBINE_ROWS,), lambda i: (jnp.minimum(i + 1, nb - 1),), memory_space=smem)
    return pl.pallas_call(
        _combine_kernel,
        grid=(nb,),
        in_specs=[cur, cur, nxt, nxt,
                  pl.BlockSpec(memory_space=pl.ANY),
                  row,
                  pl.BlockSpec((COMBINE_ROWS, LANES), lambda i: (i, 0)),
                  pl.BlockSpec((1, d), lambda i: (0, 0))],
        out_specs=[row, row],
        out_shape=[jax.ShapeDtypeStruct((t, d), F32), jax.ShapeDtypeStruct((t, d), BF16)],
        scratch_shapes=[pltpu.VMEM((2, TOP_K, COMBINE_ROWS, d), F32), pltpu.SemaphoreType.DMA((2, TOP_K))],
        compiler_params=_params(("arbitrary",)),
        name="moe_combine",
    )(pos0, pos1, pos0, pos1, y, h, probs, norm_w.reshape(1, d))


def _dispatch_plan(meta, counts):
    t = meta.shape[0]
    a = t * TOP_K
    expert_ids = meta[:, :TOP_K]
    ranks = meta[:, TOP_K:2 * TOP_K]
    counts = counts[0, N_GROUPS:N_GROUPS + N_EXPERTS].astype(jnp.int32)
    padded = (counts + MOE_ROWS - 1) // MOE_ROWS * MOE_ROWS
    pad_end = jnp.cumsum(padded)
    pad_start = pad_end - padded
    dest = pad_start[expert_ids] + ranks
    p = a + N_EXPERTS * MOE_ROWS
    nb = p // MOE_ROWS
    token_ids = jnp.repeat(jnp.arange(t, dtype=jnp.int32), TOP_K)
    buf_tok = jnp.zeros((p,), jnp.int32).at[dest.reshape(a)].set(token_ids, unique_indices=True)
    block_start = jnp.arange(nb, dtype=jnp.int32) * MOE_ROWS
    block_e = jnp.minimum(jnp.sum((pad_end[None, :] <= block_start[:, None]).astype(jnp.int32), axis=1),
                          N_EXPERTS - 1)
    n_used = (pad_end[-1:] // MOE_ROWS).astype(jnp.int32)
    return buf_tok, block_e, n_used, dest[:, 0], dest[:, 1]


def _pad_cols(w, n):
    return jnp.pad(w, ((0, 0), (0, n - w.shape[1])))


def _router_weights(i, router_group_w, router_group_b, router_expert_w, router_expert_b):
    w_r = _pad_cols(jnp.concatenate([router_group_w[i], router_expert_w[i]], axis=1), LANES)
    b_r = _pad_cols(jnp.concatenate([router_group_b[i], router_expert_b[i]])[None, :], LANES)
    return w_r, b_r


def kernel(x, p, mix_norm_w, w_in, fox_f_bias, pool_w, pool_scale, sgu_norm_w, sgu_w, sgu_b,
           hgrn_lb_logits, hgrn_norm_w, branch_proj, w_out, ffn_norm_w, router_group_w,
           router_group_b, router_expert_w, router_expert_b, expert_w_gate, expert_w_up,
           expert_w_down, ple_norm_w, ple_gate_w, ple_proj_w, final_norm_w):
    bsz, seq, d = x.shape
    t = bsz * seq
    lb_p = jax.nn.softmax(hgrn_lb_logits.astype(F32), axis=0)
    lower_bounds = jnp.cumsum(lb_p, axis=0) - lb_p[0:1]
    sgu_u_col = FOX_QKV_END
    gate_col = FOX_QKV_END + 2 * BRANCH_WIDTH
    h = x.reshape(t, d)
    w_in_t = jnp.transpose(w_in, (0, 2, 1))

    def tiles(*ranges):
        return [c for lo, hi in ranges for c in range(lo, hi, MM_COLS)]

    starts_a = tiles((0, FOX_QKV_END), (POOL_END, SGU_END), (HGRN_END, IN_COLS))
    starts_b = tiles((FOX_F_END, POOL_END), (SGU_END, HGRN_END))
    n_a = len(starts_a) * MM_COLS
    n_b = len(starts_b) * MM_COLS
    q_scale = jnp.where(jnp.arange(n_a) < BRANCH_WIDTH, FOX_HEAD_DIM ** -0.5 * LOG2E, 1.0).astype(F32)[None, :]
    pool_w_bf, branch_proj_bf, w_out_bf, ple_gate_bf, ple_proj_bf = (
        w.astype(BF16) for w in (pool_w, branch_proj, w_out, ple_gate_w, ple_proj_w))
    p_bf = p.reshape(DEPTH, t, PLE_DIM).astype(BF16)
    for i in range(DEPTH):
        w_r, b_r = _router_weights(i, router_group_w, router_group_b, router_expert_w, router_expert_b)
        n, z_f = rms_norm_proj(h, mix_norm_w[i], w_in_t, i, FOX_QKV_END, BF16)
        z_a, (w_gate_bf, w_up_bf) = in_projection(n, w_in_t, i, starts_a, q_scale, BF16,
                                                  cast=((expert_w_gate, CAST_ROWS), (expert_w_up, CAST_ROWS)))
        z_b, (w_down_bf,) = in_projection(n, w_in_t, i, starts_b, jnp.ones((1, n_b), F32), F32,
                                          cast=((expert_w_down, CAST_ROWS),))
        z_a3 = z_a.reshape(bsz, seq, -1)
        z_b3 = z_b.reshape(bsz, seq, -1)

        f_bias = _pad_cols(fox_f_bias[i][None, :].astype(F32), LANES)
        eq, ek = forget_prep(z_f.reshape(bsz, seq, LANES), f_bias)
        br_fox = fox_attention(z_a3, eq, ek).reshape(t, BRANCH_WIDTH)
        br_pool = multiscale_pool(z_b3, pool_w_bf, i, pool_scale[i]).reshape(t, BRANCH_WIDTH)
        br_sgu = spatial_gating(z_a, sgu_u_col, sgu_norm_w[i], sgu_w[i], sgu_b[i])
        br_hgrn = hgrn2(z_b3, BRANCH_WIDTH, lower_bounds[i], hgrn_norm_w[i]).reshape(t, BRANCH_WIDTH)

        merged = merge_branches((br_fox, br_pool, br_sgu, br_hgrn), branch_proj_bf, i, z_a, gate_col)
        h = matmul_residual(merged, w_out_bf, i, h)

        xn, meta, probs, counts = moe_router(h, ffn_norm_w[i], w_r, b_r)
        buf_tok, block_e, n_used, pos0, pos1 = _dispatch_plan(meta, counts)
        y = expert_mlp(xn, buf_tok, block_e, n_used, w_gate_bf, w_up_bf, w_down_bf)
        h, n_ple = moe_combine(y, pos0, pos1, probs, h, ple_norm_w[i])

        h = ple_update(n_ple, ple_gate_bf, p_bf, ple_proj_bf, i, h)
    return rms_norm(h, final_norm_w, F32).reshape(bsz, seq, d)
```

```python
import functools

import jax
import jax.numpy as jnp
from jax import lax
from jax.experimental import pallas as pl
from jax.experimental.pallas import tpu as pltpu

F32 = jnp.float32
BF16 = jnp.bfloat16
HIGHEST = lax.Precision.HIGHEST

D_MODEL = 2048
DEPTH = 2
BRANCH_WIDTH = D_MODEL // 2
N_BRANCHES = 4
PLE_DIM = 256
EPS = 1e-6
FOX_HEAD_DIM = 128
FOX_HEADS = BRANCH_WIDTH // FOX_HEAD_DIM
POOL_WINDOWS = (2, 4, 8, 16)
POOL_GROUP = BRANCH_WIDTH // len(POOL_WINDOWS)
POOL_HALO = 16
SGU_GROUPS = 4
SGU_CHUNK = 128
SGU_GROUP_WIDTH = BRANCH_WIDTH // SGU_GROUPS
HGRN_KEY_DIM = 128
HGRN_VAL_DIM = 128
HGRN_HEADS = BRANCH_WIDTH // HGRN_VAL_DIM
HGRN_SUBCHUNK = 16
HGRN_HEADS_PER_STEP = 2
N_GROUPS = 4
EXPERTS_PER_GROUP = 8
N_EXPERTS = N_GROUPS * EXPERTS_PER_GROUP
TOP_K = 2
EXPERT_HIDDEN = D_MODEL // 2
FOX_QKV_END = 3 * BRANCH_WIDTH
FOX_F_END = FOX_QKV_END + FOX_HEADS
POOL_END = FOX_F_END + BRANCH_WIDTH
SGU_END = POOL_END + 2 * BRANCH_WIDTH
HGRN_END = SGU_END + 4 * BRANCH_WIDTH
IN_COLS = HGRN_END + N_BRANCHES * D_MODEL

LANES = 128
SUBLANES = 8
LOG2E = 1.4426950408889634
FOX_HEADS_PER_STEP = 8
FOX_QUERY_CHUNK = 256
VMEM_LIMIT = 56 * 1024 * 1024

NORM_ROWS = 512
MM_ROWS = 2048
MM_COLS = 512
CAST_ROWS = 512
MERGE_ROWS = 1024
ATT_BLOCK = 512
CUMSUM_BLOCK = 512
POOL_ROWS = 512
SGU_ROWS = 512
HGRN_ROWS = 512
MOE_ROWS = 256
COMBINE_ROWS = 256
PLE_ROWS = 2048


def _params(semantics, **kw):
    return pltpu.CompilerParams(dimension_semantics=semantics, vmem_limit_bytes=VMEM_LIMIT, **kw)


def _sigmoid(x):
    return 1.0 / (1.0 + jnp.exp(-x))


def _log_sigmoid(x):
    return jnp.minimum(x, 0.0) - jnp.log(1.0 + jnp.exp(-jnp.abs(x)))


def _rms(x, w):
    return x * lax.rsqrt(jnp.mean(x * x, axis=-1, keepdims=True) + EPS) * w


def _norm_kernel(x_ref, w_ref, n_ref):
    n_ref[...] = _rms(x_ref[...], w_ref[...]).astype(n_ref.dtype)


def rms_norm(x, w, out_dtype):
    t, d = x.shape
    return pl.pallas_call(
        _norm_kernel,
        grid=(t // NORM_ROWS,),
        in_specs=[pl.BlockSpec((NORM_ROWS, d), lambda i: (i, 0)),
                  pl.BlockSpec((1, d), lambda i: (0, 0))],
        out_specs=pl.BlockSpec((NORM_ROWS, d), lambda i: (i, 0)),
        out_shape=jax.ShapeDtypeStruct((t, d), out_dtype),
        compiler_params=_params(("parallel",)),
        name="rms_norm",
    )(x, w.reshape(1, d))


def _norm_proj_kernel(x_ref, w_ref, wpt_ref, n_ref, p_ref):
    n = _rms(x_ref[...], w_ref[...]).astype(BF16)
    n_ref[...] = n.astype(n_ref.dtype)
    p_ref[...] = lax.dot_general(n, wpt_ref[0].astype(BF16), (((1,), (1,)), ((), ())), preferred_element_type=F32)


def rms_norm_proj(x, w, wt, layer, row0, out_dtype):
    t, d = x.shape
    jp = row0 // LANES
    return pl.pallas_call(
        _norm_proj_kernel,
        grid=(t // NORM_ROWS,),
        in_specs=[pl.BlockSpec((NORM_ROWS, d), lambda i: (i, 0)),
                  pl.BlockSpec((1, d), lambda i: (0, 0)),
                  pl.BlockSpec((1, LANES, d), lambda i: (layer, jp, 0))],
        out_specs=[pl.BlockSpec((NORM_ROWS, d), lambda i: (i, 0)),
                   pl.BlockSpec((NORM_ROWS, LANES), lambda i: (i, 0))],
        out_shape=[jax.ShapeDtypeStruct((t, d), out_dtype),
                   jax.ShapeDtypeStruct((t, LANES), F32)],
        compiler_params=_params(("parallel",)),
        name="rms_norm_proj",
    )(x, w.reshape(1, d), wt)


def _in_proj_kernel(start_tiles_ref, a_ref, wt_ref, s_ref, *rest, n_cast):
    cast_in, o_ref, cast_out = rest[:n_cast], rest[n_cast], rest[n_cast + 1:]
    w = wt_ref[0].astype(BF16)
    acc = lax.dot_general(a_ref[...], w, (((1,), (1,)), ((), ())), preferred_element_type=F32)
    o_ref[...] = (acc * s_ref[...]).astype(o_ref.dtype)
    for src_ref, dst_ref in zip(cast_in, cast_out):
        dst_ref[...] = src_ref[0].astype(dst_ref.dtype)


def in_projection(a, wt, layer, starts, col_scale, out_dtype, cast=()):
    m, k = a.shape
    assert all(c % SUBLANES == 0 for c in starts)
    n = len(starts) * MM_COLS
    nj = len(starts)
    n_steps = (m // MM_ROWS) * nj
    cast_specs_in, cast_specs_out, cast_shapes = [], [], []
    for arr, rows in cast:
        _, e, r, c = arr.shape
        per = r // rows
        n_blocks = e * per
        assert n_blocks <= n_steps

        def block(i, j, st, per=per, n_blocks=n_blocks):
            kk = jnp.minimum(i * nj + j, n_blocks - 1)
            return kk // per, kk % per

        cast_specs_in.append(pl.BlockSpec((1, 1, rows, c), lambda i, j, st, block=block: (layer, *block(i, j, st), 0)))
        cast_specs_out.append(pl.BlockSpec((1, rows, c), lambda i, j, st, block=block: (*block(i, j, st), 0)))
        cast_shapes.append(jax.ShapeDtypeStruct((e, r, c), BF16))
    grid_spec = pltpu.PrefetchScalarGridSpec(
        num_scalar_prefetch=1,
        grid=(m // MM_ROWS, nj),
        in_specs=[pl.BlockSpec((MM_ROWS, k), lambda i, j, st: (i, 0)),
                  pl.BlockSpec((pl.Element(1), pl.Element(MM_COLS), pl.Element(k)),
                               lambda i, j, st: (layer, st[j] * SUBLANES, 0)),
                  pl.BlockSpec((1, MM_COLS), lambda i, j, st: (0, j))] + cast_specs_in,
        out_specs=[pl.BlockSpec((MM_ROWS, MM_COLS), lambda i, j, st: (i, j))] + cast_specs_out,
    )
    outs = pl.pallas_call(
        functools.partial(_in_proj_kernel, n_cast=len(cast)),
        grid_spec=grid_spec,
        out_shape=[jax.ShapeDtypeStruct((m, n), out_dtype)] + cast_shapes,
        compiler_params=_params(("arbitrary", "arbitrary")),
        name="in_projection",
    )(jnp.asarray([c // SUBLANES for c in starts], jnp.int32), a, wt, col_scale, *[arr for arr, _ in cast])
    return outs[0], outs[1:]


def _mm_res_kernel(a_ref, w_ref, r_ref, o_ref):
    o_ref[...] = r_ref[...] + jnp.dot(a_ref[...], w_ref[0], preferred_element_type=F32)


def matmul_residual(a, w, layer, res):
    m, k = a.shape
    n = w.shape[2]
    return pl.pallas_call(
        _mm_res_kernel,
        grid=(m // MM_ROWS, n // MM_COLS),
        in_specs=[pl.BlockSpec((MM_ROWS, k), lambda i, j: (i, 0)),
                  pl.BlockSpec((1, k, MM_COLS), lambda i, j: (layer, 0, j)),
                  pl.BlockSpec((MM_ROWS, MM_COLS), lambda i, j: (i, j))],
        out_specs=pl.BlockSpec((MM_ROWS, MM_COLS), lambda i, j: (i, j)),
        out_shape=jax.ShapeDtypeStruct((m, n), F32),
        compiler_params=_params(("parallel", "arbitrary")),
        name="matmul_residual",
    )(a, w, res)


def _merge_kernel(b0_ref, b1_ref, b2_ref, b3_ref, p_ref, g0_ref, g1_ref, g2_ref, g3_ref, o_ref):
    acc = None
    for bi, (b_ref, g_ref) in enumerate(((b0_ref, g0_ref), (b1_ref, g1_ref),
                                         (b2_ref, g2_ref), (b3_ref, g3_ref))):
        y = jnp.dot(b_ref[...], p_ref[0, bi], preferred_element_type=F32)
        y = _sigmoid(g_ref[...].astype(F32)) * y
        acc = y if acc is None else acc + y
    o_ref[...] = acc.astype(o_ref.dtype)


def merge_branches(branches, proj, layer, z_a, gate_col0):
    t = branches[0].shape[0]
    g0 = gate_col0 // MM_COLS
    per = D_MODEL // MM_COLS
    b_spec = pl.BlockSpec((MERGE_ROWS, BRANCH_WIDTH), lambda i, j: (i, 0))
    g_specs = [pl.BlockSpec((MERGE_ROWS, MM_COLS), functools.partial(lambda i, j, o: (i, o + j), o=g0 + bi * per))
               for bi in range(N_BRANCHES)]
    return pl.pallas_call(
        _merge_kernel,
        grid=(t // MERGE_ROWS, per),
        in_specs=[b_spec] * N_BRANCHES
        + [pl.BlockSpec((1, N_BRANCHES, BRANCH_WIDTH, MM_COLS), lambda i, j: (layer, 0, 0, j))] + g_specs,
        out_specs=pl.BlockSpec((MERGE_ROWS, MM_COLS), lambda i, j: (i, j)),
        out_shape=jax.ShapeDtypeStruct((t, D_MODEL), BF16),
        compiler_params=_params(("parallel", "arbitrary")),
        name="merge_branches",
    )(*branches, proj, z_a, z_a, z_a, z_a)


def _ple_kernel(n_ref, wg_ref, p_ref, wp_ref, h_ref, o_ref):
    gate = _sigmoid(jnp.dot(n_ref[...], wg_ref[0], preferred_element_type=F32))
    emb = jnp.dot(p_ref[0], wp_ref[0], preferred_element_type=F32)
    o_ref[...] = h_ref[...] + gate * emb


def ple_update(n, wg, p, wp, layer, h):
    t = h.shape[0]
    return pl.pallas_call(
        _ple_kernel,
        grid=(t // PLE_ROWS, D_MODEL // MM_COLS),
        in_specs=[pl.BlockSpec((PLE_ROWS, D_MODEL), lambda i, j: (i, 0)),
                  pl.BlockSpec((1, D_MODEL, MM_COLS), lambda i, j: (layer, 0, j)),
                  pl.BlockSpec((1, PLE_ROWS, PLE_DIM), lambda i, j: (layer, i, 0)),
                  pl.BlockSpec((1, PLE_DIM, MM_COLS), lambda i, j: (layer, 0, j)),
                  pl.BlockSpec((PLE_ROWS, MM_COLS), lambda i, j: (i, j))],
        out_specs=pl.BlockSpec((PLE_ROWS, MM_COLS), lambda i, j: (i, j)),
        out_shape=jax.ShapeDtypeStruct((t, D_MODEL), F32),
        compiler_params=_params(("parallel", "arbitrary")),
        name="ple_update",
    )(n, wg, p, wp, h)


def _bf16_split3(x):
    hi = x.astype(BF16).astype(F32)
    r = x - hi
    mid = r.astype(BF16).astype(F32)
    lo = (r - mid).astype(BF16).astype(F32)
    return hi, mid, lo


def _forget_prep_kernel(z_ref, b_ref, eq_ref, ek_ref, carry_ref):
    @pl.when(pl.program_id(1) == 0)
    def _():
        carry_ref[...] = jnp.zeros_like(carry_ref)

    lf = _log_sigmoid(z_ref[0] + b_ref[...]) * LOG2E
    n = lf.shape[0]
    tril = (lax.broadcasted_iota(jnp.int32, (n, n), 0) >= lax.broadcasted_iota(jnp.int32, (n, n), 1)).astype(F32)
    c = jnp.dot(tril, lf, precision=HIGHEST, preferred_element_type=F32) + carry_ref[...]
    carry_ref[...] = c[n - 1:n, :]
    parts = _bf16_split3(c)
    lane = lax.broadcasted_iota(jnp.int32, (n, LANES), 1)
    for h in range(FOX_HEADS):
        hi, mid, lo = (jnp.sum(jnp.where(lane == h, part, 0.0), axis=-1, keepdims=True) for part in parts)
        eq = jnp.where(lane == 0, hi, jnp.where(lane == 1, mid, jnp.where(lane == 2, lo,
                       jnp.where(lane < 6, 1.0, 0.0))))
        ek = jnp.where(lane < 3, 1.0, jnp.where(lane == 3, -hi, jnp.where(lane == 4, -mid,
                       jnp.where(lane == 5, -lo, 0.0))))
        eq_ref[0, h] = eq.astype(BF16)
        ek_ref[0, h] = ek.astype(BF16)


def forget_prep(z_f, bias):
    b, s, _ = z_f.shape
    out = pl.BlockSpec((1, FOX_HEADS, CUMSUM_BLOCK, LANES), lambda bi, i: (bi, 0, i, 0))
    shape = jax.ShapeDtypeStruct((b, FOX_HEADS, s, LANES), BF16)
    return pl.pallas_call(
        _forget_prep_kernel,
        grid=(b, s // CUMSUM_BLOCK),
        in_specs=[pl.BlockSpec((1, CUMSUM_BLOCK, LANES), lambda bi, i: (bi, i, 0)),
                  pl.BlockSpec((1, LANES), lambda bi, i: (0, 0))],
        out_specs=[out, out],
        out_shape=[shape, shape],
        scratch_shapes=[pltpu.VMEM((1, LANES), F32)],
        compiler_params=_params(("parallel", "arbitrary")),
        name="forget_prep",
    )(z_f, bias)


def _fox_kernel(qi_ref, ki_ref, q_ref, k_ref, v_ref, eq_ref, ek_ref, o_ref, m_ref, l_ref, acc_ref):
    step = pl.program_id(2)
    qi = qi_ref[step]
    ki = ki_ref[step]
    blk = ATT_BLOCK
    dh = FOX_HEAD_DIM
    qc = FOX_QUERY_CHUNK

    @pl.when(ki == 0)
    def _():
        m_ref[...] = jnp.full_like(m_ref, -jnp.inf)
        l_ref[...] = jnp.zeros_like(l_ref)
        acc_ref[...] = jnp.zeros_like(acc_ref)

    def update(masked):
        chains = [(hh, c) for hh in range(FOX_HEADS_PER_STEP) for c in range(blk // qc)]
        scores = []
        for hh, c in chains:
            cols = slice(hh * dh, (hh + 1) * dh)
            qs = slice(c * qc, (c + 1) * qc)
            k = jnp.concatenate([k_ref[0, :, cols], ek_ref[0, hh]], axis=1)
            q = jnp.concatenate([q_ref[0, qs, cols], eq_ref[0, hh, qs, :]], axis=1)
            s = lax.dot_general(k, q, (((1,), (1,)), ((), ())), preferred_element_type=F32)
            if masked:
                key = lax.broadcasted_iota(jnp.int32, (blk, qc), 0)
                qry = lax.broadcasted_iota(jnp.int32, (blk, qc), 1) + c * qc
                s = jnp.where(key <= qry, s, -jnp.inf)
            scores.append(s)
        probs = []
        for (hh, c), s in zip(chains, scores):
            qs = slice(c * qc, (c + 1) * qc)
            m_old = m_ref[hh, :, qs]
            m_new = jnp.maximum(m_old, jnp.max(s, axis=0, keepdims=True))
            alpha = jnp.exp2(m_old - m_new)
            p = jnp.exp2(s - m_new)
            l_ref[hh, :, qs] = alpha * l_ref[hh, :, qs] + jnp.sum(p, axis=0, keepdims=True)
            m_ref[hh, :, qs] = m_new
            probs.append((alpha, p.astype(BF16)))
        v_t = [jnp.transpose(v_ref[0, :, hh * dh:(hh + 1) * dh]) for hh in range(FOX_HEADS_PER_STEP)]
        for (hh, c), (alpha, p) in zip(chains, probs):
            qs = slice(c * qc, (c + 1) * qc)
            acc_ref[hh, :, qs] = alpha * acc_ref[hh, :, qs] + jnp.dot(v_t[hh], p,
                                                                      preferred_element_type=F32)

    @pl.when(ki < qi)
    def _():
        update(False)

    @pl.when(ki == qi)
    def _():
        update(True)
        for hh in range(FOX_HEADS_PER_STEP):
            o_ref[0, :, hh * dh:(hh + 1) * dh] = jnp.transpose(acc_ref[hh] / l_ref[hh]).astype(o_ref.dtype)


def fox_attention(z_a, eq, ek):
    b, s, _ = z_a.shape
    nb = s // ATT_BLOCK
    hp = FOX_HEADS_PER_STEP
    groups = FOX_HEADS // hp
    pairs = [(qi, ki) for qi in range(nb) for ki in range(qi + 1)]
    qi_tab = jnp.asarray([pr[0] for pr in pairs], jnp.int32)
    ki_tab = jnp.asarray([pr[1] for pr in pairs], jnp.int32)
    width = hp * FOX_HEAD_DIM
    grid_spec = pltpu.PrefetchScalarGridSpec(
        num_scalar_prefetch=2,
        grid=(b, groups, len(pairs)),
        in_specs=[pl.BlockSpec((1, ATT_BLOCK, width), lambda bi, g, st, qt, kt: (bi, qt[st], g)),
                  pl.BlockSpec((1, ATT_BLOCK, width), lambda bi, g, st, qt, kt: (bi, kt[st], groups + g)),
                  pl.BlockSpec((1, ATT_BLOCK, width), lambda bi, g, st, qt, kt: (bi, kt[st], 2 * groups + g)),
                  pl.BlockSpec((1, hp, ATT_BLOCK, LANES), lambda bi, g, st, qt, kt: (bi, g, qt[st], 0)),
                  pl.BlockSpec((1, hp, ATT_BLOCK, LANES), lambda bi, g, st, qt, kt: (bi, g, kt[st], 0))],
        out_specs=pl.BlockSpec((1, ATT_BLOCK, width), lambda bi, g, st, qt, kt: (bi, qt[st], g)),
        scratch_shapes=[pltpu.VMEM((hp, 1, ATT_BLOCK), F32), pltpu.VMEM((hp, 1, ATT_BLOCK), F32),
                        pltpu.VMEM((hp, FOX_HEAD_DIM, ATT_BLOCK), F32)],
    )
    return pl.pallas_call(
        _fox_kernel,
        grid_spec=grid_spec,
        out_shape=jax.ShapeDtypeStruct((b, s, BRANCH_WIDTH), BF16),
        compiler_params=_params(("parallel", "parallel", "arbitrary")),
        name="fox_attention",
    )(qi_tab, ki_tab, z_a, z_a, z_a, eq, ek)


def _pool_kernel(u_ref, halo_ref, w_ref, scale_ref, o_ref):
    i = pl.program_id(1)
    rows = POOL_ROWS
    t = i * rows + lax.broadcasted_iota(jnp.int32, (rows, 1), 0)
    keep_halo = (i > 0).astype(F32)
    for g, win in enumerate(POOL_WINDOWS):
        cols = slice(g * POOL_GROUP, (g + 1) * POOL_GROUP)
        u = u_ref[0, :, cols]
        ext = jnp.concatenate([halo_ref[0, :, cols] * keep_halo, u], axis=0)
        span = 1
        while span < win:
            n = ext.shape[0]
            ext = ext[:n - span] + ext[span:]
            span *= 2
        window_sum = ext[POOL_HALO + 1 - win:POOL_HALO + 1 - win + rows]
        count = jnp.minimum(t + 1, win).astype(F32)
        d = (window_sum / count - u).astype(BF16)
        y = jnp.dot(d, w_ref[0, g], preferred_element_type=F32) * scale_ref[:, cols]
        o_ref[0, :, cols] = y.astype(o_ref.dtype)


def multiscale_pool(z_b, pool_w, layer, pool_scale):
    b, s, _ = z_b.shape
    per = POOL_ROWS // POOL_HALO
    return pl.pallas_call(
        _pool_kernel,
        grid=(b, s // POOL_ROWS),
        in_specs=[pl.BlockSpec((1, POOL_ROWS, BRANCH_WIDTH), lambda bi, i: (bi, i, 0)),
                  pl.BlockSpec((1, POOL_HALO, BRANCH_WIDTH), lambda bi, i: (bi, jnp.maximum(i * per - 1, 0), 0)),
                  pl.BlockSpec((1, len(POOL_WINDOWS), POOL_GROUP, POOL_GROUP), lambda bi, i: (layer, 0, 0, 0)),
                  pl.BlockSpec((1, BRANCH_WIDTH), lambda bi, i: (0, 0))],
        out_specs=pl.BlockSpec((1, POOL_ROWS, BRANCH_WIDTH), lambda bi, i: (bi, i, 0)),
        out_shape=jax.ShapeDtypeStruct((b, s, BRANCH_WIDTH), BF16),
        compiler_params=_params(("parallel", "parallel")),
        name="multiscale_pool",
    )(z_b, z_b, pool_w, pool_scale.reshape(1, BRANCH_WIDTH))


def _sgu_kernel(u_ref, v_ref, nw_ref, w_ref, b_ref, o_ref):
    c = SGU_CHUNK
    v = jax.nn.gelu(v_ref[...].astype(F32))
    vc = v - jnp.mean(v, axis=-1, keepdims=True)
    vn = (vc * lax.rsqrt(jnp.mean(vc * vc, axis=-1, keepdims=True) + EPS) * nw_ref[...]).astype(BF16)
    causal = lax.broadcasted_iota(jnp.int32, (c, c), 0) >= lax.broadcasted_iota(jnp.int32, (c, c), 1)
    for g in range(SGU_GROUPS):
        cols = slice(g * SGU_GROUP_WIDTH, (g + 1) * SGU_GROUP_WIDTH)
        w = jnp.where(causal, w_ref[g], 0.0).astype(BF16)
        bias = b_ref[g]
        for n in range(SGU_ROWS // c):
            rows = slice(n * c, (n + 1) * c)
            sv = jnp.dot(w, vn[rows, cols], preferred_element_type=F32) + bias
            u = jax.nn.gelu(u_ref[rows, cols].astype(F32))
            o_ref[rows, cols] = (u * sv).astype(o_ref.dtype)


def spatial_gating(z_a, u_col0, norm_w, w_s, b_s):
    t = z_a.shape[0]
    ub = u_col0 // BRANCH_WIDTH
    return pl.pallas_call(
        _sgu_kernel,
        grid=(t // SGU_ROWS,),
        in_specs=[pl.BlockSpec((SGU_ROWS, BRANCH_WIDTH), lambda i: (i, ub)),
                  pl.BlockSpec((SGU_ROWS, BRANCH_WIDTH), lambda i: (i, ub + 1)),
                  pl.BlockSpec((1, BRANCH_WIDTH), lambda i: (0, 0)),
                  pl.BlockSpec((SGU_GROUPS, SGU_CHUNK, SGU_CHUNK), lambda i: (0, 0, 0)),
                  pl.BlockSpec((SGU_GROUPS, SGU_CHUNK, 1), lambda i: (0, 0, 0))],
        out_specs=pl.BlockSpec((SGU_ROWS, BRANCH_WIDTH), lambda i: (i, 0)),
        out_shape=jax.ShapeDtypeStruct((t, BRANCH_WIDTH), BF16),
        compiler_params=_params(("parallel",)),
        name="spatial_gating",
    )(z_a, z_a, norm_w.reshape(1, BRANCH_WIDTH), w_s, b_s.reshape(SGU_GROUPS, SGU_CHUNK, 1))


def _hgrn_kernel(q_ref, f_ref, v_ref, g_ref, lb_ref, nw_ref, o_ref, state_ref):
    c = HGRN_SUBCHUNK
    half = c // 2
    rows_total = HGRN_ROWS
    n = rows_total // c

    @pl.when(pl.program_id(2) == 0)
    def _():
        state_ref[...] = jnp.zeros_like(state_ref)

    local = lax.broadcasted_iota(jnp.int32, (rows_total, 1), 0) & (c - 1)
    t_half = lax.broadcasted_iota(jnp.int32, (1, half, 1), 1)
    for hh in range(HGRN_HEADS_PER_STEP):
        cols = slice(hh * LANES, (hh + 1) * LANES)
        lb = lb_ref[hh]
        q = q_ref[0, :, cols]
        fl = f_ref[0, :, cols]
        v = v_ref[0, :, cols]
        e = jnp.exp(-jnp.abs(fl))
        r = 1.0 / (1.0 + e)
        sig_pos = jnp.where(fl >= 0, r, e * r)
        sig_neg = jnp.where(fl >= 0, e * r, r)
        kk = (1.0 - lb) * sig_neg
        b = jnp.log2(lb + (1.0 - lb) * sig_pos)
        shift = 1
        while shift < c:
            b = b + jnp.where(local >= shift, pltpu.roll(b, shift, 0), 0.0)
            shift *= 2
        b3 = b.reshape(n, c, LANES)
        kk3 = kk.reshape(n, c, LANES)
        q3 = q.reshape(n, c, LANES)
        v3 = v.reshape(n, c, LANES)
        tot3 = b3[:, c - 1:c, :]
        q_dec = (q * jnp.exp2(b)).astype(BF16)
        k_dec = (kk3 * jnp.exp2(tot3 - b3)).reshape(rows_total, LANES).astype(BF16)
        decay = jnp.exp2(tot3)
        v_bf = v.astype(BF16)

        updates = [lax.dot_general(v_bf[i * c:(i + 1) * c], k_dec[i * c:(i + 1) * c], (((0,), (0,)), ((), ())),
                                   preferred_element_type=F32) for i in range(n)]
        state_t = state_ref[hh]
        states = []
        for i in range(n):
            states.append(state_t.astype(BF16))
            state_t = state_t * decay[i] + updates[i]
        state_ref[hh] = state_t
        outs = [lax.dot_general(q_dec[i * c:(i + 1) * c], states[i], (((1,), (1,)), ((), ())),
                                preferred_element_type=F32) for i in range(n)]

        b_lo, b_hi = b3[:, :half, :], b3[:, half:, :]
        q_lo, q_hi = q3[:, :half, :], q3[:, half:, :]
        o_lo = jnp.zeros((n, half, LANES), F32)
        o_hi = jnp.zeros((n, half, LANES), F32)
        for s in range(c):
            ks = kk3[:, s:s + 1, :]
            bs = b3[:, s:s + 1, :]
            vs = v3[:, s:s + 1, :]
            if s < half:
                d = jnp.where(t_half >= s, b_lo - bs, -jnp.inf)
                a = jnp.sum(q_lo * ks * jnp.exp2(d), axis=-1, keepdims=True)
                o_lo = o_lo + a * vs
                d = b_hi - bs
            else:
                d = jnp.where(t_half >= s - half, b_hi - bs, -jnp.inf)
            a = jnp.sum(q_hi * ks * jnp.exp2(d), axis=-1, keepdims=True)
            o_hi = o_hi + a * vs
        o = jnp.concatenate([o_lo, o_hi], axis=1).reshape(rows_total, LANES) + jnp.concatenate(outs, axis=0)

        g = g_ref[0, :, cols]
        o = o * lax.rsqrt(jnp.mean(o * o, axis=-1, keepdims=True) + EPS) * nw_ref[hh]
        o_ref[0, :, cols] = (o * (g * _sigmoid(g))).astype(o_ref.dtype)


def hgrn2(z_b, col0, lower_bound, norm_w):
    b, s, _ = z_b.shape
    hp = HGRN_HEADS_PER_STEP
    width = hp * LANES
    c0 = col0 // width
    groups = HGRN_HEADS // hp

    def spec(part):
        return pl.BlockSpec((1, HGRN_ROWS, width), lambda bi, h, i: (bi, i, c0 + part * groups + h))

    vec = pl.BlockSpec((hp, 1, LANES), lambda bi, h, i: (h, 0, 0))
    return pl.pallas_call(
        _hgrn_kernel,
        grid=(b, groups, s // HGRN_ROWS),
        in_specs=[spec(0), spec(1), spec(2), spec(3), vec, vec],
        out_specs=pl.BlockSpec((1, HGRN_ROWS, width), lambda bi, h, i: (bi, i, h)),
        out_shape=jax.ShapeDtypeStruct((b, s, BRANCH_WIDTH), BF16),
        scratch_shapes=[pltpu.VMEM((hp, HGRN_VAL_DIM, HGRN_KEY_DIM), F32)],
        compiler_params=_params(("parallel", "parallel", "arbitrary")),
        name="hgrn2",
    )(z_b, z_b, z_b, z_b, lower_bound.reshape(HGRN_HEADS, 1, LANES), norm_w.reshape(HGRN_HEADS, 1, LANES))


def _bf16_split2(x):
    hi = x.astype(BF16)
    return hi, (x - hi.astype(F32)).astype(BF16)


def _router_kernel(x_ref, w_ref, wr_ref, br_ref, n_ref, meta_ref, p_ref, cnt_ref, carry_ref):
    @pl.when(pl.program_id(0) == 0)
    def _():
        carry_ref[...] = jnp.zeros_like(carry_ref)

    n = _rms(x_ref[...], w_ref[...])
    n_ref[...] = n
    n_hi, n_lo = _bf16_split2(n)
    w_hi, w_lo = _bf16_split2(wr_ref[...])
    logits = (jnp.dot(n_hi, w_hi, preferred_element_type=F32) + jnp.dot(n_lo, w_hi, preferred_element_type=F32)
              + jnp.dot(n_hi, w_lo, preferred_element_type=F32)) + br_ref[...]
    rows = logits.shape[0]
    col = lax.broadcasted_iota(jnp.int32, logits.shape, 1)
    big = jnp.int32(LANES)
    neg = -jnp.inf

    def first_argmax(vals):
        top = jnp.max(vals, axis=-1, keepdims=True)
        return top, jnp.min(jnp.where(vals == top, col, big), axis=-1, keepdims=True)

    group_logits = jnp.where(col < N_GROUPS, logits, neg)
    g_top, g_idx = first_argmax(group_logits)
    p_group = 1.0 / jnp.sum(jnp.exp(group_logits - g_top), axis=-1, keepdims=True)
    lo = N_GROUPS + g_idx * EXPERTS_PER_GROUP
    in_group = jnp.where((col >= lo) & (col < lo + EXPERTS_PER_GROUP), logits, neg)
    top1, idx1 = first_argmax(in_group)
    top2, idx2 = first_argmax(jnp.where(col == idx1, neg, in_group))
    r = jnp.exp(top2 - top1)
    p1 = p_group / (1.0 + r)
    p2 = p_group * r / (1.0 + r)
    p_ref[...] = jnp.where(col == 0, p1, jnp.where(col == 1, p2, 0.0))

    chosen = ((col == idx1) | (col == idx2)).astype(BF16)
    earlier = (lax.broadcasted_iota(jnp.int32, (rows, rows), 0)
               > lax.broadcasted_iota(jnp.int32, (rows, rows), 1)).astype(BF16)
    before = jnp.dot(earlier, chosen, preferred_element_type=F32) + carry_ref[...]
    rank1 = jnp.sum(jnp.where(col == idx1, before, 0.0), axis=-1, keepdims=True).astype(jnp.int32)
    rank2 = jnp.sum(jnp.where(col == idx2, before, 0.0), axis=-1, keepdims=True).astype(jnp.int32)
    carry_ref[...] += jnp.sum(chosen.astype(F32), axis=0, keepdims=True)
    cnt_ref[...] = carry_ref[...]
    meta_ref[...] = jnp.where(col == 0, idx1 - N_GROUPS, jnp.where(col == 1, idx2 - N_GROUPS,
                              jnp.where(col == 2, rank1, jnp.where(col == 3, rank2, 0))))


def moe_router(h, norm_w, w_router, b_router):
    t, d = h.shape
    row = pl.BlockSpec((NORM_ROWS, d), lambda i: (i, 0))
    small = pl.BlockSpec((NORM_ROWS, LANES), lambda i: (i, 0))
    one = pl.BlockSpec((1, LANES), lambda i: (0, 0))
    return pl.pallas_call(
        _router_kernel,
        grid=(t // NORM_ROWS,),
        in_specs=[row, pl.BlockSpec((1, d), lambda i: (0, 0)), pl.BlockSpec((d, LANES), lambda i: (0, 0)), one],
        out_specs=[row, small, small, one],
        out_shape=[jax.ShapeDtypeStruct((t, d), F32), jax.ShapeDtypeStruct((t, LANES), jnp.int32),
                   jax.ShapeDtypeStruct((t, LANES), F32), jax.ShapeDtypeStruct((1, LANES), F32)],
        scratch_shapes=[pltpu.VMEM((1, LANES), F32)],
        compiler_params=_params(("arbitrary",)),
        name="moe_router",
    )(h, norm_w.reshape(1, d), w_router, b_router)


def _gather_rows(idx_ref, src_hbm, dst_ref, sem, unrolled):
    n_rows = dst_ref.shape[0]

    def start(r, priority):
        pltpu.make_async_copy(src_hbm.at[pl.ds(idx_ref[r], 1)], dst_ref.at[pl.ds(r, 1)], sem).start(priority=priority)

    if unrolled:
        for r in range(n_rows):
            start(r, r % 2)
    else:
        def body(r, carry):
            start(r, 0)
            return carry
        lax.fori_loop(0, n_rows, body, 0)


def _wait_rows(src_hbm, dst_ref, sem):
    pltpu.make_async_copy(src_hbm.at[pl.ds(0, dst_ref.shape[0])], dst_ref, sem).wait()


def _expert_kernel(be_ref, nused_ref, tok_cur_ref, tok_next_ref, x_hbm, wg_ref, wu_ref, wd_ref, y_ref,
                   xbuf_ref, sem_ref):
    i = pl.program_id(0)
    last = pl.num_programs(0) - 1
    n_used = nused_ref[0]
    slot = i % 2

    @pl.when((i == 0) & (n_used > 0))
    def _():
        _gather_rows(tok_cur_ref, x_hbm, xbuf_ref.at[0], sem_ref.at[0], unrolled=False)

    @pl.when(i < n_used)
    def _():
        _gather_rows(tok_next_ref, x_hbm, xbuf_ref.at[1 - slot], sem_ref.at[1 - slot], unrolled=True)
        _wait_rows(x_hbm, xbuf_ref.at[slot], sem_ref.at[slot])
        x = xbuf_ref[slot].astype(BF16)
        gate = jnp.dot(x, wg_ref[0], preferred_element_type=F32)
        up = jnp.dot(x, wu_ref[0], preferred_element_type=F32)
        hidden = (gate * _sigmoid(gate) * up).astype(BF16)
        y_ref[...] = jnp.dot(hidden, wd_ref[0], preferred_element_type=F32)

    @pl.when((i == n_used) & (n_used > 0))
    def _():
        _wait_rows(x_hbm, xbuf_ref.at[slot], sem_ref.at[slot])

    @pl.when((i == last) & (n_used > last))
    def _():
        _wait_rows(x_hbm, xbuf_ref.at[1 - slot], sem_ref.at[1 - slot])

    @pl.when(i >= n_used)
    def _():
        y_ref[...] = jnp.zeros_like(y_ref)


def expert_mlp(x, buf_tok, block_e, n_used, w_gate, w_up, w_down):
    p = buf_tok.shape[0]
    nb = p // MOE_ROWS
    d = x.shape[1]
    smem = pltpu.SMEM
    grid_spec = pltpu.PrefetchScalarGridSpec(
        num_scalar_prefetch=2,
        grid=(nb,),
        in_specs=[pl.BlockSpec((MOE_ROWS,), lambda i, be, nu: (i,), memory_space=smem),
                  pl.BlockSpec((MOE_ROWS,), lambda i, be, nu: (jnp.minimum(i + 1, nb - 1),), memory_space=smem),
                  pl.BlockSpec(memory_space=pl.ANY),
                  pl.BlockSpec((1, d, EXPERT_HIDDEN), lambda i, be, nu: (be[i], 0, 0)),
                  pl.BlockSpec((1, d, EXPERT_HIDDEN), lambda i, be, nu: (be[i], 0, 0)),
                  pl.BlockSpec((1, EXPERT_HIDDEN, d), lambda i, be, nu: (be[i], 0, 0))],
        out_specs=pl.BlockSpec((MOE_ROWS, d), lambda i, be, nu: (i, 0)),
        scratch_shapes=[pltpu.VMEM((2, MOE_ROWS, d), F32), pltpu.SemaphoreType.DMA((2,))],
    )
    return pl.pallas_call(
        _expert_kernel,
        grid_spec=grid_spec,
        out_shape=jax.ShapeDtypeStruct((p, d), F32),
        compiler_params=_params(("arbitrary",)),
        name="expert_mlp",
    )(block_e, n_used, buf_tok, buf_tok, x, w_gate, w_up, w_down)


def _combine_kernel(pos0_ref, pos1_ref, pos0_next_ref, pos1_next_ref, y_hbm, h_ref, p_ref, w_ref, o_ref, n_ref,
                    ybuf_ref, sem_ref):
    i = pl.program_id(0)
    last = pl.num_programs(0) - 1
    slot = i % 2

    @pl.when(i == 0)
    def _():
        _gather_rows(pos0_ref, y_hbm, ybuf_ref.at[0, 0], sem_ref.at[0, 0], unrolled=False)
        _gather_rows(pos1_ref, y_hbm, ybuf_ref.at[0, 1], sem_ref.at[0, 1], unrolled=False)

    _gather_rows(pos0_next_ref, y_hbm, ybuf_ref.at[1 - slot, 0], sem_ref.at[1 - slot, 0], unrolled=True)
    _gather_rows(pos1_next_ref, y_hbm, ybuf_ref.at[1 - slot, 1], sem_ref.at[1 - slot, 1], unrolled=True)
    _wait_rows(y_hbm, ybuf_ref.at[slot, 0], sem_ref.at[slot, 0])
    _wait_rows(y_hbm, ybuf_ref.at[slot, 1], sem_ref.at[slot, 1])
    p = p_ref[...]
    h = h_ref[...] + (ybuf_ref[slot, 0] * p[:, 0:1] + ybuf_ref[slot, 1] * p[:, 1:2])
    o_ref[...] = h
    n_ref[...] = _rms(h, w_ref[...]).astype(n_ref.dtype)

    @pl.when(i == last)
    def _():
        _wait_rows(y_hbm, ybuf_ref.at[1 - slot, 0], sem_ref.at[1 - slot, 0])
        _wait_rows(y_hbm, ybuf_ref.at[1 - slot, 1], sem_ref.at[1 - slot, 1])


def moe_combine(y, pos0, pos1, probs, h, norm_w):
    t, d = h.shape
    nb = t // COMBINE_ROWS
    smem = pltpu.SMEM
    row = pl.BlockSpec((COMBINE_ROWS, d), lambda i: (i, 0))
    cur = pl.BlockSpec((COMBINE_ROWS,), lambda i: (i,), memory_space=smem)
    nxt = pl.BlockSpec((COMBINE_ROWS,), lambda i: (jnp.minimum(i + 1, nb - 1),), memory_space=smem)
    return pl.pallas_call(
        _combine_kernel,
        grid=(nb,),
        in_specs=[cur, cur, nxt, nxt,
                  pl.BlockSpec(memory_space=pl.ANY),
                  row,
                  pl.BlockSpec((COMBINE_ROWS, LANES), lambda i: (i, 0)),
                  pl.BlockSpec((1, d), lambda i: (0, 0))],
        out_specs=[row, row],
        out_shape=[jax.ShapeDtypeStruct((t, d), F32), jax.ShapeDtypeStruct((t, d), BF16)],
        scratch_shapes=[pltpu.VMEM((2, TOP_K, COMBINE_ROWS, d), F32), pltpu.SemaphoreType.DMA((2, TOP_K))],
        compiler_params=_params(("arbitrary",)),
        name="moe_combine",
    )(pos0, pos1, pos0, pos1, y, h, probs, norm_w.reshape(1, d))


def _dispatch_plan(meta, counts):
    t = meta.shape[0]
    a = t * TOP_K
    expert_ids = meta[:, :TOP_K]
    ranks = meta[:, TOP_K:2 * TOP_K]
    counts = counts[0, N_GROUPS:N_GROUPS + N_EXPERTS].astype(jnp.int32)
    padded = (counts + MOE_ROWS - 1) // MOE_ROWS * MOE_ROWS
    pad_end = jnp.cumsum(padded)
    pad_start = pad_end - padded
    dest = pad_start[expert_ids] + ranks
    p = a + N_EXPERTS * MOE_ROWS
    nb = p // MOE_ROWS
    token_ids = jnp.repeat(jnp.arange(t, dtype=jnp.int32), TOP_K)
    buf_tok = jnp.zeros((p,), jnp.int32).at[dest.reshape(a)].set(token_ids, unique_indices=True)
    block_start = jnp.arange(nb, dtype=jnp.int32) * MOE_ROWS
    block_e = jnp.minimum(jnp.sum((pad_end[None, :] <= block_start[:, None]).astype(jnp.int32), axis=1),
                          N_EXPERTS - 1)
    n_used = (pad_end[-1:] // MOE_ROWS).astype(jnp.int32)
    return buf_tok, block_e, n_used, dest[:, 0], dest[:, 1]


def _pad_cols(w, n):
    return jnp.pad(w, ((0, 0), (0, n - w.shape[1])))


def _router_weights(i, router_group_w, router_group_b, router_expert_w, router_expert_b):
    w_r = _pad_cols(jnp.concatenate([router_group_w[i], router_expert_w[i]], axis=1), LANES)
    b_r = _pad_cols(jnp.concatenate([router_group_b[i], router_expert_b[i]])[None, :], LANES)
    return w_r, b_r


def kernel(x, p, mix_norm_w, w_in, fox_f_bias, pool_w, pool_scale, sgu_norm_w, sgu_w, sgu_b,
           hgrn_lb_logits, hgrn_norm_w, branch_proj, w_out, ffn_norm_w, router_group_w,
           router_group_b, router_expert_w, router_expert_b, expert_w_gate, expert_w_up,
           expert_w_down, ple_norm_w, ple_gate_w, ple_proj_w, final_norm_w):
    bsz, seq, d = x.shape
    t = bsz * seq
    lb_p = jax.nn.softmax(hgrn_lb_logits.astype(F32), axis=0)
    lower_bounds = jnp.cumsum(lb_p, axis=0) - lb_p[0:1]
    sgu_u_col = FOX_QKV_END
    gate_col = FOX_QKV_END + 2 * BRANCH_WIDTH
    h = x.reshape(t, d)
    w_in_t = jnp.transpose(w_in, (0, 2, 1))

    def tiles(*ranges):
        return [c for lo, hi in ranges for c in range(lo, hi, MM_COLS)]

    starts_a = tiles((0, FOX_QKV_END), (POOL_END, SGU_END), (HGRN_END, IN_COLS))
    starts_b = tiles((FOX_F_END, POOL_END), (SGU_END, HGRN_END))
    n_a = len(starts_a) * MM_COLS
    n_b = len(starts_b) * MM_COLS
    q_scale = jnp.where(jnp.arange(n_a) < BRANCH_WIDTH, FOX_HEAD_DIM ** -0.5 * LOG2E, 1.0).astype(F32)[None, :]
    pool_w_bf, branch_proj_bf, w_out_bf, ple_gate_bf, ple_proj_bf = (
        w.astype(BF16) for w in (pool_w, branch_proj, w_out, ple_gate_w, ple_proj_w))
    p_bf = p.reshape(DEPTH, t, PLE_DIM).astype(BF16)
    for i in range(DEPTH):
        w_r, b_r = _router_weights(i, router_group_w, router_group_b, router_expert_w, router_expert_b)
        n, z_f = rms_norm_proj(h, mix_norm_w[i], w_in_t, i, FOX_QKV_END, BF16)
        z_a, (w_gate_bf, w_up_bf) = in_projection(n, w_in_t, i, starts_a, q_scale, BF16,
                                                  cast=((expert_w_gate, CAST_ROWS), (expert_w_up, CAST_ROWS)))
        z_b, (w_down_bf,) = in_projection(n, w_in_t, i, starts_b, jnp.ones((1, n_b), F32), F32,
                                          cast=((expert_w_down, CAST_ROWS),))
        z_a3 = z_a.reshape(bsz, seq, -1)
        z_b3 = z_b.reshape(bsz, seq, -1)

        f_bias = _pad_cols(fox_f_bias[i][None, :].astype(F32), LANES)
        eq, ek = forget_prep(z_f.reshape(bsz, seq, LANES), f_bias)
        br_fox = fox_attention(z_a3, eq, ek).reshape(t, BRANCH_WIDTH)
        br_pool = multiscale_pool(z_b3, pool_w_bf, i, pool_scale[i]).reshape(t, BRANCH_WIDTH)
        br_sgu = spatial_gating(z_a, sgu_u_col, sgu_norm_w[i], sgu_w[i], sgu_b[i])
        br_hgrn = hgrn2(z_b3, BRANCH_WIDTH, lower_bounds[i], hgrn_norm_w[i]).reshape(t, BRANCH_WIDTH)

        merged = merge_branches((br_fox, br_pool, br_sgu, br_hgrn), branch_proj_bf, i, z_a, gate_col)
        h = matmul_residual(merged, w_out_bf, i, h)

        xn, meta, probs, counts = moe_router(h, ffn_norm_w[i], w_r, b_r)
        buf_tok, block_e, n_used, pos0, pos1 = _dispatch_plan(meta, counts)
        y = expert_mlp(xn, buf_tok, block_e, n_used, w_gate_bf, w_up_bf, w_down_bf)
        h, n_ple = moe_combine(y, pos0, pos1, probs, h, ple_norm_w[i])

        h = ple_update(n_ple, ple_gate_bf, p_bf, ple_proj_bf, i, h)
    return rms_norm(h, final_norm_w, F32).reshape(bsz, seq, d)
```

```python
import functools

import jax
import jax.numpy as jnp
from jax import lax
from jax.experimental import pallas as pl
from jax.experimental.pallas import tpu as pltpu

F32 = jnp.float32
BF16 = jnp.bfloat16
HIGHEST = lax.Precision.HIGHEST

D_MODEL = 2048
DEPTH = 2
BRANCH_WIDTH = D_MODEL // 2
N_BRANCHES = 4
PLE_DIM = 256
EPS = 1e-6
FOX_HEAD_DIM = 128
FOX_HEADS = BRANCH_WIDTH // FOX_HEAD_DIM
POOL_WINDOWS = (2, 4, 8, 16)
POOL_GROUP = BRANCH_WIDTH // len(POOL_WINDOWS)
POOL_HALO = 16
SGU_GROUPS = 4
SGU_CHUNK = 128
SGU_GROUP_WIDTH = BRANCH_WIDTH // SGU_GROUPS
HGRN_KEY_DIM = 128
HGRN_VAL_DIM = 128
HGRN_HEADS = BRANCH_WIDTH // HGRN_VAL_DIM
HGRN_SUBCHUNK = 16
HGRN_HEADS_PER_STEP = 2
N_GROUPS = 4
EXPERTS_PER_GROUP = 8
N_EXPERTS = N_GROUPS * EXPERTS_PER_GROUP
TOP_K = 2
EXPERT_HIDDEN = D_MODEL // 2
FOX_QKV_END = 3 * BRANCH_WIDTH
FOX_F_END = FOX_QKV_END + FOX_HEADS
POOL_END = FOX_F_END + BRANCH_WIDTH
SGU_END = POOL_END + 2 * BRANCH_WIDTH
HGRN_END = SGU_END + 4 * BRANCH_WIDTH
IN_COLS = HGRN_END + N_BRANCHES * D_MODEL

LANES = 128
SUBLANES = 8
LOG2E = 1.4426950408889634
FOX_HEADS_PER_STEP = 8
FOX_QUERY_CHUNK = 256
VMEM_LIMIT = 56 * 1024 * 1024

NORM_ROWS = 512
MM_ROWS = 2048
MM_COLS = 512
CAST_ROWS = 512
MERGE_ROWS = 1024
ATT_BLOCK = 512
CUMSUM_BLOCK = 512
POOL_ROWS = 512
SGU_ROWS = 512
HGRN_ROWS = 512
MOE_ROWS = 256
COMBINE_ROWS = 256
PLE_ROWS = 2048


def _params(semantics, **kw):
    return pltpu.CompilerParams(dimension_semantics=semantics, vmem_limit_bytes=VMEM_LIMIT, **kw)


def _sigmoid(x):
    return 1.0 / (1.0 + jnp.exp(-x))


def _log_sigmoid(x):
    return jnp.minimum(x, 0.0) - jnp.log(1.0 + jnp.exp(-jnp.abs(x)))


def _rms(x, w):
    return x * lax.rsqrt(jnp.mean(x * x, axis=-1, keepdims=True) + EPS) * w


def _norm_kernel(x_ref, w_ref, n_ref):
    n_ref[...] = _rms(x_ref[...], w_ref[...]).astype(n_ref.dtype)


def rms_norm(x, w, out_dtype):
    t, d = x.shape
    return pl.pallas_call(
        _norm_kernel,
        grid=(t // NORM_ROWS,),
        in_specs=[pl.BlockSpec((NORM_ROWS, d), lambda i: (i, 0)),
                  pl.BlockSpec((1, d), lambda i: (0, 0))],
        out_specs=pl.BlockSpec((NORM_ROWS, d), lambda i: (i, 0)),
        out_shape=jax.ShapeDtypeStruct((t, d), out_dtype),
        compiler_params=_params(("parallel",)),
        name="rms_norm",
    )(x, w.reshape(1, d))


def _norm_proj_kernel(x_ref, w_ref, wpt_ref, n_ref, p_ref):
    n = _rms(x_ref[...], w_ref[...]).astype(BF16)
    n_ref[...] = n.astype(n_ref.dtype)
    p_ref[...] = lax.dot_general(n, wpt_ref[0].astype(BF16), (((1,), (1,)), ((), ())), preferred_element_type=F32)


def rms_norm_proj(x, w, wt, layer, row0, out_dtype):
    t, d = x.shape
    jp = row0 // LANES
    return pl.pallas_call(
        _norm_proj_kernel,
        grid=(t // NORM_ROWS,),
        in_specs=[pl.BlockSpec((NORM_ROWS, d), lambda i: (i, 0)),
                  pl.BlockSpec((1, d), lambda i: (0, 0)),
                  pl.BlockSpec((1, LANES, d), lambda i: (layer, jp, 0))],
        out_specs=[pl.BlockSpec((NORM_ROWS, d), lambda i: (i, 0)),
                   pl.BlockSpec((NORM_ROWS, LANES), lambda i: (i, 0))],
        out_shape=[jax.ShapeDtypeStruct((t, d), out_dtype),
                   jax.ShapeDtypeStruct((t, LANES), F32)],
        compiler_params=_params(("parallel",)),
        name="rms_norm_proj",
    )(x, w.reshape(1, d), wt)


def _in_proj_kernel(start_tiles_ref, a_ref, wt_ref, s_ref, *rest, n_cast):
    cast_in, o_ref, cast_out = rest[:n_cast], rest[n_cast], rest[n_cast + 1:]
    w = wt_ref[0].astype(BF16)
    acc = lax.dot_general(a_ref[...], w, (((1,), (1,)), ((), ())), preferred_element_type=F32)
    o_ref[...] = (acc * s_ref[...]).astype(o_ref.dtype)
    for src_ref, dst_ref in zip(cast_in, cast_out):
        dst_ref[...] = src_ref[0].astype(dst_ref.dtype)


def in_projection(a, wt, layer, starts, col_scale, out_dtype, cast=()):
    m, k = a.shape
    assert all(c % SUBLANES == 0 for c in starts)
    n = len(starts) * MM_COLS
    nj = len(starts)
    n_steps = (m // MM_ROWS) * nj
    cast_specs_in, cast_specs_out, cast_shapes = [], [], []
    for arr, rows in cast:
        _, e, r, c = arr.shape
        per = r // rows
        n_blocks = e * per
        assert n_blocks <= n_steps

        def block(i, j, st, per=per, n_blocks=n_blocks):
            kk = jnp.minimum(i * nj + j, n_blocks - 1)
            return kk // per, kk % per

        cast_specs_in.append(pl.BlockSpec((1, 1, rows, c), lambda i, j, st, block=block: (layer, *block(i, j, st), 0)))
        cast_specs_out.append(pl.BlockSpec((1, rows, c), lambda i, j, st, block=block: (*block(i, j, st), 0)))
        cast_shapes.append(jax.ShapeDtypeStruct((e, r, c), BF16))
    grid_spec = pltpu.PrefetchScalarGridSpec(
        num_scalar_prefetch=1,
        grid=(m // MM_ROWS, nj),
        in_specs=[pl.BlockSpec((MM_ROWS, k), lambda i, j, st: (i, 0)),
                  pl.BlockSpec((pl.Element(1), pl.Element(MM_COLS), pl.Element(k)),
                               lambda i, j, st: (layer, st[j] * SUBLANES, 0)),
                  pl.BlockSpec((1, MM_COLS), lambda i, j, st: (0, j))] + cast_specs_in,
        out_specs=[pl.BlockSpec((MM_ROWS, MM_COLS), lambda i, j, st: (i, j))] + cast_specs_out,
    )
    outs = pl.pallas_call(
        functools.partial(_in_proj_kernel, n_cast=len(cast)),
        grid_spec=grid_spec,
        out_shape=[jax.ShapeDtypeStruct((m, n), out_dtype)] + cast_shapes,
        compiler_params=_params(("arbitrary", "arbitrary")),
        name="in_projection",
    )(jnp.asarray([c // SUBLANES for c in starts], jnp.int32), a, wt, col_scale, *[arr for arr, _ in cast])
    return outs[0], outs[1:]


def _mm_res_kernel(a_ref, w_ref, r_ref, o_ref):
    o_ref[...] = r_ref[...] + jnp.dot(a_ref[...], w_ref[0], preferred_element_type=F32)


def matmul_residual(a, w, layer, res):
    m, k = a.shape
    n = w.shape[2]
    return pl.pallas_call(
        _mm_res_kernel,
        grid=(m // MM_ROWS, n // MM_COLS),
        in_specs=[pl.BlockSpec((MM_ROWS, k), lambda i, j: (i, 0)),
                  pl.BlockSpec((1, k, MM_COLS), lambda i, j: (layer, 0, j)),
                  pl.BlockSpec((MM_ROWS, MM_COLS), lambda i, j: (i, j))],
        out_specs=pl.BlockSpec((MM_ROWS, MM_COLS), lambda i, j: (i, j)),
        out_shape=jax.ShapeDtypeStruct((m, n), F32),
        compiler_params=_params(("parallel", "arbitrary")),
        name="matmul_residual",
    )(a, w, res)


def _merge_kernel(b0_ref, b1_ref, b2_ref, b3_ref, p_ref, g0_ref, g1_ref, g2_ref, g3_ref, o_ref):
    acc = None
    for bi, (b_ref, g_ref) in enumerate(((b0_ref, g0_ref), (b1_ref, g1_ref),
                                         (b2_ref, g2_ref), (b3_ref, g3_ref))):
        y = jnp.dot(b_ref[...], p_ref[0, bi], preferred_element_type=F32)
        y = _sigmoid(g_ref[...].astype(F32)) * y
        acc = y if acc is None else acc + y
    o_ref[...] = acc.astype(o_ref.dtype)


def merge_branches(branches, proj, layer, z_a, gate_col0):
    t = branches[0].shape[0]
    g0 = gate_col0 // MM_COLS
    per = D_MODEL // MM_COLS
    b_spec = pl.BlockSpec((MERGE_ROWS, BRANCH_WIDTH), lambda i, j: (i, 0))
    g_specs = [pl.BlockSpec((MERGE_ROWS, MM_COLS), functools.partial(lambda i, j, o: (i, o + j), o=g0 + bi * per))
               for bi in range(N_BRANCHES)]
    return pl.pallas_call(
        _merge_kernel,
        grid=(t // MERGE_ROWS, per),
        in_specs=[b_spec] * N_BRANCHES
        + [pl.BlockSpec((1, N_BRANCHES, BRANCH_WIDTH, MM_COLS), lambda i, j: (layer, 0, 0, j))] + g_specs,
        out_specs=pl.BlockSpec((MERGE_ROWS, MM_COLS), lambda i, j: (i, j)),
        out_shape=jax.ShapeDtypeStruct((t, D_MODEL), BF16),
        compiler_params=_params(("parallel", "arbitrary")),
        name="merge_branches",
    )(*branches, proj, z_a, z_a, z_a, z_a)


def _ple_kernel(n_ref, wg_ref, p_ref, wp_ref, h_ref, o_ref):
    gate = _sigmoid(jnp.dot(n_ref[...], wg_ref[0], preferred_element_type=F32))
    emb = jnp.dot(p_ref[0], wp_ref[0], preferred_element_type=F32)
    o_ref[...] = h_ref[...] + gate * emb


def ple_update(n, wg, p, wp, layer, h):
    t = h.shape[0]
    return pl.pallas_call(
        _ple_kernel,
        grid=(t // PLE_ROWS, D_MODEL // MM_COLS),
        in_specs=[pl.BlockSpec((PLE_ROWS, D_MODEL), lambda i, j: (i, 0)),
                  pl.BlockSpec((1, D_MODEL, MM_COLS), lambda i, j: (layer, 0, j)),
                  pl.BlockSpec((1, PLE_ROWS, PLE_DIM), lambda i, j: (layer, i, 0)),
                  pl.BlockSpec((1, PLE_DIM, MM_COLS), lambda i, j: (layer, 0, j)),
                  pl.BlockSpec((PLE_ROWS, MM_COLS), lambda i, j: (i, j))],
        out_specs=pl.BlockSpec((PLE_ROWS, MM_COLS), lambda i, j: (i, j)),
        out_shape=jax.ShapeDtypeStruct((t, D_MODEL), F32),
        compiler_params=_params(("parallel", "arbitrary")),
        name="ple_update",
    )(n, wg, p, wp, h)


def _bf16_split3(x):
    hi = x.astype(BF16).astype(F32)
    r = x - hi
    mid = r.astype(BF16).astype(F32)
    lo = (r - mid).astype(BF16).astype(F32)
    return hi, mid, lo


def _forget_prep_kernel(z_ref, b_ref, eq_ref, ek_ref, carry_ref):
    @pl.when(pl.program_id(1) == 0)
    def _():
        carry_ref[...] = jnp.zeros_like(carry_ref)

    lf = _log_sigmoid(z_ref[0] + b_ref[...]) * LOG2E
    n = lf.shape[0]
    tril = (lax.broadcasted_iota(jnp.int32, (n, n), 0) >= lax.broadcasted_iota(jnp.int32, (n, n), 1)).astype(F32)
    c = jnp.dot(tril, lf, precision=HIGHEST, preferred_element_type=F32) + carry_ref[...]
    carry_ref[...] = c[n - 1:n, :]
    parts = _bf16_split3(c)
    lane = lax.broadcasted_iota(jnp.int32, (n, LANES), 1)
    for h in range(FOX_HEADS):
        hi, mid, lo = (jnp.sum(jnp.where(lane == h, part, 0.0), axis=-1, keepdims=True) for part in parts)
        eq = jnp.where(lane == 0, hi, jnp.where(lane == 1, mid, jnp.where(lane == 2, lo,
                       jnp.where(lane < 6, 1.0, 0.0))))
        ek = jnp.where(lane < 3, 1.0, jnp.where(lane == 3, -hi, jnp.where(lane == 4, -mid,
                       jnp.where(lane == 5, -lo, 0.0))))
        eq_ref[0, h] = eq.astype(BF16)
        ek_ref[0, h] = ek.astype(BF16)


def forget_prep(z_f, bias):
    b, s, _ = z_f.shape
    out = pl.BlockSpec((1, FOX_HEADS, CUMSUM_BLOCK, LANES), lambda bi, i: (bi, 0, i, 0))
    shape = jax.ShapeDtypeStruct((b, FOX_HEADS, s, LANES), BF16)
    return pl.pallas_call(
        _forget_prep_kernel,
        grid=(b, s // CUMSUM_BLOCK),
        in_specs=[pl.BlockSpec((1, CUMSUM_BLOCK, LANES), lambda bi, i: (bi, i, 0)),
                  pl.BlockSpec((1, LANES), lambda bi, i: (0, 0))],
        out_specs=[out, out],
        out_shape=[shape, shape],
        scratch_shapes=[pltpu.VMEM((1, LANES), F32)],
        compiler_params=_params(("parallel", "arbitrary")),
        name="forget_prep",
    )(z_f, bias)


def _fox_kernel(qi_ref, ki_ref, q_ref, k_ref, v_ref, eq_ref, ek_ref, o_ref, m_ref, l_ref, acc_ref):
    step = pl.program_id(2)
    qi = qi_ref[step]
    ki = ki_ref[step]
    blk = ATT_BLOCK
    dh = FOX_HEAD_DIM
    qc = FOX_QUERY_CHUNK

    @pl.when(ki == 0)
    def _():
        m_ref[...] = jnp.full_like(m_ref, -jnp.inf)
        l_ref[...] = jnp.zeros_like(l_ref)
        acc_ref[...] = jnp.zeros_like(acc_ref)

    def update(masked):
        chains = [(hh, c) for hh in range(FOX_HEADS_PER_STEP) for c in range(blk // qc)]
        scores = []
        for hh, c in chains:
            cols = slice(hh * dh, (hh + 1) * dh)
            qs = slice(c * qc, (c + 1) * qc)
            k = jnp.concatenate([k_ref[0, :, cols], ek_ref[0, hh]], axis=1)
            q = jnp.concatenate([q_ref[0, qs, cols], eq_ref[0, hh, qs, :]], axis=1)
            s = lax.dot_general(k, q, (((1,), (1,)), ((), ())), preferred_element_type=F32)
            if masked:
                key = lax.broadcasted_iota(jnp.int32, (blk, qc), 0)
                qry = lax.broadcasted_iota(jnp.int32, (blk, qc), 1) + c * qc
                s = jnp.where(key <= qry, s, -jnp.inf)
            scores.append(s)
        probs = []
        for (hh, c), s in zip(chains, scores):
            qs = slice(c * qc, (c + 1) * qc)
            m_old = m_ref[hh, :, qs]
            m_new = jnp.maximum(m_old, jnp.max(s, axis=0, keepdims=True))
            alpha = jnp.exp2(m_old - m_new)
            p = jnp.exp2(s - m_new)
            l_ref[hh, :, qs] = alpha * l_ref[hh, :, qs] + jnp.sum(p, axis=0, keepdims=True)
            m_ref[hh, :, qs] = m_new
            probs.append((alpha, p.astype(BF16)))
        v_t = [jnp.transpose(v_ref[0, :, hh * dh:(hh + 1) * dh]) for hh in range(FOX_HEADS_PER_STEP)]
        for (hh, c), (alpha, p) in zip(chains, probs):
            qs = slice(c * qc, (c + 1) * qc)
            acc_ref[hh, :, qs] = alpha * acc_ref[hh, :, qs] + jnp.dot(v_t[hh], p,
                                                                      preferred_element_type=F32)

    @pl.when(ki < qi)
    def _():
        update(False)

    @pl.when(ki == qi)
    def _():
        update(True)
        for hh in range(FOX_HEADS_PER_STEP):
            o_ref[0, :, hh * dh:(hh + 1) * dh] = jnp.transpose(acc_ref[hh] / l_ref[hh]).astype(o_ref.dtype)


def fox_attention(z_a, eq, ek):
    b, s, _ = z_a.shape
    nb = s // ATT_BLOCK
    hp = FOX_HEADS_PER_STEP
    groups = FOX_HEADS // hp
    pairs = [(qi, ki) for qi in range(nb) for ki in range(qi + 1)]
    qi_tab = jnp.asarray([pr[0] for pr in pairs], jnp.int32)
    ki_tab = jnp.asarray([pr[1] for pr in pairs], jnp.int32)
    width = hp * FOX_HEAD_DIM
    grid_spec = pltpu.PrefetchScalarGridSpec(
        num_scalar_prefetch=2,
        grid=(b, groups, len(pairs)),
        in_specs=[pl.BlockSpec((1, ATT_BLOCK, width), lambda bi, g, st, qt, kt: (bi, qt[st], g)),
                  pl.BlockSpec((1, ATT_BLOCK, width), lambda bi, g, st, qt, kt: (bi, kt[st], groups + g)),
                  pl.BlockSpec((1, ATT_BLOCK, width), lambda bi, g, st, qt, kt: (bi, kt[st], 2 * groups + g)),
                  pl.BlockSpec((1, hp, ATT_BLOCK, LANES), lambda bi, g, st, qt, kt: (bi, g, qt[st], 0)),
                  pl.BlockSpec((1, hp, ATT_BLOCK, LANES), lambda bi, g, st, qt, kt: (bi, g, kt[st], 0))],
        out_specs=pl.BlockSpec((1, ATT_BLOCK, width), lambda bi, g, st, qt, kt: (bi, qt[st], g)),
        scratch_shapes=[pltpu.VMEM((hp, 1, ATT_BLOCK), F32), pltpu.VMEM((hp, 1, ATT_BLOCK), F32),
                        pltpu.VMEM((hp, FOX_HEAD_DIM, ATT_BLOCK), F32)],
    )
    return pl.pallas_call(
        _fox_kernel,
        grid_spec=grid_spec,
        out_shape=jax.ShapeDtypeStruct((b, s, BRANCH_WIDTH), BF16),
        compiler_params=_params(("parallel", "parallel", "arbitrary")),
        name="fox_attention",
    )(qi_tab, ki_tab, z_a, z_a, z_a, eq, ek)


def _pool_kernel(u_ref, halo_ref, w_ref, scale_ref, o_ref):
    i = pl.program_id(1)
    rows = POOL_ROWS
    t = i * rows + lax.broadcasted_iota(jnp.int32, (rows, 1), 0)
    keep_halo = (i > 0).astype(F32)
    for g, win in enumerate(POOL_WINDOWS):
        cols = slice(g * POOL_GROUP, (g + 1) * POOL_GROUP)
        u = u_ref[0, :, cols]
        ext = jnp.concatenate([halo_ref[0, :, cols] * keep_halo, u], axis=0)
        span = 1
        while span < win:
            n = ext.shape[0]
            ext = ext[:n - span] + ext[span:]
            span *= 2
        window_sum = ext[POOL_HALO + 1 - win:POOL_HALO + 1 - win + rows]
        count = jnp.minimum(t + 1, win).astype(F32)
        d = (window_sum / count - u).astype(BF16)
        y = jnp.dot(d, w_ref[0, g], preferred_element_type=F32) * scale_ref[:, cols]
        o_ref[0, :, cols] = y.astype(o_ref.dtype)


def multiscale_pool(z_b, pool_w, layer, pool_scale):
    b, s, _ = z_b.shape
    per = POOL_ROWS // POOL_HALO
    return pl.pallas_call(
        _pool_kernel,
        grid=(b, s // POOL_ROWS),
        in_specs=[pl.BlockSpec((1, POOL_ROWS, BRANCH_WIDTH), lambda bi, i: (bi, i, 0)),
                  pl.BlockSpec((1, POOL_HALO, BRANCH_WIDTH), lambda bi, i: (bi, jnp.maximum(i * per - 1, 0), 0)),
                  pl.BlockSpec((1, len(POOL_WINDOWS), POOL_GROUP, POOL_GROUP), lambda bi, i: (layer, 0, 0, 0)),
                  pl.BlockSpec((1, BRANCH_WIDTH), lambda bi, i: (0, 0))],
        out_specs=pl.BlockSpec((1, POOL_ROWS, BRANCH_WIDTH), lambda bi, i: (bi, i, 0)),
        out_shape=jax.ShapeDtypeStruct((b, s, BRANCH_WIDTH), BF16),
        compiler_params=_params(("parallel", "parallel")),
        name="multiscale_pool",
    )(z_b, z_b, pool_w, pool_scale.reshape(1, BRANCH_WIDTH))


def _sgu_kernel(u_ref, v_ref, nw_ref, w_ref, b_ref, o_ref):
    c = SGU_CHUNK
    v = jax.nn.gelu(v_ref[...].astype(F32))
    vc = v - jnp.mean(v, axis=-1, keepdims=True)
    vn = (vc * lax.rsqrt(jnp.mean(vc * vc, axis=-1, keepdims=True) + EPS) * nw_ref[...]).astype(BF16)
    causal = lax.broadcasted_iota(jnp.int32, (c, c), 0) >= lax.broadcasted_iota(jnp.int32, (c, c), 1)
    for g in range(SGU_GROUPS):
        cols = slice(g * SGU_GROUP_WIDTH, (g + 1) * SGU_GROUP_WIDTH)
        w = jnp.where(causal, w_ref[g], 0.0).astype(BF16)
        bias = b_ref[g]
        for n in range(SGU_ROWS // c):
            rows = slice(n * c, (n + 1) * c)
            sv = jnp.dot(w, vn[rows, cols], preferred_element_type=F32) + bias
            u = jax.nn.gelu(u_ref[rows, cols].astype(F32))
            o_ref[rows, cols] = (u * sv).astype(o_ref.dtype)


def spatial_gating(z_a, u_col0, norm_w, w_s, b_s):
    t = z_a.shape[0]
    ub = u_col0 // BRANCH_WIDTH
    return pl.pallas_call(
        _sgu_kernel,
        grid=(t // SGU_ROWS,),
        in_specs=[pl.BlockSpec((SGU_ROWS, BRANCH_WIDTH), lambda i: (i, ub)),
                  pl.BlockSpec((SGU_ROWS, BRANCH_WIDTH), lambda i: (i, ub + 1)),
                  pl.BlockSpec((1, BRANCH_WIDTH), lambda i: (0, 0)),
                  pl.BlockSpec((SGU_GROUPS, SGU_CHUNK, SGU_CHUNK), lambda i: (0, 0, 0)),
                  pl.BlockSpec((SGU_GROUPS, SGU_CHUNK, 1), lambda i: (0, 0, 0))],
        out_specs=pl.BlockSpec((SGU_ROWS, BRANCH_WIDTH), lambda i: (i, 0)),
        out_shape=jax.ShapeDtypeStruct((t, BRANCH_WIDTH), BF16),
        compiler_params=_params(("parallel",)),
        name="spatial_gating",
    )(z_a, z_a, norm_w.reshape(1, BRANCH_WIDTH), w_s, b_s.reshape(SGU_GROUPS, SGU_CHUNK, 1))


def _hgrn_kernel(q_ref, f_ref, v_ref, g_ref, lb_ref, nw_ref, o_ref, state_ref):
    c = HGRN_SUBCHUNK
    half = c // 2
    rows_total = HGRN_ROWS
    n = rows_total // c

    @pl.when(pl.program_id(2) == 0)
    def _():
        state_ref[...] = jnp.zeros_like(state_ref)

    local = lax.broadcasted_iota(jnp.int32, (rows_total, 1), 0) & (c - 1)
    t_half = lax.broadcasted_iota(jnp.int32, (1, half, 1), 1)
    for hh in range(HGRN_HEADS_PER_STEP):
        cols = slice(hh * LANES, (hh + 1) * LANES)
        lb = lb_ref[hh]
        q = q_ref[0, :, cols]
        fl = f_ref[0, :, cols]
        v = v_ref[0, :, cols]
        e = jnp.exp(-jnp.abs(fl))
        r = 1.0 / (1.0 + e)
        sig_pos = jnp.where(fl >= 0, r, e * r)
        sig_neg = jnp.where(fl >= 0, e * r, r)
        kk = (1.0 - lb) * sig_neg
        log_kk = jnp.log2(kk)
        b = jnp.log2(lb + (1.0 - lb) * sig_pos)
        shift = 1
        while shift < c:
            b = b + jnp.where(local >= shift, pltpu.roll(b, shift, 0), 0.0)
            shift *= 2
        b3 = b.reshape(n, c, LANES)
        kk3 = kk.reshape(n, c, LANES)
        bk3 = b3 - log_kk.reshape(n, c, LANES)
        q3 = q.reshape(n, c, LANES)
        v3 = v.reshape(n, c, LANES)
        tot3 = b3[:, c - 1:c, :]
        q_dec = (q * jnp.exp2(b)).astype(BF16)
        k_dec = (kk3 * jnp.exp2(tot3 - b3)).reshape(rows_total, LANES).astype(BF16)
        decay = jnp.exp2(tot3)
        v_bf = v.astype(BF16)

        updates = [lax.dot_general(v_bf[i * c:(i + 1) * c], k_dec[i * c:(i + 1) * c], (((0,), (0,)), ((), ())),
                                   preferred_element_type=F32) for i in range(n)]
        state_t = state_ref[hh]
        states = []
        for i in range(n):
            states.append(state_t.astype(BF16))
            state_t = state_t * decay[i] + updates[i]
        state_ref[hh] = state_t
        outs = [lax.dot_general(q_dec[i * c:(i + 1) * c], states[i], (((1,), (1,)), ((), ())),
                                preferred_element_type=F32) for i in range(n)]

        b_lo, b_hi = b3[:, :half, :], b3[:, half:, :]
        q_lo, q_hi = q3[:, :half, :], q3[:, half:, :]
        o_lo = jnp.zeros((n, half, LANES), F32)
        o_hi = jnp.zeros((n, half, LANES), F32)
        for s in range(c):
            bs = bk3[:, s:s + 1, :]
            vs = v3[:, s:s + 1, :]
            if s < half:
                d = jnp.where(t_half >= s, b_lo - bs, -jnp.inf)
                a = jnp.sum(q_lo * jnp.exp2(d), axis=-1, keepdims=True)
                o_lo = o_lo + a * vs
                d = b_hi - bs
            else:
                d = jnp.where(t_half >= s - half, b_hi - bs, -jnp.inf)
            a = jnp.sum(q_hi * jnp.exp2(d), axis=-1, keepdims=True)
            o_hi = o_hi + a * vs
        o = jnp.concatenate([o_lo, o_hi], axis=1).reshape(rows_total, LANES) + jnp.concatenate(outs, axis=0)

        g = g_ref[0, :, cols]
        o = o * lax.rsqrt(jnp.mean(o * o, axis=-1, keepdims=True) + EPS) * nw_ref[hh]
        o_ref[0, :, cols] = (o * (g * _sigmoid(g))).astype(o_ref.dtype)


def hgrn2(z_b, col0, lower_bound, norm_w):
    b, s, _ = z_b.shape
    hp = HGRN_HEADS_PER_STEP
    width = hp * LANES
    c0 = col0 // width
    groups = HGRN_HEADS // hp

    def spec(part):
        return pl.BlockSpec((1, HGRN_ROWS, width), lambda bi, h, i: (bi, i, c0 + part * groups + h))

    vec = pl.BlockSpec((hp, 1, LANES), lambda bi, h, i: (h, 0, 0))
    return pl.pallas_call(
        _hgrn_kernel,
        grid=(b, groups, s // HGRN_ROWS),
        in_specs=[spec(0), spec(1), spec(2), spec(3), vec, vec],
        out_specs=pl.BlockSpec((1, HGRN_ROWS, width), lambda bi, h, i: (bi, i, h)),
        out_shape=jax.ShapeDtypeStruct((b, s, BRANCH_WIDTH), BF16),
        scratch_shapes=[pltpu.VMEM((hp, HGRN_VAL_DIM, HGRN_KEY_DIM), F32)],
        compiler_params=_params(("parallel", "parallel", "arbitrary")),
        name="hgrn2",
    )(z_b, z_b, z_b, z_b, lower_bound.reshape(HGRN_HEADS, 1, LANES), norm_w.reshape(HGRN_HEADS, 1, LANES))


def _bf16_split2(x):
    hi = x.astype(BF16)
    return hi, (x - hi.astype(F32)).astype(BF16)


def _router_kernel(x_ref, w_ref, wr_ref, br_ref, n_ref, meta_ref, p_ref, cnt_ref, carry_ref):
    @pl.when(pl.program_id(0) == 0)
    def _():
        carry_ref[...] = jnp.zeros_like(carry_ref)

    n = _rms(x_ref[...], w_ref[...])
    n_ref[...] = n
    n_hi, n_lo = _bf16_split2(n)
    w_hi, w_lo = _bf16_split2(wr_ref[...])
    logits = (jnp.dot(n_hi, w_hi, preferred_element_type=F32) + jnp.dot(n_lo, w_hi, preferred_element_type=F32)
              + jnp.dot(n_hi, w_lo, preferred_element_type=F32)) + br_ref[...]
    rows = logits.shape[0]
    col = lax.broadcasted_iota(jnp.int32, logits.shape, 1)
    big = jnp.int32(LANES)
    neg = -jnp.inf

    def first_argmax(vals):
        top = jnp.max(vals, axis=-1, keepdims=True)
        return top, jnp.min(jnp.where(vals == top, col, big), axis=-1, keepdims=True)

    group_logits = jnp.where(col < N_GROUPS, logits, neg)
    g_top, g_idx = first_argmax(group_logits)
    p_group = 1.0 / jnp.sum(jnp.exp(group_logits - g_top), axis=-1, keepdims=True)
    lo = N_GROUPS + g_idx * EXPERTS_PER_GROUP
    in_group = jnp.where((col >= lo) & (col < lo + EXPERTS_PER_GROUP), logits, neg)
    top1, idx1 = first_argmax(in_group)
    top2, idx2 = first_argmax(jnp.where(col == idx1, neg, in_group))
    r = jnp.exp(top2 - top1)
    p1 = p_group / (1.0 + r)
    p2 = p_group * r / (1.0 + r)
    p_ref[...] = jnp.where(col == 0, p1, jnp.where(col == 1, p2, 0.0))

    chosen = ((col == idx1) | (col == idx2)).astype(BF16)
    earlier = (lax.broadcasted_iota(jnp.int32, (rows, rows), 0)
               > lax.broadcasted_iota(jnp.int32, (rows, rows), 1)).astype(BF16)
    before = jnp.dot(earlier, chosen, preferred_element_type=F32) + carry_ref[...]
    rank1 = jnp.sum(jnp.where(col == idx1, before, 0.0), axis=-1, keepdims=True).astype(jnp.int32)
    rank2 = jnp.sum(jnp.where(col == idx2, before, 0.0), axis=-1, keepdims=True).astype(jnp.int32)
    carry_ref[...] += jnp.sum(chosen.astype(F32), axis=0, keepdims=True)
    cnt_ref[...] = carry_ref[...]
    meta_ref[...] = jnp.where(col == 0, idx1 - N_GROUPS, jnp.where(col == 1, idx2 - N_GROUPS,
                              jnp.where(col == 2, rank1, jnp.where(col == 3, rank2, 0))))


def moe_router(h, norm_w, w_router, b_router):
    t, d = h.shape
    row = pl.BlockSpec((NORM_ROWS, d), lambda i: (i, 0))
    small = pl.BlockSpec((NORM_ROWS, LANES), lambda i: (i, 0))
    one = pl.BlockSpec((1, LANES), lambda i: (0, 0))
    return pl.pallas_call(
        _router_kernel,
        grid=(t // NORM_ROWS,),
        in_specs=[row, pl.BlockSpec((1, d), lambda i: (0, 0)), pl.BlockSpec((d, LANES), lambda i: (0, 0)), one],
        out_specs=[row, small, small, one],
        out_shape=[jax.ShapeDtypeStruct((t, d), F32), jax.ShapeDtypeStruct((t, LANES), jnp.int32),
                   jax.ShapeDtypeStruct((t, LANES), F32), jax.ShapeDtypeStruct((1, LANES), F32)],
        scratch_shapes=[pltpu.VMEM((1, LANES), F32)],
        compiler_params=_params(("arbitrary",)),
        name="moe_router",
    )(h, norm_w.reshape(1, d), w_router, b_router)


def _gather_rows(idx_ref, src_hbm, dst_ref, sem, unrolled):
    n_rows = dst_ref.shape[0]

    def start(r, priority):
        pltpu.make_async_copy(src_hbm.at[pl.ds(idx_ref[r], 1)], dst_ref.at[pl.ds(r, 1)], sem).start(priority=priority)

    if unrolled:
        for r in range(n_rows):
            start(r, r % 2)
    else:
        def body(r, carry):
            start(r, 0)
            return carry
        lax.fori_loop(0, n_rows, body, 0)


def _wait_rows(src_hbm, dst_ref, sem):
    pltpu.make_async_copy(src_hbm.at[pl.ds(0, dst_ref.shape[0])], dst_ref, sem).wait()


def _expert_kernel(be_ref, nused_ref, tok_cur_ref, tok_next_ref, x_hbm, wg_ref, wu_ref, wd_ref, y_ref,
                   xbuf_ref, sem_ref):
    i = pl.program_id(0)
    last = pl.num_programs(0) - 1
    n_used = nused_ref[0]
    slot = i % 2

    @pl.when((i == 0) & (n_used > 0))
    def _():
        _gather_rows(tok_cur_ref, x_hbm, xbuf_ref.at[0], sem_ref.at[0], unrolled=False)

    @pl.when(i < n_used)
    def _():
        _gather_rows(tok_next_ref, x_hbm, xbuf_ref.at[1 - slot], sem_ref.at[1 - slot], unrolled=True)
        _wait_rows(x_hbm, xbuf_ref.at[slot], sem_ref.at[slot])
        x = xbuf_ref[slot].astype(BF16)
        gate = jnp.dot(x, wg_ref[0], preferred_element_type=F32)
        up = jnp.dot(x, wu_ref[0], preferred_element_type=F32)
        hidden = (gate * _sigmoid(gate) * up).astype(BF16)
        y_ref[...] = jnp.dot(hidden, wd_ref[0], preferred_element_type=F32)

    @pl.when((i == n_used) & (n_used > 0))
    def _():
        _wait_rows(x_hbm, xbuf_ref.at[slot], sem_ref.at[slot])

    @pl.when((i == last) & (n_used > last))
    def _():
        _wait_rows(x_hbm, xbuf_ref.at[1 - slot], sem_ref.at[1 - slot])

    @pl.when(i >= n_used)
    def _():
        y_ref[...] = jnp.zeros_like(y_ref)


def expert_mlp(x, buf_tok, block_e, n_used, w_gate, w_up, w_down):
    p = buf_tok.shape[0]
    nb = p // MOE_ROWS
    d = x.shape[1]
    smem = pltpu.SMEM
    grid_spec = pltpu.PrefetchScalarGridSpec(
        num_scalar_prefetch=2,
        grid=(nb,),
        in_specs=[pl.BlockSpec((MOE_ROWS,), lambda i, be, nu: (i,), memory_space=smem),
                  pl.BlockSpec((MOE_ROWS,), lambda i, be, nu: (jnp.minimum(i + 1, nb - 1),), memory_space=smem),
                  pl.BlockSpec(memory_space=pl.ANY),
                  pl.BlockSpec((1, d, EXPERT_HIDDEN), lambda i, be, nu: (be[i], 0, 0)),
                  pl.BlockSpec((1, d, EXPERT_HIDDEN), lambda i, be, nu: (be[i], 0, 0)),
                  pl.BlockSpec((1, EXPERT_HIDDEN, d), lambda i, be, nu: (be[i], 0, 0))],
        out_specs=pl.BlockSpec((MOE_ROWS, d), lambda i, be, nu: (i, 0)),
        scratch_shapes=[pltpu.VMEM((2, MOE_ROWS, d), F32), pltpu.SemaphoreType.DMA((2,))],
    )
    return pl.pallas_call(
        _expert_kernel,
        grid_spec=grid_spec,
        out_shape=jax.ShapeDtypeStruct((p, d), F32),
        compiler_params=_params(("arbitrary",)),
        name="expert_mlp",
    )(block_e, n_used, buf_tok, buf_tok, x, w_gate, w_up, w_down)


def _combine_kernel(pos0_ref, pos1_ref, pos0_next_ref, pos1_next_ref, y_hbm, h_ref, p_ref, w_ref, o_ref, n_ref,
                    ybuf_ref, sem_ref):
    i = pl.program_id(0)
    last = pl.num_programs(0) - 1
    slot = i % 2

    @pl.when(i == 0)
    def _():
        _gather_rows(pos0_ref, y_hbm, ybuf_ref.at[0, 0], sem_ref.at[0, 0], unrolled=False)
        _gather_rows(pos1_ref, y_hbm, ybuf_ref.at[0, 1], sem_ref.at[0, 1], unrolled=False)

    _gather_rows(pos0_next_ref, y_hbm, ybuf_ref.at[1 - slot, 0], sem_ref.at[1 - slot, 0], unrolled=True)
    _gather_rows(pos1_next_ref, y_hbm, ybuf_ref.at[1 - slot, 1], sem_ref.at[1 - slot, 1], unrolled=True)
    _wait_rows(y_hbm, ybuf_ref.at[slot, 0], sem_ref.at[slot, 0])
    _wait_rows(y_hbm, ybuf_ref.at[slot, 1], sem_ref.at[slot, 1])
    p = p_ref[...]
    h = h_ref[...] + (ybuf_ref[slot, 0] * p[:, 0:1] + ybuf_ref[slot, 1] * p[:, 1:2])
    o_ref[...] = h
    n_ref[...] = _rms(h, w_ref[...]).astype(n_ref.dtype)

    @pl.when(i == last)
    def _():
        _wait_rows(y_hbm, ybuf_ref.at[1 - slot, 0], sem_ref.at[1 - slot, 0])
        _wait_rows(y_hbm, ybuf_ref.at[1 - slot, 1], sem_ref.at[1 - slot, 1])


def moe_combine(y, pos0, pos1, probs, h, norm_w):
    t, d = h.shape
    nb = t // COMBINE_ROWS
    smem = pltpu.SMEM
    row = pl.BlockSpec((COMBINE_ROWS, d), lambda i: (i, 0))
    cur = pl.BlockSpec((COMBINE_ROWS,), lambda i: (i,), memory_space=smem)
    nxt = pl.BlockSpec((COMBINE_ROWS,), lambda i: (jnp.minimum(i + 1, nb - 1),), memory_space=smem)
    return pl.pallas_call(
        _combine_kernel,
        grid=(nb,),
        in_specs=[cur, cur, nxt, nxt,
                  pl.BlockSpec(memory_space=pl.ANY),
                  row,
                  pl.BlockSpec((COMBINE_ROWS, LANES), lambda i: (i, 0)),
                  pl.BlockSpec((1, d), lambda i: (0, 0))],
        out_specs=[row, row],
        out_shape=[jax.ShapeDtypeStruct((t, d), F32), jax.ShapeDtypeStruct((t, d), BF16)],
        scratch_shapes=[pltpu.VMEM((2, TOP_K, COMBINE_ROWS, d), F32), pltpu.SemaphoreType.DMA((2, TOP_K))],
        compiler_params=_params(("arbitrary",)),
        name="moe_combine",
    )(pos0, pos1, pos0, pos1, y, h, probs, norm_w.reshape(1, d))


def _dispatch_plan(meta, counts):
    t = meta.shape[0]
    a = t * TOP_K
    expert_ids = meta[:, :TOP_K]
    ranks = meta[:, TOP_K:2 * TOP_K]
    counts = counts[0, N_GROUPS:N_GROUPS + N_EXPERTS].astype(jnp.int32)
    padded = (counts + MOE_ROWS - 1) // MOE_ROWS * MOE_ROWS
    pad_end = jnp.cumsum(padded)
    pad_start = pad_end - padded
    dest = pad_start[expert_ids] + ranks
    p = a + N_EXPERTS * MOE_ROWS
    nb = p // MOE_ROWS
    token_ids = jnp.repeat(jnp.arange(t, dtype=jnp.int32), TOP_K)
    buf_tok = jnp.zeros((p,), jnp.int32).at[dest.reshape(a)].set(token_ids, unique_indices=True,
                                                                mode="promise_in_bounds")
    block_start = jnp.arange(nb, dtype=jnp.int32) * MOE_ROWS
    block_e = jnp.minimum(jnp.sum((pad_end[None, :] <= block_start[:, None]).astype(jnp.int32), axis=1),
                          N_EXPERTS - 1)
    n_used = (pad_end[-1:] // MOE_ROWS).astype(jnp.int32)
    return buf_tok, block_e, n_used, dest[:, 0], dest[:, 1]


def _pad_cols(w, n):
    return jnp.pad(w, ((0, 0), (0, n - w.shape[1])))


def _router_weights(i, router_group_w, router_group_b, router_expert_w, router_expert_b):
    w_r = _pad_cols(jnp.concatenate([router_group_w[i], router_expert_w[i]], axis=1), LANES)
    b_r = _pad_cols(jnp.concatenate([router_group_b[i], router_expert_b[i]])[None, :], LANES)
    return w_r, b_r


def kernel(x, p, mix_norm_w, w_in, fox_f_bias, pool_w, pool_scale, sgu_norm_w, sgu_w, sgu_b,
           hgrn_lb_logits, hgrn_norm_w, branch_proj, w_out, ffn_norm_w, router_group_w,
           router_group_b, router_expert_w, router_expert_b, expert_w_gate, expert_w_up,
           expert_w_down, ple_norm_w, ple_gate_w, ple_proj_w, final_norm_w):
    bsz, seq, d = x.shape
    t = bsz * seq
    lb_p = jax.nn.softmax(hgrn_lb_logits.astype(F32), axis=0)
    lower_bounds = jnp.cumsum(lb_p, axis=0) - lb_p[0:1]
    sgu_u_col = FOX_QKV_END
    gate_col = FOX_QKV_END + 2 * BRANCH_WIDTH
    h = x.reshape(t, d)
    w_in_t = jnp.transpose(w_in, (0, 2, 1))

    def tiles(*ranges):
        return [c for lo, hi in ranges for c in range(lo, hi, MM_COLS)]

    starts_a = tiles((0, FOX_QKV_END), (POOL_END, SGU_END), (HGRN_END, IN_COLS))
    starts_b = tiles((FOX_F_END, POOL_END), (SGU_END, HGRN_END))
    n_a = len(starts_a) * MM_COLS
    n_b = len(starts_b) * MM_COLS
    q_scale = jnp.where(jnp.arange(n_a) < BRANCH_WIDTH, FOX_HEAD_DIM ** -0.5 * LOG2E, 1.0).astype(F32)[None, :]
    pool_w_bf, branch_proj_bf, w_out_bf, ple_gate_bf, ple_proj_bf = (
        w.astype(BF16) for w in (pool_w, branch_proj, w_out, ple_gate_w, ple_proj_w))
    p_bf = p.reshape(DEPTH, t, PLE_DIM).astype(BF16)
    for i in range(DEPTH):
        w_r, b_r = _router_weights(i, router_group_w, router_group_b, router_expert_w, router_expert_b)
        n, z_f = rms_norm_proj(h, mix_norm_w[i], w_in_t, i, FOX_QKV_END, BF16)
        z_a, (w_gate_bf, w_up_bf) = in_projection(n, w_in_t, i, starts_a, q_scale, BF16,
                                                  cast=((expert_w_gate, CAST_ROWS), (expert_w_up, CAST_ROWS)))
        z_b, (w_down_bf,) = in_projection(n, w_in_t, i, starts_b, jnp.ones((1, n_b), F32), F32,
                                          cast=((expert_w_down, CAST_ROWS),))
        z_a3 = z_a.reshape(bsz, seq, -1)
        z_b3 = z_b.reshape(bsz, seq, -1)

        f_bias = _pad_cols(fox_f_bias[i][None, :].astype(F32), LANES)
        eq, ek = forget_prep(z_f.reshape(bsz, seq, LANES), f_bias)
        br_fox = fox_attention(z_a3, eq, ek).reshape(t, BRANCH_WIDTH)
        br_pool = multiscale_pool(z_b3, pool_w_bf, i, pool_scale[i]).reshape(t, BRANCH_WIDTH)
        br_sgu = spatial_gating(z_a, sgu_u_col, sgu_norm_w[i], sgu_w[i], sgu_b[i])
        br_hgrn = hgrn2(z_b3, BRANCH_WIDTH, lower_bounds[i], hgrn_norm_w[i]).reshape(t, BRANCH_WIDTH)

        merged = merge_branches((br_fox, br_pool, br_sgu, br_hgrn), branch_proj_bf, i, z_a, gate_col)
        h = matmul_residual(merged, w_out_bf, i, h)

        xn, meta, probs, counts = moe_router(h, ffn_norm_w[i], w_r, b_r)
        buf_tok, block_e, n_used, pos0, pos1 = _dispatch_plan(meta, counts)
        y = expert_mlp(xn, buf_tok, block_e, n_used, w_gate_bf, w_up_bf, w_down_bf)
        h, n_ple = moe_combine(y, pos0, pos1, probs, h, ple_norm_w[i])

        h = ple_update(n_ple, ple_gate_bf, p_bf, ple_proj_bf, i, h)
    return rms_norm(h, final_norm_w, F32).reshape(bsz, seq, d)
```

```python
import functools

import jax
import jax.numpy as jnp
from jax import lax
from jax.experimental import pallas as pl
from jax.experimental.pallas import tpu as pltpu

F32 = jnp.float32
BF16 = jnp.bfloat16
HIGHEST = lax.Precision.HIGHEST

D_MODEL = 2048
DEPTH = 2
BRANCH_WIDTH = D_MODEL // 2
N_BRANCHES = 4
PLE_DIM = 256
EPS = 1e-6
FOX_HEAD_DIM = 128
FOX_HEADS = BRANCH_WIDTH // FOX_HEAD_DIM
POOL_WINDOWS = (2, 4, 8, 16)
POOL_GROUP = BRANCH_WIDTH // len(POOL_WINDOWS)
POOL_HALO = 16
SGU_GROUPS = 4
SGU_CHUNK = 128
SGU_GROUP_WIDTH = BRANCH_WIDTH // SGU_GROUPS
HGRN_KEY_DIM = 128
HGRN_VAL_DIM = 128
HGRN_HEADS = BRANCH_WIDTH // HGRN_VAL_DIM
HGRN_SUBCHUNK = 16
HGRN_HEADS_PER_STEP = 2
N_GROUPS = 4
EXPERTS_PER_GROUP = 8
N_EXPERTS = N_GROUPS * EXPERTS_PER_GROUP
TOP_K = 2
EXPERT_HIDDEN = D_MODEL // 2
FOX_QKV_END = 3 * BRANCH_WIDTH
FOX_F_END = FOX_QKV_END + FOX_HEADS
POOL_END = FOX_F_END + BRANCH_WIDTH
SGU_END = POOL_END + 2 * BRANCH_WIDTH
HGRN_END = SGU_END + 4 * BRANCH_WIDTH
IN_COLS = HGRN_END + N_BRANCHES * D_MODEL

LANES = 128
SUBLANES = 8
LOG2E = 1.4426950408889634
FOX_HEADS_PER_STEP = 8
FOX_QUERY_CHUNK = 256
VMEM_LIMIT = 56 * 1024 * 1024

NORM_ROWS = 512
MM_ROWS = 2048
MM_COLS = 512
CAST_ROWS = 512
MERGE_ROWS = 1024
ATT_BLOCK = 512
CUMSUM_BLOCK = 512
POOL_ROWS = 512
SGU_ROWS = 512
HGRN_ROWS = 512
MOE_ROWS = 256
COMBINE_ROWS = 256
PLE_ROWS = 2048


def _params(semantics, **kw):
    return pltpu.CompilerParams(dimension_semantics=semantics, vmem_limit_bytes=VMEM_LIMIT, **kw)


def _sigmoid(x):
    return 1.0 / (1.0 + jnp.exp(-x))


def _log_sigmoid(x):
    return jnp.minimum(x, 0.0) - jnp.log(1.0 + jnp.exp(-jnp.abs(x)))


def _rms(x, w):
    return x * lax.rsqrt(jnp.mean(x * x, axis=-1, keepdims=True) + EPS) * w


def _norm_kernel(x_ref, w_ref, n_ref):
    n_ref[...] = _rms(x_ref[...], w_ref[...]).astype(n_ref.dtype)


def rms_norm(x, w, out_dtype):
    t, d = x.shape
    return pl.pallas_call(
        _norm_kernel,
        grid=(t // NORM_ROWS,),
        in_specs=[pl.BlockSpec((NORM_ROWS, d), lambda i: (i, 0)),
                  pl.BlockSpec((1, d), lambda i: (0, 0))],
        out_specs=pl.BlockSpec((NORM_ROWS, d), lambda i: (i, 0)),
        out_shape=jax.ShapeDtypeStruct((t, d), out_dtype),
        compiler_params=_params(("parallel",)),
        name="rms_norm",
    )(x, w.reshape(1, d))


def _norm_proj_kernel(x_ref, w_ref, wpt_ref, n_ref, p_ref):
    n = _rms(x_ref[...], w_ref[...]).astype(BF16)
    n_ref[...] = n.astype(n_ref.dtype)
    p_ref[...] = lax.dot_general(n, wpt_ref[0].astype(BF16), (((1,), (1,)), ((), ())), preferred_element_type=F32)


def rms_norm_proj(x, w, wt, layer, row0, out_dtype):
    t, d = x.shape
    jp = row0 // LANES
    return pl.pallas_call(
        _norm_proj_kernel,
        grid=(t // NORM_ROWS,),
        in_specs=[pl.BlockSpec((NORM_ROWS, d), lambda i: (i, 0)),
                  pl.BlockSpec((1, d), lambda i: (0, 0)),
                  pl.BlockSpec((1, LANES, d), lambda i: (layer, jp, 0))],
        out_specs=[pl.BlockSpec((NORM_ROWS, d), lambda i: (i, 0)),
                   pl.BlockSpec((NORM_ROWS, LANES), lambda i: (i, 0))],
        out_shape=[jax.ShapeDtypeStruct((t, d), out_dtype),
                   jax.ShapeDtypeStruct((t, LANES), F32)],
        compiler_params=_params(("parallel",)),
        name="rms_norm_proj",
    )(x, w.reshape(1, d), wt)


def _in_proj_kernel(start_tiles_ref, a_ref, wt_ref, s_ref, *rest, n_cast):
    cast_in, o_ref, cast_out = rest[:n_cast], rest[n_cast], rest[n_cast + 1:]
    w = wt_ref[0].astype(BF16)
    acc = lax.dot_general(a_ref[...], w, (((1,), (1,)), ((), ())), preferred_element_type=F32)
    o_ref[...] = (acc * s_ref[...]).astype(o_ref.dtype)
    for src_ref, dst_ref in zip(cast_in, cast_out):
        dst_ref[...] = src_ref[0].astype(dst_ref.dtype)


def in_projection(a, wt, layer, starts, col_scale, out_dtype, cast=()):
    m, k = a.shape
    assert all(c % SUBLANES == 0 for c in starts)
    n = len(starts) * MM_COLS
    nj = len(starts)
    n_steps = (m // MM_ROWS) * nj
    cast_specs_in, cast_specs_out, cast_shapes = [], [], []
    for arr, rows in cast:
        _, e, r, c = arr.shape
        per = r // rows
        n_blocks = e * per
        assert n_blocks <= n_steps

        def block(i, j, st, per=per, n_blocks=n_blocks):
            kk = jnp.minimum(i * nj + j, n_blocks - 1)
            return kk // per, kk % per

        cast_specs_in.append(pl.BlockSpec((1, 1, rows, c), lambda i, j, st, block=block: (layer, *block(i, j, st), 0)))
        cast_specs_out.append(pl.BlockSpec((1, rows, c), lambda i, j, st, block=block: (*block(i, j, st), 0)))
        cast_shapes.append(jax.ShapeDtypeStruct((e, r, c), BF16))
    grid_spec = pltpu.PrefetchScalarGridSpec(
        num_scalar_prefetch=1,
        grid=(m // MM_ROWS, nj),
        in_specs=[pl.BlockSpec((MM_ROWS, k), lambda i, j, st: (i, 0)),
                  pl.BlockSpec((pl.Element(1), pl.Element(MM_COLS), pl.Element(k)),
                               lambda i, j, st: (layer, st[j] * SUBLANES, 0)),
                  pl.BlockSpec((1, MM_COLS), lambda i, j, st: (0, j))] + cast_specs_in,
        out_specs=[pl.BlockSpec((MM_ROWS, MM_COLS), lambda i, j, st: (i, j))] + cast_specs_out,
    )
    outs = pl.pallas_call(
        functools.partial(_in_proj_kernel, n_cast=len(cast)),
        grid_spec=grid_spec,
        out_shape=[jax.ShapeDtypeStruct((m, n), out_dtype)] + cast_shapes,
        compiler_params=_params(("arbitrary", "arbitrary")),
        name="in_projection",
    )(jnp.asarray([c // SUBLANES for c in starts], jnp.int32), a, wt, col_scale, *[arr for arr, _ in cast])
    return outs[0], outs[1:]


def _mm_res_kernel(a_ref, w_ref, r_ref, o_ref):
    o_ref[...] = r_ref[...] + jnp.dot(a_ref[...], w_ref[0], preferred_element_type=F32)


def matmul_residual(a, w, layer, res):
    m, k = a.shape
    n = w.shape[2]
    return pl.pallas_call(
        _mm_res_kernel,
        grid=(m // MM_ROWS, n // MM_COLS),
        in_specs=[pl.BlockSpec((MM_ROWS, k), lambda i, j: (i, 0)),
                  pl.BlockSpec((1, k, MM_COLS), lambda i, j: (layer, 0, j)),
                  pl.BlockSpec((MM_ROWS, MM_COLS), lambda i, j: (i, j))],
        out_specs=pl.BlockSpec((MM_ROWS, MM_COLS), lambda i, j: (i, j)),
        out_shape=jax.ShapeDtypeStruct((m, n), F32),
        compiler_params=_params(("parallel", "arbitrary")),
        name="matmul_residual",
    )(a, w, res)


def _merge_kernel(b0_ref, b1_ref, b2_ref, b3_ref, p_ref, g0_ref, g1_ref, g2_ref, g3_ref, o_ref):
    acc = None
    for bi, (b_ref, g_ref) in enumerate(((b0_ref, g0_ref), (b1_ref, g1_ref),
                                         (b2_ref, g2_ref), (b3_ref, g3_ref))):
        y = jnp.dot(b_ref[...], p_ref[0, bi], preferred_element_type=F32)
        y = _sigmoid(g_ref[...].astype(F32)) * y
        acc = y if acc is None else acc + y
    o_ref[...] = acc.astype(o_ref.dtype)


def merge_branches(branches, proj, layer, z_a, gate_col0):
    t = branches[0].shape[0]
    g0 = gate_col0 // MM_COLS
    per = D_MODEL // MM_COLS
    b_spec = pl.BlockSpec((MERGE_ROWS, BRANCH_WIDTH), lambda i, j: (i, 0))
    g_specs = [pl.BlockSpec((MERGE_ROWS, MM_COLS), functools.partial(lambda i, j, o: (i, o + j), o=g0 + bi * per))
               for bi in range(N_BRANCHES)]
    return pl.pallas_call(
        _merge_kernel,
        grid=(t // MERGE_ROWS, per),
        in_specs=[b_spec] * N_BRANCHES
        + [pl.BlockSpec((1, N_BRANCHES, BRANCH_WIDTH, MM_COLS), lambda i, j: (layer, 0, 0, j))] + g_specs,
        out_specs=pl.BlockSpec((MERGE_ROWS, MM_COLS), lambda i, j: (i, j)),
        out_shape=jax.ShapeDtypeStruct((t, D_MODEL), BF16),
        compiler_params=_params(("parallel", "arbitrary")),
        name="merge_branches",
    )(*branches, proj, z_a, z_a, z_a, z_a)


def _ple_kernel(n_ref, wg_ref, p_ref, wp_ref, h_ref, o_ref):
    gate = _sigmoid(jnp.dot(n_ref[...], wg_ref[0], preferred_element_type=F32))
    emb = jnp.dot(p_ref[0], wp_ref[0], preferred_element_type=F32)
    o_ref[...] = h_ref[...] + gate * emb


def ple_update(n, wg, p, wp, layer, h):
    t = h.shape[0]
    return pl.pallas_call(
        _ple_kernel,
        grid=(t // PLE_ROWS, D_MODEL // MM_COLS),
        in_specs=[pl.BlockSpec((PLE_ROWS, D_MODEL), lambda i, j: (i, 0)),
                  pl.BlockSpec((1, D_MODEL, MM_COLS), lambda i, j: (layer, 0, j)),
                  pl.BlockSpec((1, PLE_ROWS, PLE_DIM), lambda i, j: (layer, i, 0)),
                  pl.BlockSpec((1, PLE_DIM, MM_COLS), lambda i, j: (layer, 0, j)),
                  pl.BlockSpec((PLE_ROWS, MM_COLS), lambda i, j: (i, j))],
        out_specs=pl.BlockSpec((PLE_ROWS, MM_COLS), lambda i, j: (i, j)),
        out_shape=jax.ShapeDtypeStruct((t, D_MODEL), F32),
        compiler_params=_params(("parallel", "arbitrary")),
        name="ple_update",
    )(n, wg, p, wp, h)


def _bf16_split3(x):
    hi = x.astype(BF16).astype(F32)
    r = x - hi
    mid = r.astype(BF16).astype(F32)
    lo = (r - mid).astype(BF16).astype(F32)
    return hi, mid, lo


def _forget_prep_kernel(z_ref, b_ref, eq_ref, ek_ref, carry_ref):
    @pl.when(pl.program_id(1) == 0)
    def _():
        carry_ref[...] = jnp.zeros_like(carry_ref)

    lf = _log_sigmoid(z_ref[0] + b_ref[...]) * LOG2E
    n = lf.shape[0]
    tril = (lax.broadcasted_iota(jnp.int32, (n, n), 0) >= lax.broadcasted_iota(jnp.int32, (n, n), 1)).astype(F32)
    c = jnp.dot(tril, lf, precision=HIGHEST, preferred_element_type=F32) + carry_ref[...]
    carry_ref[...] = c[n - 1:n, :]
    parts = _bf16_split3(c)
    lane = lax.broadcasted_iota(jnp.int32, (n, LANES), 1)
    for h in range(FOX_HEADS):
        hi, mid, lo = (jnp.sum(jnp.where(lane == h, part, 0.0), axis=-1, keepdims=True) for part in parts)
        eq = jnp.where(lane == 0, hi, jnp.where(lane == 1, mid, jnp.where(lane == 2, lo,
                       jnp.where(lane < 6, 1.0, 0.0))))
        ek = jnp.where(lane < 3, 1.0, jnp.where(lane == 3, -hi, jnp.where(lane == 4, -mid,
                       jnp.where(lane == 5, -lo, 0.0))))
        eq_ref[0, h] = eq.astype(BF16)
        ek_ref[0, h] = ek.astype(BF16)


def forget_prep(z_f, bias):
    b, s, _ = z_f.shape
    out = pl.BlockSpec((1, FOX_HEADS, CUMSUM_BLOCK, LANES), lambda bi, i: (bi, 0, i, 0))
    shape = jax.ShapeDtypeStruct((b, FOX_HEADS, s, LANES), BF16)
    return pl.pallas_call(
        _forget_prep_kernel,
        grid=(b, s // CUMSUM_BLOCK),
        in_specs=[pl.BlockSpec((1, CUMSUM_BLOCK, LANES), lambda bi, i: (bi, i, 0)),
                  pl.BlockSpec((1, LANES), lambda bi, i: (0, 0))],
        out_specs=[out, out],
        out_shape=[shape, shape],
        scratch_shapes=[pltpu.VMEM((1, LANES), F32)],
        compiler_params=_params(("parallel", "arbitrary")),
        name="forget_prep",
    )(z_f, bias)


def _fox_kernel(qi_ref, ki_ref, q_ref, k_ref, v_ref, eq_ref, ek_ref, o_ref, m_ref, l_ref, acc_ref):
    step = pl.program_id(2)
    qi = qi_ref[step]
    ki = ki_ref[step]
    blk = ATT_BLOCK
    dh = FOX_HEAD_DIM
    qc = FOX_QUERY_CHUNK

    @pl.when(ki == 0)
    def _():
        m_ref[...] = jnp.full_like(m_ref, -jnp.inf)
        l_ref[...] = jnp.zeros_like(l_ref)
        acc_ref[...] = jnp.zeros_like(acc_ref)

    def update(masked):
        chains = [(hh, c) for hh in range(FOX_HEADS_PER_STEP) for c in range(blk // qc)]
        scores = []
        for hh, c in chains:
            cols = slice(hh * dh, (hh + 1) * dh)
            qs = slice(c * qc, (c + 1) * qc)
            k = jnp.concatenate([k_ref[0, :, cols], ek_ref[0, hh]], axis=1)
            q = jnp.concatenate([q_ref[0, qs, cols], eq_ref[0, hh, qs, :]], axis=1)
            s = lax.dot_general(k, q, (((1,), (1,)), ((), ())), preferred_element_type=F32)
            if masked:
                key = lax.broadcasted_iota(jnp.int32, (blk, qc), 0)
                qry = lax.broadcasted_iota(jnp.int32, (blk, qc), 1) + c * qc
                s = jnp.where(key <= qry, s, -jnp.inf)
            scores.append(s)
        probs = []
        for (hh, c), s in zip(chains, scores):
            qs = slice(c * qc, (c + 1) * qc)
            m_old = m_ref[hh, :, qs]
            m_new = jnp.maximum(m_old, jnp.max(s, axis=0, keepdims=True))
            alpha = jnp.exp2(m_old - m_new)
            p = jnp.exp2(s - m_new)
            l_ref[hh, :, qs] = alpha * l_ref[hh, :, qs] + jnp.sum(p, axis=0, keepdims=True)
            m_ref[hh, :, qs] = m_new
            probs.append((alpha, p.astype(BF16)))
        v_t = [jnp.transpose(v_ref[0, :, hh * dh:(hh + 1) * dh]) for hh in range(FOX_HEADS_PER_STEP)]
        for (hh, c), (alpha, p) in zip(chains, probs):
            qs = slice(c * qc, (c + 1) * qc)
            acc_ref[hh, :, qs] = alpha * acc_ref[hh, :, qs] + jnp.dot(v_t[hh], p,
                                                                      preferred_element_type=F32)

    @pl.when(ki < qi)
    def _():
        update(False)

    @pl.when(ki == qi)
    def _():
        update(True)
        for hh in range(FOX_HEADS_PER_STEP):
            o_ref[0, :, hh * dh:(hh + 1) * dh] = jnp.transpose(acc_ref[hh] / l_ref[hh]).astype(o_ref.dtype)


def fox_attention(z_a, eq, ek):
    b, s, _ = z_a.shape
    nb = s // ATT_BLOCK
    hp = FOX_HEADS_PER_STEP
    groups = FOX_HEADS // hp
    pairs = [(qi, ki) for qi in range(nb) for ki in range(qi + 1)]
    qi_tab = jnp.asarray([pr[0] for pr in pairs], jnp.int32)
    ki_tab = jnp.asarray([pr[1] for pr in pairs], jnp.int32)
    width = hp * FOX_HEAD_DIM
    grid_spec = pltpu.PrefetchScalarGridSpec(
        num_scalar_prefetch=2,
        grid=(b, groups, len(pairs)),
        in_specs=[pl.BlockSpec((1, ATT_BLOCK, width), lambda bi, g, st, qt, kt: (bi, qt[st], g)),
                  pl.BlockSpec((1, ATT_BLOCK, width), lambda bi, g, st, qt, kt: (bi, kt[st], groups + g)),
                  pl.BlockSpec((1, ATT_BLOCK, width), lambda bi, g, st, qt, kt: (bi, kt[st], 2 * groups + g)),
                  pl.BlockSpec((1, hp, ATT_BLOCK, LANES), lambda bi, g, st, qt, kt: (bi, g, qt[st], 0)),
                  pl.BlockSpec((1, hp, ATT_BLOCK, LANES), lambda bi, g, st, qt, kt: (bi, g, kt[st], 0))],
        out_specs=pl.BlockSpec((1, ATT_BLOCK, width), lambda bi, g, st, qt, kt: (bi, qt[st], g)),
        scratch_shapes=[pltpu.VMEM((hp, 1, ATT_BLOCK), F32), pltpu.VMEM((hp, 1, ATT_BLOCK), F32),
                        pltpu.VMEM((hp, FOX_HEAD_DIM, ATT_BLOCK), F32)],
    )
    return pl.pallas_call(
        _fox_kernel,
        grid_spec=grid_spec,
        out_shape=jax.ShapeDtypeStruct((b, s, BRANCH_WIDTH), BF16),
        compiler_params=_params(("parallel", "parallel", "arbitrary")),
        name="fox_attention",
    )(qi_tab, ki_tab, z_a, z_a, z_a, eq, ek)


def _pool_kernel(u_ref, halo_ref, w_ref, scale_ref, o_ref):
    i = pl.program_id(1)
    rows = POOL_ROWS
    t = i * rows + lax.broadcasted_iota(jnp.int32, (rows, 1), 0)
    keep_halo = (i > 0).astype(F32)
    for g, win in enumerate(POOL_WINDOWS):
        cols = slice(g * POOL_GROUP, (g + 1) * POOL_GROUP)
        u = u_ref[0, :, cols]
        ext = jnp.concatenate([halo_ref[0, :, cols] * keep_halo, u], axis=0)
        span = 1
        while span < win:
            n = ext.shape[0]
            ext = ext[:n - span] + ext[span:]
            span *= 2
        window_sum = ext[POOL_HALO + 1 - win:POOL_HALO + 1 - win + rows]
        count = jnp.minimum(t + 1, win).astype(F32)
        d = (window_sum / count - u).astype(BF16)
        y = jnp.dot(d, w_ref[0, g], preferred_element_type=F32) * scale_ref[:, cols]
        o_ref[0, :, cols] = y.astype(o_ref.dtype)


def multiscale_pool(z_b, pool_w, layer, pool_scale):
    b, s, _ = z_b.shape
    per = POOL_ROWS // POOL_HALO
    return pl.pallas_call(
        _pool_kernel,
        grid=(b, s // POOL_ROWS),
        in_specs=[pl.BlockSpec((1, POOL_ROWS, BRANCH_WIDTH), lambda bi, i: (bi, i, 0)),
                  pl.BlockSpec((1, POOL_HALO, BRANCH_WIDTH), lambda bi, i: (bi, jnp.maximum(i * per - 1, 0), 0)),
                  pl.BlockSpec((1, len(POOL_WINDOWS), POOL_GROUP, POOL_GROUP), lambda bi, i: (layer, 0, 0, 0)),
                  pl.BlockSpec((1, BRANCH_WIDTH), lambda bi, i: (0, 0))],
        out_specs=pl.BlockSpec((1, POOL_ROWS, BRANCH_WIDTH), lambda bi, i: (bi, i, 0)),
        out_shape=jax.ShapeDtypeStruct((b, s, BRANCH_WIDTH), BF16),
        compiler_params=_params(("parallel", "parallel")),
        name="multiscale_pool",
    )(z_b, z_b, pool_w, pool_scale.reshape(1, BRANCH_WIDTH))


def _sgu_kernel(u_ref, v_ref, nw_ref, w_ref, b_ref, o_ref):
    c = SGU_CHUNK
    v = jax.nn.gelu(v_ref[...].astype(F32))
    vc = v - jnp.mean(v, axis=-1, keepdims=True)
    vn = (vc * lax.rsqrt(jnp.mean(vc * vc, axis=-1, keepdims=True) + EPS) * nw_ref[...]).astype(BF16)
    causal = lax.broadcasted_iota(jnp.int32, (c, c), 0) >= lax.broadcasted_iota(jnp.int32, (c, c), 1)
    for g in range(SGU_GROUPS):
        cols = slice(g * SGU_GROUP_WIDTH, (g + 1) * SGU_GROUP_WIDTH)
        w = jnp.where(causal, w_ref[g], 0.0).astype(BF16)
        bias = b_ref[g]
        for n in range(SGU_ROWS // c):
            rows = slice(n * c, (n + 1) * c)
            sv = jnp.dot(w, vn[rows, cols], preferred_element_type=F32) + bias
            u = jax.nn.gelu(u_ref[rows, cols].astype(F32))
            o_ref[rows, cols] = (u * sv).astype(o_ref.dtype)


def spatial_gating(z_a, u_col0, norm_w, w_s, b_s):
    t = z_a.shape[0]
    ub = u_col0 // BRANCH_WIDTH
    return pl.pallas_call(
        _sgu_kernel,
        grid=(t // SGU_ROWS,),
        in_specs=[pl.BlockSpec((SGU_ROWS, BRANCH_WIDTH), lambda i: (i, ub)),
                  pl.BlockSpec((SGU_ROWS, BRANCH_WIDTH), lambda i: (i, ub + 1)),
                  pl.BlockSpec((1, BRANCH_WIDTH), lambda i: (0, 0)),
                  pl.BlockSpec((SGU_GROUPS, SGU_CHUNK, SGU_CHUNK), lambda i: (0, 0, 0)),
                  pl.BlockSpec((SGU_GROUPS, SGU_CHUNK, 1), lambda i: (0, 0, 0))],
        out_specs=pl.BlockSpec((SGU_ROWS, BRANCH_WIDTH), lambda i: (i, 0)),
        out_shape=jax.ShapeDtypeStruct((t, BRANCH_WIDTH), BF16),
        compiler_params=_params(("parallel",)),
        name="spatial_gating",
    )(z_a, z_a, norm_w.reshape(1, BRANCH_WIDTH), w_s, b_s.reshape(SGU_GROUPS, SGU_CHUNK, 1))


def _hgrn_kernel(q_ref, f_ref, v_ref, g_ref, lb_ref, nw_ref, o_ref, state_ref):
    c = HGRN_SUBCHUNK
    half = c // 2
    rows_total = HGRN_ROWS
    n = rows_total // c

    @pl.when(pl.program_id(2) == 0)
    def _():
        state_ref[...] = jnp.zeros_like(state_ref)

    local = lax.broadcasted_iota(jnp.int32, (rows_total, 1), 0) & (c - 1)
    t_half = lax.broadcasted_iota(jnp.int32, (1, half, 1), 1)
    for hh in range(HGRN_HEADS_PER_STEP):
        cols = slice(hh * LANES, (hh + 1) * LANES)
        lb = lb_ref[hh]
        q = q_ref[0, :, cols]
        fl = f_ref[0, :, cols]
        v = v_ref[0, :, cols]
        e = jnp.exp(-jnp.abs(fl))
        r = 1.0 / (1.0 + e)
        sig_pos = jnp.where(fl >= 0, r, e * r)
        sig_neg = jnp.where(fl >= 0, e * r, r)
        kk = (1.0 - lb) * sig_neg
        log_kk = jnp.log2(kk)
        b = jnp.log2(lb + (1.0 - lb) * sig_pos)
        shift = 1
        while shift < c:
            b = b + jnp.where(local >= shift, pltpu.roll(b, shift, 0), 0.0)
            shift *= 2
        b3 = b.reshape(n, c, LANES)
        kk3 = kk.reshape(n, c, LANES)
        bk3 = b3 - log_kk.reshape(n, c, LANES)
        q3 = q.reshape(n, c, LANES)
        v3 = v.reshape(n, c, LANES)
        tot3 = b3[:, c - 1:c, :]
        q_dec = (q * jnp.exp2(b)).astype(BF16)
        k_dec = (kk3 * jnp.exp2(tot3 - b3)).reshape(rows_total, LANES).astype(BF16)
        decay = jnp.exp2(tot3)
        v_bf = v.astype(BF16)

        updates = [lax.dot_general(v_bf[i * c:(i + 1) * c], k_dec[i * c:(i + 1) * c], (((0,), (0,)), ((), ())),
                                   preferred_element_type=F32) for i in range(n)]
        state_t = state_ref[hh]
        states = []
        for i in range(n):
            states.append(state_t.astype(BF16))
            state_t = state_t * decay[i] + updates[i]
        state_ref[hh] = state_t
        outs = [lax.dot_general(q_dec[i * c:(i + 1) * c], states[i], (((1,), (1,)), ((), ())),
                                preferred_element_type=F32) for i in range(n)]

        b_lo, b_hi = b3[:, :half, :], b3[:, half:, :]
        q_lo, q_hi = q3[:, :half, :], q3[:, half:, :]
        o_lo = jnp.zeros((n, half, LANES), F32)
        o_hi = jnp.zeros((n, half, LANES), F32)
        for s in range(c):
            bs = bk3[:, s:s + 1, :]
            vs = v3[:, s:s + 1, :]
            if s < half:
                d = jnp.where(t_half >= s, b_lo - bs, -jnp.inf)
                a = jnp.sum(q_lo * jnp.exp2(d), axis=-1, keepdims=True)
                o_lo = o_lo + a * vs
                d = b_hi - bs
            else:
                d = jnp.where(t_half >= s - half, b_hi - bs, -jnp.inf)
            a = jnp.sum(q_hi * jnp.exp2(d), axis=-1, keepdims=True)
            o_hi = o_hi + a * vs
        o = jnp.concatenate([o_lo, o_hi], axis=1).reshape(rows_total, LANES) + jnp.concatenate(outs, axis=0)

        g = g_ref[0, :, cols]
        o = o * lax.rsqrt(jnp.mean(o * o, axis=-1, keepdims=True) + EPS) * nw_ref[hh]
        o_ref[0, :, cols] = (o * (g * _sigmoid(g))).astype(o_ref.dtype)


def hgrn2(z_b, col0, lower_bound, norm_w):
    b, s, _ = z_b.shape
    hp = HGRN_HEADS_PER_STEP
    width = hp * LANES
    c0 = col0 // width
    groups = HGRN_HEADS // hp

    def spec(part):
        return pl.BlockSpec((1, HGRN_ROWS, width), lambda bi, h, i: (bi, i, c0 + part * groups + h))

    vec = pl.BlockSpec((hp, 1, LANES), lambda bi, h, i: (h, 0, 0))
    return pl.pallas_call(
        _hgrn_kernel,
        grid=(b, groups, s // HGRN_ROWS),
        in_specs=[spec(0), spec(1), spec(2), spec(3), vec, vec],
        out_specs=pl.BlockSpec((1, HGRN_ROWS, width), lambda bi, h, i: (bi, i, h)),
        out_shape=jax.ShapeDtypeStruct((b, s, BRANCH_WIDTH), BF16),
        scratch_shapes=[pltpu.VMEM((hp, HGRN_VAL_DIM, HGRN_KEY_DIM), F32)],
        compiler_params=_params(("parallel", "parallel", "arbitrary")),
        name="hgrn2",
    )(z_b, z_b, z_b, z_b, lower_bound.reshape(HGRN_HEADS, 1, LANES), norm_w.reshape(HGRN_HEADS, 1, LANES))


def _bf16_split2(x):
    hi = x.astype(BF16)
    return hi, (x - hi.astype(F32)).astype(BF16)


def _router_kernel(x_ref, w_ref, wr_ref, br_ref, n_ref, meta_ref, p_ref, cnt_ref, carry_ref):
    @pl.when(pl.program_id(0) == 0)
    def _():
        carry_ref[...] = jnp.zeros_like(carry_ref)

    n = _rms(x_ref[...], w_ref[...])
    n_ref[...] = n
    n_hi, n_lo = _bf16_split2(n)
    w_hi, w_lo = _bf16_split2(wr_ref[...])
    logits = (jnp.dot(n_hi, w_hi, preferred_element_type=F32) + jnp.dot(n_lo, w_hi, preferred_element_type=F32)
              + jnp.dot(n_hi, w_lo, preferred_element_type=F32)) + br_ref[...]
    rows = logits.shape[0]
    col = lax.broadcasted_iota(jnp.int32, logits.shape, 1)
    big = jnp.int32(LANES)
    neg = -jnp.inf

    def first_argmax(vals):
        top = jnp.max(vals, axis=-1, keepdims=True)
        return top, jnp.min(jnp.where(vals == top, col, big), axis=-1, keepdims=True)

    group_logits = jnp.where(col < N_GROUPS, logits, neg)
    g_top, g_idx = first_argmax(group_logits)
    p_group = 1.0 / jnp.sum(jnp.exp(group_logits - g_top), axis=-1, keepdims=True)
    lo = N_GROUPS + g_idx * EXPERTS_PER_GROUP
    in_group = jnp.where((col >= lo) & (col < lo + EXPERTS_PER_GROUP), logits, neg)
    top1, idx1 = first_argmax(in_group)
    top2, idx2 = first_argmax(jnp.where(col == idx1, neg, in_group))
    r = jnp.exp(top2 - top1)
    p1 = p_group / (1.0 + r)
    p2 = p_group * r / (1.0 + r)
    p_ref[...] = jnp.where(col == 0, p1, jnp.where(col == 1, p2, 0.0))

    chosen = ((col == idx1) | (col == idx2)).astype(BF16)
    earlier = (lax.broadcasted_iota(jnp.int32, (rows, rows), 0)
               > lax.broadcasted_iota(jnp.int32, (rows, rows), 1)).astype(BF16)
    before = jnp.dot(earlier, chosen, preferred_element_type=F32) + carry_ref[...]
    rank1 = jnp.sum(jnp.where(col == idx1, before, 0.0), axis=-1, keepdims=True).astype(jnp.int32)
    rank2 = jnp.sum(jnp.where(col == idx2, before, 0.0), axis=-1, keepdims=True).astype(jnp.int32)
    carry_ref[...] += jnp.sum(chosen.astype(F32), axis=0, keepdims=True)
    cnt_ref[...] = carry_ref[...]
    meta_ref[...] = jnp.where(col == 0, idx1 - N_GROUPS, jnp.where(col == 1, idx2 - N_GROUPS,
                              jnp.where(col == 2, rank1, jnp.where(col == 3, rank2, 0))))


def moe_router(h, norm_w, w_router, b_router):
    t, d = h.shape
    row = pl.BlockSpec((NORM_ROWS, d), lambda i: (i, 0))
    small = pl.BlockSpec((NORM_ROWS, LANES), lambda i: (i, 0))
    one = pl.BlockSpec((1, LANES), lambda i: (0, 0))
    return pl.pallas_call(
        _router_kernel,
        grid=(t // NORM_ROWS,),
        in_specs=[row, pl.BlockSpec((1, d), lambda i: (0, 0)), pl.BlockSpec((d, LANES), lambda i: (0, 0)), one],
        out_specs=[row, small, small, one],
        out_shape=[jax.ShapeDtypeStruct((t, d), F32), jax.ShapeDtypeStruct((t, LANES), jnp.int32),
                   jax.ShapeDtypeStruct((t, LANES), F32), jax.ShapeDtypeStruct((1, LANES), F32)],
        scratch_shapes=[pltpu.VMEM((1, LANES), F32)],
        compiler_params=_params(("arbitrary",)),
        name="moe_router",
    )(h, norm_w.reshape(1, d), w_router, b_router)


def _gather_rows(idx_ref, src_hbm, dst_ref, sem, unrolled):
    n_rows = dst_ref.shape[0]

    def start(r):
        pltpu.make_async_copy(src_hbm.at[pl.ds(idx_ref[r], 1)], dst_ref.at[pl.ds(r, 1)], sem).start()

    if unrolled:
        for r in range(n_rows):
            start(r)
    else:
        def body(r, carry):
            start(r)
            return carry
        lax.fori_loop(0, n_rows, body, 0)


def _wait_rows(src_hbm, dst_ref, sem):
    pltpu.make_async_copy(src_hbm.at[pl.ds(0, dst_ref.shape[0])], dst_ref, sem).wait()


def _expert_kernel(be_ref, nused_ref, tok_cur_ref, tok_next_ref, x_hbm, wg_ref, wu_ref, wd_ref, y_ref,
                   xbuf_ref, sem_ref):
    i = pl.program_id(0)
    last = pl.num_programs(0) - 1
    n_used = nused_ref[0]
    slot = i % 2

    @pl.when((i == 0) & (n_used > 0))
    def _():
        _gather_rows(tok_cur_ref, x_hbm, xbuf_ref.at[0], sem_ref.at[0], unrolled=False)

    @pl.when(i < n_used)
    def _():
        _gather_rows(tok_next_ref, x_hbm, xbuf_ref.at[1 - slot], sem_ref.at[1 - slot], unrolled=True)
        _wait_rows(x_hbm, xbuf_ref.at[slot], sem_ref.at[slot])
        x = xbuf_ref[slot].astype(BF16)
        gate = jnp.dot(x, wg_ref[0], preferred_element_type=F32)
        up = jnp.dot(x, wu_ref[0], preferred_element_type=F32)
        hidden = (gate * _sigmoid(gate) * up).astype(BF16)
        y_ref[...] = jnp.dot(hidden, wd_ref[0], preferred_element_type=F32)

    @pl.when((i == n_used) & (n_used > 0))
    def _():
        _wait_rows(x_hbm, xbuf_ref.at[slot], sem_ref.at[slot])

    @pl.when((i == last) & (n_used > last))
    def _():
        _wait_rows(x_hbm, xbuf_ref.at[1 - slot], sem_ref.at[1 - slot])

    @pl.when(i >= n_used)
    def _():
        y_ref[...] = jnp.zeros_like(y_ref)


def expert_mlp(x, buf_tok, block_e, n_used, w_gate, w_up, w_down):
    p = buf_tok.shape[0]
    nb = p // MOE_ROWS
    d = x.shape[1]
    smem = pltpu.SMEM
    grid_spec = pltpu.PrefetchScalarGridSpec(
        num_scalar_prefetch=2,
        grid=(nb,),
        in_specs=[pl.BlockSpec((MOE_ROWS,), lambda i, be, nu: (i,), memory_space=smem),
                  pl.BlockSpec((MOE_ROWS,), lambda i, be, nu: (jnp.minimum(i + 1, nb - 1),), memory_space=smem),
                  pl.BlockSpec(memory_space=pl.ANY),
                  pl.BlockSpec((1, d, EXPERT_HIDDEN), lambda i, be, nu: (be[i], 0, 0)),
                  pl.BlockSpec((1, d, EXPERT_HIDDEN), lambda i, be, nu: (be[i], 0, 0)),
                  pl.BlockSpec((1, EXPERT_HIDDEN, d), lambda i, be, nu: (be[i], 0, 0))],
        out_specs=pl.BlockSpec((MOE_ROWS, d), lambda i, be, nu: (i, 0)),
        scratch_shapes=[pltpu.VMEM((2, MOE_ROWS, d), F32), pltpu.SemaphoreType.DMA((2,))],
    )
    return pl.pallas_call(
        _expert_kernel,
        grid_spec=grid_spec,
        out_shape=jax.ShapeDtypeStruct((p, d), F32),
        compiler_params=_params(("arbitrary",)),
        name="expert_mlp",
    )(block_e, n_used, buf_tok, buf_tok, x, w_gate, w_up, w_down)


def _combine_kernel(pos0_ref, pos1_ref, pos0_next_ref, pos1_next_ref, y_hbm, h_ref, p_ref, w_ref, o_ref, n_ref,
                    ybuf_ref, sem_ref):
    i = pl.program_id(0)
    last = pl.num_programs(0) - 1
    slot = i % 2

    @pl.when(i == 0)
    def _():
        _gather_rows(pos0_ref, y_hbm, ybuf_ref.at[0, 0], sem_ref.at[0, 0], unrolled=False)
        _gather_rows(pos1_ref, y_hbm, ybuf_ref.at[0, 1], sem_ref.at[0, 1], unrolled=False)

    _gather_rows(pos0_next_ref, y_hbm, ybuf_ref.at[1 - slot, 0], sem_ref.at[1 - slot, 0], unrolled=True)
    _gather_rows(pos1_next_ref, y_hbm, ybuf_ref.at[1 - slot, 1], sem_ref.at[1 - slot, 1], unrolled=True)
    _wait_rows(y_hbm, ybuf_ref.at[slot, 0], sem_ref.at[slot, 0])
    _wait_rows(y_hbm, ybuf_ref.at[slot, 1], sem_ref.at[slot, 1])
    p = p_ref[...]
    h = h_ref[...] + (ybuf_ref[slot, 0] * p[:, 0:1] + ybuf_ref[slot, 1] * p[:, 1:2])
    o_ref[...] = h
    n_ref[...] = _rms(h, w_ref[...]).astype(n_ref.dtype)

    @pl.when(i == last)
    def _():
        _wait_rows(y_hbm, ybuf_ref.at[1 - slot, 0], sem_ref.at[1 - slot, 0])
        _wait_rows(y_hbm, ybuf_ref.at[1 - slot, 1], sem_ref.at[1 - slot, 1])


def moe_combine(y, pos0, pos1, probs, h, norm_w):
    t, d = h.shape
    nb = t // COMBINE_ROWS
    smem = pltpu.SMEM
    row = pl.BlockSpec((COMBINE_ROWS, d), lambda i: (i, 0))
    cur = pl.BlockSpec((COMBINE_ROWS,), lambda i: (i,), memory_space=smem)
    nxt = pl.BlockSpec((COMBINE_ROWS,), lambda i: (jnp.minimum(i + 1, nb - 1),), memory_space=smem)
    return pl.pallas_call(
        _combine_kernel,
        grid=(nb,),
        in_specs=[cur, cur, nxt, nxt,
                  pl.BlockSpec(memory_space=pl.ANY),
                  row,
                  pl.BlockSpec((COMBINE_ROWS, LANES), lambda i: (i, 0)),
                  pl.BlockSpec((1, d), lambda i: (0, 0))],
        out_specs=[row, row],
        out_shape=[jax.ShapeDtypeStruct((t, d), F32), jax.ShapeDtypeStruct((t, d), BF16)],
        scratch_shapes=[pltpu.VMEM((2, TOP_K, COMBINE_ROWS, d), F32), pltpu.SemaphoreType.DMA((2, TOP_K))],
        compiler_params=_params(("arbitrary",)),
        name="moe_combine",
    )(pos0, pos1, pos0, pos1, y, h, probs, norm_w.reshape(1, d))


def _dispatch_plan(meta, counts):
    t = meta.shape[0]
    a = t * TOP_K
    expert_ids = meta[:, :TOP_K]
    ranks = meta[:, TOP_K:2 * TOP_K]
    counts = counts[0, N_GROUPS:N_GROUPS + N_EXPERTS].astype(jnp.int32)
    padded = (counts + MOE_ROWS - 1) // MOE_ROWS * MOE_ROWS
    pad_end = jnp.cumsum(padded)
    pad_start = pad_end - padded
    dest = pad_start[expert_ids] + ranks
    p = a + N_EXPERTS * MOE_ROWS
    nb = p // MOE_ROWS
    token_ids = jnp.repeat(jnp.arange(t, dtype=jnp.int32), TOP_K)
    buf_tok = jnp.zeros((p,), jnp.int32).at[dest.reshape(a)].set(token_ids, unique_indices=True,
                                                                mode="promise_in_bounds")
    block_start = jnp.arange(nb, dtype=jnp.int32) * MOE_ROWS
    block_e = jnp.minimum(jnp.sum((pad_end[None, :] <= block_start[:, None]).astype(jnp.int32), axis=1),
                          N_EXPERTS - 1)
    n_used = (pad_end[-1:] // MOE_ROWS).astype(jnp.int32)
    return buf_tok, block_e, n_used, dest[:, 0], dest[:, 1]


def _pad_cols(w, n):
    return jnp.pad(w, ((0, 0), (0, n - w.shape[1])))


def _router_weights(i, router_group_w, router_group_b, router_expert_w, router_expert_b):
    w_r = _pad_cols(jnp.concatenate([router_group_w[i], router_expert_w[i]], axis=1), LANES)
    b_r = _pad_cols(jnp.concatenate([router_group_b[i], router_expert_b[i]])[None, :], LANES)
    return w_r, b_r


def kernel(x, p, mix_norm_w, w_in, fox_f_bias, pool_w, pool_scale, sgu_norm_w, sgu_w, sgu_b,
           hgrn_lb_logits, hgrn_norm_w, branch_proj, w_out, ffn_norm_w, router_group_w,
           router_group_b, router_expert_w, router_expert_b, expert_w_gate, expert_w_up,
           expert_w_down, ple_norm_w, ple_gate_w, ple_proj_w, final_norm_w):
    bsz, seq, d = x.shape
    t = bsz * seq
    lb_p = jax.nn.softmax(hgrn_lb_logits.astype(F32), axis=0)
    lower_bounds = jnp.cumsum(lb_p, axis=0) - lb_p[0:1]
    sgu_u_col = FOX_QKV_END
    gate_col = FOX_QKV_END + 2 * BRANCH_WIDTH
    h = x.reshape(t, d)
    w_in_t = jnp.transpose(w_in, (0, 2, 1))

    def tiles(*ranges):
        return [c for lo, hi in ranges for c in range(lo, hi, MM_COLS)]

    starts_a = tiles((0, FOX_QKV_END), (POOL_END, SGU_END), (HGRN_END, IN_COLS))
    starts_b = tiles((FOX_F_END, POOL_END), (SGU_END, HGRN_END))
    n_a = len(starts_a) * MM_COLS
    n_b = len(starts_b) * MM_COLS
    q_scale = jnp.where(jnp.arange(n_a) < BRANCH_WIDTH, FOX_HEAD_DIM ** -0.5 * LOG2E, 1.0).astype(F32)[None, :]
    pool_w_bf, branch_proj_bf, w_out_bf, ple_gate_bf, ple_proj_bf = (
        w.astype(BF16) for w in (pool_w, branch_proj, w_out, ple_gate_w, ple_proj_w))
    p_bf = p.reshape(DEPTH, t, PLE_DIM).astype(BF16)
    for i in range(DEPTH):
        w_r, b_r = _router_weights(i, router_group_w, router_group_b, router_expert_w, router_expert_b)
        n, z_f = rms_norm_proj(h, mix_norm_w[i], w_in_t, i, FOX_QKV_END, BF16)
        z_a, (w_gate_bf, w_up_bf) = in_projection(n, w_in_t, i, starts_a, q_scale, BF16,
                                                  cast=((expert_w_gate, CAST_ROWS), (expert_w_up, CAST_ROWS)))
        z_b, (w_down_bf,) = in_projection(n, w_in_t, i, starts_b, jnp.ones((1, n_b), F32), F32,
                                          cast=((expert_w_down, CAST_ROWS),))
        z_a3 = z_a.reshape(bsz, seq, -1)
        z_b3 = z_b.reshape(bsz, seq, -1)

        f_bias = _pad_cols(fox_f_bias[i][None, :].astype(F32), LANES)
        eq, ek = forget_prep(z_f.reshape(bsz, seq, LANES), f_bias)
        br_fox = fox_attention(z_a3, eq, ek).reshape(t, BRANCH_WIDTH)
        br_pool = multiscale_pool(z_b3, pool_w_bf, i, pool_scale[i]).reshape(t, BRANCH_WIDTH)
        br_sgu = spatial_gating(z_a, sgu_u_col, sgu_norm_w[i], sgu_w[i], sgu_b[i])
        br_hgrn = hgrn2(z_b3, BRANCH_WIDTH, lower_bounds[i], hgrn_norm_w[i]).reshape(t, BRANCH_WIDTH)

        merged = merge_branches((br_fox, br_pool, br_sgu, br_hgrn), branch_proj_bf, i, z_a, gate_col)
        h = matmul_residual(merged, w_out_bf, i, h)

        xn, meta, probs, counts = moe_router(h, ffn_norm_w[i], w_r, b_r)
        buf_tok, block_e, n_used, pos0, pos1 = _dispatch_plan(meta, counts)
        y = expert_mlp(xn, buf_tok, block_e, n_used, w_gate_bf, w_up_bf, w_down_bf)
        h, n_ple = moe_combine(y, pos0, pos1, probs, h, ple_norm_w[i])

        h = ple_update(n_ple, ple_gate_bf, p_bf, ple_proj_bf, i, h)
    return rms_norm(h, final_norm_w, F32).reshape(bsz, seq, d)
```

```python
import functools

import jax
import jax.numpy as jnp
from jax import lax
from jax.experimental import pallas as pl
from jax.experimental.pallas import tpu as pltpu

F32 = jnp.float32
BF16 = jnp.bfloat16
HIGHEST = lax.Precision.HIGHEST

D_MODEL = 2048
DEPTH = 2
BRANCH_WIDTH = D_MODEL // 2
N_BRANCHES = 4
PLE_DIM = 256
EPS = 1e-6
FOX_HEAD_DIM = 128
FOX_HEADS = BRANCH_WIDTH // FOX_HEAD_DIM
POOL_WINDOWS = (2, 4, 8, 16)
POOL_GROUP = BRANCH_WIDTH // len(POOL_WINDOWS)
POOL_HALO = 16
SGU_GROUPS = 4
SGU_CHUNK = 128
SGU_GROUP_WIDTH = BRANCH_WIDTH // SGU_GROUPS
HGRN_KEY_DIM = 128
HGRN_VAL_DIM = 128
HGRN_HEADS = BRANCH_WIDTH // HGRN_VAL_DIM
HGRN_SUBCHUNK = 16
HGRN_HEADS_PER_STEP = 2
N_GROUPS = 4
EXPERTS_PER_GROUP = 8
N_EXPERTS = N_GROUPS * EXPERTS_PER_GROUP
TOP_K = 2
EXPERT_HIDDEN = D_MODEL // 2
FOX_QKV_END = 3 * BRANCH_WIDTH
FOX_F_END = FOX_QKV_END + FOX_HEADS
POOL_END = FOX_F_END + BRANCH_WIDTH
SGU_END = POOL_END + 2 * BRANCH_WIDTH
HGRN_END = SGU_END + 4 * BRANCH_WIDTH
IN_COLS = HGRN_END + N_BRANCHES * D_MODEL

LANES = 128
SUBLANES = 8
LOG2E = 1.4426950408889634
FOX_HEADS_PER_STEP = 8
FOX_QUERY_CHUNK = 256
VMEM_LIMIT = 56 * 1024 * 1024

NORM_ROWS = 512
MM_ROWS = 2048
MM_COLS = 512
CAST_ROWS = 512
MERGE_ROWS = 1024
ATT_BLOCK = 512
CUMSUM_BLOCK = 512
POOL_ROWS = 512
SGU_ROWS = 512
HGRN_ROWS = 512
MOE_ROWS = 256
COMBINE_ROWS = 256
PLE_ROWS = 2048


def _params(semantics, **kw):
    return pltpu.CompilerParams(dimension_semantics=semantics, vmem_limit_bytes=VMEM_LIMIT, **kw)


def _sigmoid(x):
    return 1.0 / (1.0 + jnp.exp(-x))


def _log_sigmoid(x):
    return jnp.minimum(x, 0.0) - jnp.log(1.0 + jnp.exp(-jnp.abs(x)))


def _rms(x, w):
    return x * lax.rsqrt(jnp.mean(x * x, axis=-1, keepdims=True) + EPS) * w


def _norm_kernel(x_ref, w_ref, n_ref):
    n_ref[...] = _rms(x_ref[...], w_ref[...]).astype(n_ref.dtype)


def rms_norm(x, w, out_dtype):
    t, d = x.shape
    return pl.pallas_call(
        _norm_kernel,
        grid=(t // NORM_ROWS,),
        in_specs=[pl.BlockSpec((NORM_ROWS, d), lambda i: (i, 0)),
                  pl.BlockSpec((1, d), lambda i: (0, 0))],
        out_specs=pl.BlockSpec((NORM_ROWS, d), lambda i: (i, 0)),
        out_shape=jax.ShapeDtypeStruct((t, d), out_dtype),
        compiler_params=_params(("parallel",)),
        name="rms_norm",
    )(x, w.reshape(1, d))


def _norm_proj_kernel(x_ref, w_ref, wpt_ref, n_ref, p_ref):
    n = _rms(x_ref[...], w_ref[...]).astype(BF16)
    n_ref[...] = n.astype(n_ref.dtype)
    p_ref[...] = lax.dot_general(n, wpt_ref[0].astype(BF16), (((1,), (1,)), ((), ())), preferred_element_type=F32)


def rms_norm_proj(x, w, wt, layer, row0, out_dtype):
    t, d = x.shape
    jp = row0 // LANES
    return pl.pallas_call(
        _norm_proj_kernel,
        grid=(t // NORM_ROWS,),
        in_specs=[pl.BlockSpec((NORM_ROWS, d), lambda i: (i, 0)),
                  pl.BlockSpec((1, d), lambda i: (0, 0)),
                  pl.BlockSpec((1, LANES, d), lambda i: (layer, jp, 0))],
        out_specs=[pl.BlockSpec((NORM_ROWS, d), lambda i: (i, 0)),
                   pl.BlockSpec((NORM_ROWS, LANES), lambda i: (i, 0))],
        out_shape=[jax.ShapeDtypeStruct((t, d), out_dtype),
                   jax.ShapeDtypeStruct((t, LANES), F32)],
        compiler_params=_params(("parallel",)),
        name="rms_norm_proj",
    )(x, w.reshape(1, d), wt)


def _in_proj_kernel(start_tiles_ref, a_ref, wt_ref, s_ref, *rest, n_cast):
    cast_in, o_ref, cast_out = rest[:n_cast], rest[n_cast], rest[n_cast + 1:]
    w = wt_ref[0].astype(BF16)
    acc = lax.dot_general(a_ref[...], w, (((1,), (1,)), ((), ())), preferred_element_type=F32)
    o_ref[...] = (acc * s_ref[...]).astype(o_ref.dtype)
    for src_ref, dst_ref in zip(cast_in, cast_out):
        dst_ref[...] = src_ref[0].astype(dst_ref.dtype)


def in_projection(a, wt, layer, starts, col_scale, out_dtype, cast=()):
    m, k = a.shape
    assert all(c % SUBLANES == 0 for c in starts)
    n = len(starts) * MM_COLS
    nj = len(starts)
    n_steps = (m // MM_ROWS) * nj
    cast_specs_in, cast_specs_out, cast_shapes = [], [], []
    for arr, rows in cast:
        _, e, r, c = arr.shape
        per = r // rows
        n_blocks = e * per
        assert n_blocks <= n_steps

        def block(i, j, st, per=per, n_blocks=n_blocks):
            kk = jnp.minimum(i * nj + j, n_blocks - 1)
            return kk // per, kk % per

        cast_specs_in.append(pl.BlockSpec((1, 1, rows, c), lambda i, j, st, block=block: (layer, *block(i, j, st), 0)))
        cast_specs_out.append(pl.BlockSpec((1, rows, c), lambda i, j, st, block=block: (*block(i, j, st), 0)))
        cast_shapes.append(jax.ShapeDtypeStruct((e, r, c), BF16))
    grid_spec = pltpu.PrefetchScalarGridSpec(
        num_scalar_prefetch=1,
        grid=(m // MM_ROWS, nj),
        in_specs=[pl.BlockSpec((MM_ROWS, k), lambda i, j, st: (i, 0)),
                  pl.BlockSpec((pl.Element(1), pl.Element(MM_COLS), pl.Element(k)),
                               lambda i, j, st: (layer, st[j] * SUBLANES, 0)),
                  pl.BlockSpec((1, MM_COLS), lambda i, j, st: (0, j))] + cast_specs_in,
        out_specs=[pl.BlockSpec((MM_ROWS, MM_COLS), lambda i, j, st: (i, j))] + cast_specs_out,
    )
    outs = pl.pallas_call(
        functools.partial(_in_proj_kernel, n_cast=len(cast)),
        grid_spec=grid_spec,
        out_shape=[jax.ShapeDtypeStruct((m, n), out_dtype)] + cast_shapes,
        compiler_params=_params(("arbitrary", "arbitrary")),
        name="in_projection",
    )(jnp.asarray([c // SUBLANES for c in starts], jnp.int32), a, wt, col_scale, *[arr for arr, _ in cast])
    return outs[0], outs[1:]


def _mm_res_kernel(a_ref, w_ref, r_ref, o_ref):
    o_ref[...] = r_ref[...] + jnp.dot(a_ref[...], w_ref[0], preferred_element_type=F32)


def matmul_residual(a, w, layer, res):
    m, k = a.shape
    n = w.shape[2]
    return pl.pallas_call(
        _mm_res_kernel,
        grid=(m // MM_ROWS, n // MM_COLS),
        in_specs=[pl.BlockSpec((MM_ROWS, k), lambda i, j: (i, 0)),
                  pl.BlockSpec((1, k, MM_COLS), lambda i, j: (layer, 0, j)),
                  pl.BlockSpec((MM_ROWS, MM_COLS), lambda i, j: (i, j))],
        out_specs=pl.BlockSpec((MM_ROWS, MM_COLS), lambda i, j: (i, j)),
        out_shape=jax.ShapeDtypeStruct((m, n), F32),
        compiler_params=_params(("parallel", "arbitrary")),
        name="matmul_residual",
    )(a, w, res)


def _merge_kernel(b0_ref, b1_ref, b2_ref, b3_ref, p_ref, g0_ref, g1_ref, g2_ref, g3_ref, o_ref):
    acc = None
    for bi, (b_ref, g_ref) in enumerate(((b0_ref, g0_ref), (b1_ref, g1_ref),
                                         (b2_ref, g2_ref), (b3_ref, g3_ref))):
        y = jnp.dot(b_ref[...], p_ref[0, bi], preferred_element_type=F32)
        y = _sigmoid(g_ref[...].astype(F32)) * y
        acc = y if acc is None else acc + y
    o_ref[...] = acc.astype(o_ref.dtype)


def merge_branches(branches, proj, layer, z_a, gate_col0):
    t = branches[0].shape[0]
    g0 = gate_col0 // MM_COLS
    per = D_MODEL // MM_COLS
    b_spec = pl.BlockSpec((MERGE_ROWS, BRANCH_WIDTH), lambda i, j: (i, 0))
    g_specs = [pl.BlockSpec((MERGE_ROWS, MM_COLS), functools.partial(lambda i, j, o: (i, o + j), o=g0 + bi * per))
               for bi in range(N_BRANCHES)]
    return pl.pallas_call(
        _merge_kernel,
        grid=(t // MERGE_ROWS, per),
        in_specs=[b_spec] * N_BRANCHES
        + [pl.BlockSpec((1, N_BRANCHES, BRANCH_WIDTH, MM_COLS), lambda i, j: (layer, 0, 0, j))] + g_specs,
        out_specs=pl.BlockSpec((MERGE_ROWS, MM_COLS), lambda i, j: (i, j)),
        out_shape=jax.ShapeDtypeStruct((t, D_MODEL), BF16),
        compiler_params=_params(("parallel", "arbitrary")),
        name="merge_branches",
    )(*branches, proj, z_a, z_a, z_a, z_a)


def _ple_kernel(n_ref, wg_ref, p_ref, wp_ref, h_ref, o_ref):
    gate = _sigmoid(jnp.dot(n_ref[...], wg_ref[0], preferred_element_type=F32))
    emb = jnp.dot(p_ref[0], wp_ref[0], preferred_element_type=F32)
    o_ref[...] = h_ref[...] + gate * emb


def ple_update(n, wg, p, wp, layer, h):
    t = h.shape[0]
    return pl.pallas_call(
        _ple_kernel,
        grid=(t // PLE_ROWS, D_MODEL // MM_COLS),
        in_specs=[pl.BlockSpec((PLE_ROWS, D_MODEL), lambda i, j: (i, 0)),
                  pl.BlockSpec((1, D_MODEL, MM_COLS), lambda i, j: (layer, 0, j)),
                  pl.BlockSpec((1, PLE_ROWS, PLE_DIM), lambda i, j: (layer, i, 0)),
                  pl.BlockSpec((1, PLE_DIM, MM_COLS), lambda i, j: (layer, 0, j)),
                  pl.BlockSpec((PLE_ROWS, MM_COLS), lambda i, j: (i, j))],
        out_specs=pl.BlockSpec((PLE_ROWS, MM_COLS), lambda i, j: (i, j)),
        out_shape=jax.ShapeDtypeStruct((t, D_MODEL), F32),
        compiler_params=_params(("parallel", "arbitrary")),
        name="ple_update",
    )(n, wg, p, wp, h)


def _bf16_split3(x):
    hi = x.astype(BF16).astype(F32)
    r = x - hi
    mid = r.astype(BF16).astype(F32)
    lo = (r - mid).astype(BF16).astype(F32)
    return hi, mid, lo


def _forget_prep_kernel(z_ref, b_ref, eq_ref, ek_ref, carry_ref):
    @pl.when(pl.program_id(1) == 0)
    def _():
        carry_ref[...] = jnp.zeros_like(carry_ref)

    lf = _log_sigmoid(z_ref[0] + b_ref[...]) * LOG2E
    n = lf.shape[0]
    tril = (lax.broadcasted_iota(jnp.int32, (n, n), 0) >= lax.broadcasted_iota(jnp.int32, (n, n), 1)).astype(F32)
    c = jnp.dot(tril, lf, precision=HIGHEST, preferred_element_type=F32) + carry_ref[...]
    carry_ref[...] = c[n - 1:n, :]
    parts = _bf16_split3(c)
    lane = lax.broadcasted_iota(jnp.int32, (n, LANES), 1)
    for h in range(FOX_HEADS):
        hi, mid, lo = (jnp.sum(jnp.where(lane == h, part, 0.0), axis=-1, keepdims=True) for part in parts)
        eq = jnp.where(lane == 0, hi, jnp.where(lane == 1, mid, jnp.where(lane == 2, lo,
                       jnp.where(lane < 6, 1.0, 0.0))))
        ek = jnp.where(lane < 3, 1.0, jnp.where(lane == 3, -hi, jnp.where(lane == 4, -mid,
                       jnp.where(lane == 5, -lo, 0.0))))
        eq_ref[0, h] = eq.astype(BF16)
        ek_ref[0, h] = ek.astype(BF16)


def forget_prep(z_f, bias):
    b, s, _ = z_f.shape
    out = pl.BlockSpec((1, FOX_HEADS, CUMSUM_BLOCK, LANES), lambda bi, i: (bi, 0, i, 0))
    shape = jax.ShapeDtypeStruct((b, FOX_HEADS, s, LANES), BF16)
    return pl.pallas_call(
        _forget_prep_kernel,
        grid=(b, s // CUMSUM_BLOCK),
        in_specs=[pl.BlockSpec((1, CUMSUM_BLOCK, LANES), lambda bi, i: (bi, i, 0)),
                  pl.BlockSpec((1, LANES), lambda bi, i: (0, 0))],
        out_specs=[out, out],
        out_shape=[shape, shape],
        scratch_shapes=[pltpu.VMEM((1, LANES), F32)],
        compiler_params=_params(("parallel", "arbitrary")),
        name="forget_prep",
    )(z_f, bias)


def _fox_kernel(qi_ref, ki_ref, q_ref, k_ref, v_ref, eq_ref, ek_ref, o_ref, m_ref, l_ref, acc_ref):
    step = pl.program_id(2)
    qi = qi_ref[step]
    ki = ki_ref[step]
    blk = ATT_BLOCK
    dh = FOX_HEAD_DIM
    qc = FOX_QUERY_CHUNK

    @pl.when(ki == 0)
    def _():
        m_ref[...] = jnp.full_like(m_ref, -jnp.inf)
        l_ref[...] = jnp.zeros_like(l_ref)
        acc_ref[...] = jnp.zeros_like(acc_ref)

    def update(masked):
        chains = [(hh, c) for hh in range(FOX_HEADS_PER_STEP) for c in range(blk // qc)]
        scores = []
        for hh, c in chains:
            cols = slice(hh * dh, (hh + 1) * dh)
            qs = slice(c * qc, (c + 1) * qc)
            k = jnp.concatenate([k_ref[0, :, cols], ek_ref[0, hh]], axis=1)
            q = jnp.concatenate([q_ref[0, qs, cols], eq_ref[0, hh, qs, :]], axis=1)
            s = lax.dot_general(k, q, (((1,), (1,)), ((), ())), preferred_element_type=F32)
            if masked:
                key = lax.broadcasted_iota(jnp.int32, (blk, qc), 0)
                qry = lax.broadcasted_iota(jnp.int32, (blk, qc), 1) + c * qc
                s = jnp.where(key <= qry, s, -jnp.inf)
            scores.append(s)
        probs = []
        for (hh, c), s in zip(chains, scores):
            qs = slice(c * qc, (c + 1) * qc)
            m_old = m_ref[hh, :, qs]
            m_new = jnp.maximum(m_old, jnp.max(s, axis=0, keepdims=True))
            alpha = jnp.exp2(m_old - m_new)
            p = jnp.exp2(s - m_new)
            l_ref[hh, :, qs] = alpha * l_ref[hh, :, qs] + jnp.sum(p, axis=0, keepdims=True)
            m_ref[hh, :, qs] = m_new
            probs.append((alpha, p.astype(BF16)))
        v_t = [jnp.transpose(v_ref[0, :, hh * dh:(hh + 1) * dh]) for hh in range(FOX_HEADS_PER_STEP)]
        for (hh, c), (alpha, p) in zip(chains, probs):
            qs = slice(c * qc, (c + 1) * qc)
            acc_ref[hh, :, qs] = alpha * acc_ref[hh, :, qs] + jnp.dot(v_t[hh], p,
                                                                      preferred_element_type=F32)

    @pl.when(ki < qi)
    def _():
        update(False)

    @pl.when(ki == qi)
    def _():
        update(True)
        for hh in range(FOX_HEADS_PER_STEP):
            o_ref[0, :, hh * dh:(hh + 1) * dh] = jnp.transpose(acc_ref[hh] / l_ref[hh]).astype(o_ref.dtype)


def fox_attention(z_a, eq, ek):
    b, s, _ = z_a.shape
    nb = s // ATT_BLOCK
    hp = FOX_HEADS_PER_STEP
    groups = FOX_HEADS // hp
    pairs = [(qi, ki) for qi in range(nb) for ki in range(qi + 1)]
    qi_tab = jnp.asarray([pr[0] for pr in pairs], jnp.int32)
    ki_tab = jnp.asarray([pr[1] for pr in pairs], jnp.int32)
    width = hp * FOX_HEAD_DIM
    grid_spec = pltpu.PrefetchScalarGridSpec(
        num_scalar_prefetch=2,
        grid=(b, groups, len(pairs)),
        in_specs=[pl.BlockSpec((1, ATT_BLOCK, width), lambda bi, g, st, qt, kt: (bi, qt[st], g)),
                  pl.BlockSpec((1, ATT_BLOCK, width), lambda bi, g, st, qt, kt: (bi, kt[st], groups + g)),
                  pl.BlockSpec((1, ATT_BLOCK, width), lambda bi, g, st, qt, kt: (bi, kt[st], 2 * groups + g)),
                  pl.BlockSpec((1, hp, ATT_BLOCK, LANES), lambda bi, g, st, qt, kt: (bi, g, qt[st], 0)),
                  pl.BlockSpec((1, hp, ATT_BLOCK, LANES), lambda bi, g, st, qt, kt: (bi, g, kt[st], 0))],
        out_specs=pl.BlockSpec((1, ATT_BLOCK, width), lambda bi, g, st, qt, kt: (bi, qt[st], g)),
        scratch_shapes=[pltpu.VMEM((hp, 1, ATT_BLOCK), F32), pltpu.VMEM((hp, 1, ATT_BLOCK), F32),
                        pltpu.VMEM((hp, FOX_HEAD_DIM, ATT_BLOCK), F32)],
    )
    return pl.pallas_call(
        _fox_kernel,
        grid_spec=grid_spec,
        out_shape=jax.ShapeDtypeStruct((b, s, BRANCH_WIDTH), BF16),
        compiler_params=_params(("parallel", "parallel", "arbitrary")),
        name="fox_attention",
    )(qi_tab, ki_tab, z_a, z_a, z_a, eq, ek)


def _pool_kernel(u_ref, halo_ref, w_ref, scale_ref, o_ref):
    i = pl.program_id(1)
    rows = POOL_ROWS
    t = i * rows + lax.broadcasted_iota(jnp.int32, (rows, 1), 0)
    keep_halo = (i > 0).astype(F32)
    for g, win in enumerate(POOL_WINDOWS):
        cols = slice(g * POOL_GROUP, (g + 1) * POOL_GROUP)
        u = u_ref[0, :, cols]
        ext = jnp.concatenate([halo_ref[0, :, cols] * keep_halo, u], axis=0)
        span = 1
        while span < win:
            n = ext.shape[0]
            ext = ext[:n - span] + ext[span:]
            span *= 2
        window_sum = ext[POOL_HALO + 1 - win:POOL_HALO + 1 - win + rows]
        count = jnp.minimum(t + 1, win).astype(F32)
        d = (window_sum / count - u).astype(BF16)
        y = jnp.dot(d, w_ref[0, g], preferred_element_type=F32) * scale_ref[:, cols]
        o_ref[0, :, cols] = y.astype(o_ref.dtype)


def multiscale_pool(z_b, pool_w, layer, pool_scale):
    b, s, _ = z_b.shape
    per = POOL_ROWS // POOL_HALO
    return pl.pallas_call(
        _pool_kernel,
        grid=(b, s // POOL_ROWS),
        in_specs=[pl.BlockSpec((1, POOL_ROWS, BRANCH_WIDTH), lambda bi, i: (bi, i, 0)),
                  pl.BlockSpec((1, POOL_HALO, BRANCH_WIDTH), lambda bi, i: (bi, jnp.maximum(i * per - 1, 0), 0)),
                  pl.BlockSpec((1, len(POOL_WINDOWS), POOL_GROUP, POOL_GROUP), lambda bi, i: (layer, 0, 0, 0)),
                  pl.BlockSpec((1, BRANCH_WIDTH), lambda bi, i: (0, 0))],
        out_specs=pl.BlockSpec((1, POOL_ROWS, BRANCH_WIDTH), lambda bi, i: (bi, i, 0)),
        out_shape=jax.ShapeDtypeStruct((b, s, BRANCH_WIDTH), BF16),
        compiler_params=_params(("parallel", "parallel")),
        name="multiscale_pool",
    )(z_b, z_b, pool_w, pool_scale.reshape(1, BRANCH_WIDTH))


def _sgu_kernel(u_ref, v_ref, nw_ref, w_ref, b_ref, o_ref):
    c = SGU_CHUNK
    v = jax.nn.gelu(v_ref[...].astype(F32))
    vc = v - jnp.mean(v, axis=-1, keepdims=True)
    vn = (vc * lax.rsqrt(jnp.mean(vc * vc, axis=-1, keepdims=True) + EPS) * nw_ref[...]).astype(BF16)
    causal = lax.broadcasted_iota(jnp.int32, (c, c), 0) >= lax.broadcasted_iota(jnp.int32, (c, c), 1)
    for g in range(SGU_GROUPS):
        cols = slice(g * SGU_GROUP_WIDTH, (g + 1) * SGU_GROUP_WIDTH)
        w = jnp.where(causal, w_ref[g], 0.0).astype(BF16)
        bias = b_ref[g]
        for n in range(SGU_ROWS // c):
            rows = slice(n * c, (n + 1) * c)
            sv = jnp.dot(w, vn[rows, cols], preferred_element_type=F32) + bias
            u = jax.nn.gelu(u_ref[rows, cols].astype(F32))
            o_ref[rows, cols] = (u * sv).astype(o_ref.dtype)


def spatial_gating(z_a, u_col0, norm_w, w_s, b_s):
    t = z_a.shape[0]
    ub = u_col0 // BRANCH_WIDTH
    return pl.pallas_call(
        _sgu_kernel,
        grid=(t // SGU_ROWS,),
        in_specs=[pl.BlockSpec((SGU_ROWS, BRANCH_WIDTH), lambda i: (i, ub)),
                  pl.BlockSpec((SGU_ROWS, BRANCH_WIDTH), lambda i: (i, ub + 1)),
                  pl.BlockSpec((1, BRANCH_WIDTH), lambda i: (0, 0)),
                  pl.BlockSpec((SGU_GROUPS, SGU_CHUNK, SGU_CHUNK), lambda i: (0, 0, 0)),
                  pl.BlockSpec((SGU_GROUPS, SGU_CHUNK, 1), lambda i: (0, 0, 0))],
        out_specs=pl.BlockSpec((SGU_ROWS, BRANCH_WIDTH), lambda i: (i, 0)),
        out_shape=jax.ShapeDtypeStruct((t, BRANCH_WIDTH), BF16),
        compiler_params=_params(("parallel",)),
        name="spatial_gating",
    )(z_a, z_a, norm_w.reshape(1, BRANCH_WIDTH), w_s, b_s.reshape(SGU_GROUPS, SGU_CHUNK, 1))


def _hgrn_kernel(q_ref, f_ref, v_ref, g_ref, lb_ref, nw_ref, o_ref, state_ref):
    c = HGRN_SUBCHUNK
    half = c // 2
    rows_total = HGRN_ROWS
    n = rows_total // c

    @pl.when(pl.program_id(2) == 0)
    def _():
        state_ref[...] = jnp.zeros_like(state_ref)

    local = lax.broadcasted_iota(jnp.int32, (rows_total, 1), 0) & (c - 1)
    t_half = lax.broadcasted_iota(jnp.int32, (1, half, 1), 1)
    for hh in range(HGRN_HEADS_PER_STEP):
        cols = slice(hh * LANES, (hh + 1) * LANES)
        lb = lb_ref[hh]
        q = q_ref[0, :, cols]
        fl = f_ref[0, :, cols]
        v = v_ref[0, :, cols]
        e = jnp.exp(-jnp.abs(fl))
        r = 1.0 / (1.0 + e)
        sig_pos = jnp.where(fl >= 0, r, e * r)
        sig_neg = jnp.where(fl >= 0, e * r, r)
        kk = (1.0 - lb) * sig_neg
        log_kk = jnp.log2(kk)
        b = jnp.log2(lb + (1.0 - lb) * sig_pos)
        shift = 1
        while shift < c:
            b = b + jnp.where(local >= shift, pltpu.roll(b, shift, 0), 0.0)
            shift *= 2
        b3 = b.reshape(n, c, LANES)
        kk3 = kk.reshape(n, c, LANES)
        bk3 = b3 - log_kk.reshape(n, c, LANES)
        q3 = q.reshape(n, c, LANES)
        v3 = v.reshape(n, c, LANES)
        tot3 = b3[:, c - 1:c, :]
        q_dec = (q * jnp.exp2(b)).astype(BF16)
        k_dec = (kk3 * jnp.exp2(tot3 - b3)).reshape(rows_total, LANES).astype(BF16)
        decay = jnp.exp2(tot3)
        v_bf = v.astype(BF16)

        updates = [lax.dot_general(v_bf[i * c:(i + 1) * c], k_dec[i * c:(i + 1) * c], (((0,), (0,)), ((), ())),
                                   preferred_element_type=F32) for i in range(n)]
        state_t = state_ref[hh]
        states = []
        for i in range(n):
            states.append(state_t.astype(BF16))
            state_t = state_t * decay[i] + updates[i]
        state_ref[hh] = state_t
        outs = [lax.dot_general(q_dec[i * c:(i + 1) * c], states[i], (((1,), (1,)), ((), ())),
                                preferred_element_type=F32) for i in range(n)]

        b_lo, b_hi = b3[:, :half, :], b3[:, half:, :]
        q_lo, q_hi = q3[:, :half, :], q3[:, half:, :]
        o_lo = jnp.zeros((n, half, LANES), F32)
        o_hi = jnp.zeros((n, half, LANES), F32)
        for s in range(c):
            bs = bk3[:, s:s + 1, :]
            vs = v3[:, s:s + 1, :]
            if s < half:
                d = jnp.where(t_half >= s, b_lo - bs, -jnp.inf)
                a = jnp.sum(q_lo * jnp.exp2(d), axis=-1, keepdims=True)
                o_lo = o_lo + a * vs
                d = b_hi - bs
            else:
                d = jnp.where(t_half >= s - half, b_hi - bs, -jnp.inf)
            a = jnp.sum(q_hi * jnp.exp2(d), axis=-1, keepdims=True)
            o_hi = o_hi + a * vs
        o = jnp.concatenate([o_lo, o_hi], axis=1).reshape(rows_total, LANES) + jnp.concatenate(outs, axis=0)

        g = g_ref[0, :, cols]
        o = o * lax.rsqrt(jnp.mean(o * o, axis=-1, keepdims=True) + EPS) * nw_ref[hh]
        o_ref[0, :, cols] = (o * (g * _sigmoid(g))).astype(o_ref.dtype)


def hgrn2(z_b, col0, lower_bound, norm_w):
    b, s, _ = z_b.shape
    hp = HGRN_HEADS_PER_STEP
    width = hp * LANES
    c0 = col0 // width
    groups = HGRN_HEADS // hp

    def spec(part):
        return pl.BlockSpec((1, HGRN_ROWS, width), lambda bi, h, i: (bi, i, c0 + part * groups + h))

    vec = pl.BlockSpec((hp, 1, LANES), lambda bi, h, i: (h, 0, 0))
    return pl.pallas_call(
        _hgrn_kernel,
        grid=(b, groups, s // HGRN_ROWS),
        in_specs=[spec(0), spec(1), spec(2), spec(3), vec, vec],
        out_specs=pl.BlockSpec((1, HGRN_ROWS, width), lambda bi, h, i: (bi, i, h)),
        out_shape=jax.ShapeDtypeStruct((b, s, BRANCH_WIDTH), BF16),
        scratch_shapes=[pltpu.VMEM((hp, HGRN_VAL_DIM, HGRN_KEY_DIM), F32)],
        compiler_params=_params(("parallel", "parallel", "arbitrary")),
        name="hgrn2",
    )(z_b, z_b, z_b, z_b, lower_bound.reshape(HGRN_HEADS, 1, LANES), norm_w.reshape(HGRN_HEADS, 1, LANES))


def _bf16_split2(x):
    hi = x.astype(BF16)
    return hi, (x - hi.astype(F32)).astype(BF16)


def _router_kernel(x_ref, w_ref, wr_ref, br_ref, n_ref, meta_ref, p_ref, cnt_ref, carry_ref):
    @pl.when(pl.program_id(0) == 0)
    def _():
        carry_ref[...] = jnp.zeros_like(carry_ref)

    n = _rms(x_ref[...], w_ref[...])
    n_ref[...] = n
    n_hi, n_lo = _bf16_split2(n)
    w_hi, w_lo = _bf16_split2(wr_ref[...])
    logits = (jnp.dot(n_hi, w_hi, preferred_element_type=F32) + jnp.dot(n_lo, w_hi, preferred_element_type=F32)
              + jnp.dot(n_hi, w_lo, preferred_element_type=F32)) + br_ref[...]
    rows = logits.shape[0]
    col = lax.broadcasted_iota(jnp.int32, logits.shape, 1)
    big = jnp.int32(LANES)
    neg = -jnp.inf

    def first_argmax(vals):
        top = jnp.max(vals, axis=-1, keepdims=True)
        return top, jnp.min(jnp.where(vals == top, col, big), axis=-1, keepdims=True)

    group_logits = jnp.where(col < N_GROUPS, logits, neg)
    g_top, g_idx = first_argmax(group_logits)
    p_group = 1.0 / jnp.sum(jnp.exp(group_logits - g_top), axis=-1, keepdims=True)
    lo = N_GROUPS + g_idx * EXPERTS_PER_GROUP
    in_group = jnp.where((col >= lo) & (col < lo + EXPERTS_PER_GROUP), logits, neg)
    top1, idx1 = first_argmax(in_group)
    top2, idx2 = first_argmax(jnp.where(col == idx1, neg, in_group))
    r = jnp.exp(top2 - top1)
    p1 = p_group / (1.0 + r)
    p2 = p_group * r / (1.0 + r)
    p_ref[...] = jnp.where(col == 0, p1, jnp.where(col == 1, p2, 0.0))

    chosen = ((col == idx1) | (col == idx2)).astype(BF16)
    earlier = (lax.broadcasted_iota(jnp.int32, (rows, rows), 0)
               > lax.broadcasted_iota(jnp.int32, (rows, rows), 1)).astype(BF16)
    before = jnp.dot(earlier, chosen, preferred_element_type=F32) + carry_ref[...]
    rank1 = jnp.sum(jnp.where(col == idx1, before, 0.0), axis=-1, keepdims=True).astype(jnp.int32)
    rank2 = jnp.sum(jnp.where(col == idx2, before, 0.0), axis=-1, keepdims=True).astype(jnp.int32)
    carry_ref[...] += jnp.sum(chosen.astype(F32), axis=0, keepdims=True)
    cnt_ref[...] = carry_ref[...]
    meta_ref[...] = jnp.where(col == 0, idx1 - N_GROUPS, jnp.where(col == 1, idx2 - N_GROUPS,
                              jnp.where(col == 2, rank1, jnp.where(col == 3, rank2, 0))))


def moe_router(h, norm_w, w_router, b_router):
    t, d = h.shape
    row = pl.BlockSpec((NORM_ROWS, d), lambda i: (i, 0))
    small = pl.BlockSpec((NORM_ROWS, LANES), lambda i: (i, 0))
    one = pl.BlockSpec((1, LANES), lambda i: (0, 0))
    return pl.pallas_call(
        _router_kernel,
        grid=(t // NORM_ROWS,),
        in_specs=[row, pl.BlockSpec((1, d), lambda i: (0, 0)), pl.BlockSpec((d, LANES), lambda i: (0, 0)), one],
        out_specs=[row, small, small, one],
        out_shape=[jax.ShapeDtypeStruct((t, d), F32), jax.ShapeDtypeStruct((t, LANES), jnp.int32),
                   jax.ShapeDtypeStruct((t, LANES), F32), jax.ShapeDtypeStruct((1, LANES), F32)],
        scratch_shapes=[pltpu.VMEM((1, LANES), F32)],
        compiler_params=_params(("arbitrary",)),
        name="moe_router",
    )(h, norm_w.reshape(1, d), w_router, b_router)


def _gather_rows(idx_ref, src_hbm, dst_ref, sem, unrolled):
    n_rows = dst_ref.shape[0]

    def start(r):
        pltpu.make_async_copy(src_hbm.at[pl.ds(idx_ref[r], 1)], dst_ref.at[pl.ds(r, 1)], sem).start()

    if unrolled:
        for r in range(n_rows):
            start(r)
    else:
        def body(r, carry):
            start(r)
            return carry
        lax.fori_loop(0, n_rows, body, 0)


def _wait_rows(src_hbm, dst_ref, sem):
    pltpu.make_async_copy(src_hbm.at[pl.ds(0, dst_ref.shape[0])], dst_ref, sem).wait()


def _expert_kernel(be_ref, nused_ref, tok_cur_ref, tok_next_ref, x_hbm, wg_ref, wu_ref, wd_ref, y_ref,
                   xbuf_ref, sem_ref):
    i = pl.program_id(0)
    last = pl.num_programs(0) - 1
    n_used = nused_ref[0]
    slot = i % 2

    @pl.when((i == 0) & (n_used > 0))
    def _():
        _gather_rows(tok_cur_ref, x_hbm, xbuf_ref.at[0], sem_ref.at[0], unrolled=False)

    @pl.when(i < n_used)
    def _():
        _wait_rows(x_hbm, xbuf_ref.at[slot], sem_ref.at[slot])
        x = xbuf_ref[slot].astype(BF16)
        gate = jnp.dot(x, wg_ref[0], preferred_element_type=F32)
        up = jnp.dot(x, wu_ref[0], preferred_element_type=F32)
        hidden = (gate * _sigmoid(gate) * up).astype(BF16)
        _gather_rows(tok_next_ref, x_hbm, xbuf_ref.at[1 - slot], sem_ref.at[1 - slot], unrolled=True)
        y_ref[...] = jnp.dot(hidden, wd_ref[0], preferred_element_type=F32)

    @pl.when((i == n_used) & (n_used > 0))
    def _():
        _wait_rows(x_hbm, xbuf_ref.at[slot], sem_ref.at[slot])

    @pl.when((i == last) & (n_used > last))
    def _():
        _wait_rows(x_hbm, xbuf_ref.at[1 - slot], sem_ref.at[1 - slot])

    @pl.when(i >= n_used)
    def _():
        y_ref[...] = jnp.zeros_like(y_ref)


def expert_mlp(x, buf_tok, block_e, n_used, w_gate, w_up, w_down):
    p = buf_tok.shape[0]
    nb = p // MOE_ROWS
    d = x.shape[1]
    smem = pltpu.SMEM
    grid_spec = pltpu.PrefetchScalarGridSpec(
        num_scalar_prefetch=2,
        grid=(nb,),
        in_specs=[pl.BlockSpec((MOE_ROWS,), lambda i, be, nu: (i,), memory_space=smem),
                  pl.BlockSpec((MOE_ROWS,), lambda i, be, nu: (jnp.minimum(i + 1, nb - 1),), memory_space=smem),
                  pl.BlockSpec(memory_space=pl.ANY),
                  pl.BlockSpec((1, d, EXPERT_HIDDEN), lambda i, be, nu: (be[i], 0, 0)),
                  pl.BlockSpec((1, d, EXPERT_HIDDEN), lambda i, be, nu: (be[i], 0, 0)),
                  pl.BlockSpec((1, EXPERT_HIDDEN, d), lambda i, be, nu: (be[i], 0, 0))],
        out_specs=pl.BlockSpec((MOE_ROWS, d), lambda i, be, nu: (i, 0)),
        scratch_shapes=[pltpu.VMEM((2, MOE_ROWS, d), F32), pltpu.SemaphoreType.DMA((2,))],
    )
    return pl.pallas_call(
        _expert_kernel,
        grid_spec=grid_spec,
        out_shape=jax.ShapeDtypeStruct((p, d), F32),
        compiler_params=_params(("arbitrary",)),
        name="expert_mlp",
    )(block_e, n_used, buf_tok, buf_tok, x, w_gate, w_up, w_down)


def _combine_kernel(pos0_ref, pos1_ref, pos0_next_ref, pos1_next_ref, y_hbm, h_ref, p_ref, w_ref, o_ref, n_ref,
                    ybuf_ref, sem_ref):
    i = pl.program_id(0)
    last = pl.num_programs(0) - 1
    slot = i % 2

    @pl.when(i == 0)
    def _():
        _gather_rows(pos0_ref, y_hbm, ybuf_ref.at[0, 0], sem_ref.at[0, 0], unrolled=False)
        _gather_rows(pos1_ref, y_hbm, ybuf_ref.at[0, 1], sem_ref.at[0, 1], unrolled=False)

    _wait_rows(y_hbm, ybuf_ref.at[slot, 0], sem_ref.at[slot, 0])
    _wait_rows(y_hbm, ybuf_ref.at[slot, 1], sem_ref.at[slot, 1])
    p = p_ref[...]
    h = h_ref[...] + (ybuf_ref[slot, 0] * p[:, 0:1] + ybuf_ref[slot, 1] * p[:, 1:2])
    _gather_rows(pos0_next_ref, y_hbm, ybuf_ref.at[1 - slot, 0], sem_ref.at[1 - slot, 0], unrolled=True)
    _gather_rows(pos1_next_ref, y_hbm, ybuf_ref.at[1 - slot, 1], sem_ref.at[1 - slot, 1], unrolled=True)
    o_ref[...] = h
    n_ref[...] = _rms(h, w_ref[...]).astype(n_ref.dtype)

    @pl.when(i == last)
    def _():
        _wait_rows(y_hbm, ybuf_ref.at[1 - slot, 0], sem_ref.at[1 - slot, 0])
        _wait_rows(y_hbm, ybuf_ref.at[1 - slot, 1], sem_ref.at[1 - slot, 1])


def moe_combine(y, pos0, pos1, probs, h, norm_w):
    t, d = h.shape
    nb = t // COMBINE_ROWS
    smem = pltpu.SMEM
    row = pl.BlockSpec((COMBINE_ROWS, d), lambda i: (i, 0))
    cur = pl.BlockSpec((COMBINE_ROWS,), lambda i: (i,), memory_space=smem)
    nxt = pl.BlockSpec((COMBINE_ROWS,), lambda i: (jnp.minimum(i + 1, nb - 1),), memory_space=smem)
    return pl.pallas_call(
        _combine_kernel,
        grid=(nb,),
        in_specs=[cur, cur, nxt, nxt,
                  pl.BlockSpec(memory_space=pl.ANY),
                  row,
                  pl.BlockSpec((COMBINE_ROWS, LANES), lambda i: (i, 0)),
                  pl.BlockSpec((1, d), lambda i: (0, 0))],
        out_specs=[row, row],
        out_shape=[jax.ShapeDtypeStruct((t, d), F32), jax.ShapeDtypeStruct((t, d), BF16)],
        scratch_shapes=[pltpu.VMEM((2, TOP_K, COMBINE_ROWS, d), F32), pltpu.SemaphoreType.DMA((2, TOP_K))],
        compiler_params=_params(("arbitrary",)),
        name="moe_combine",
    )(pos0, pos1, pos0, pos1, y, h, probs, norm_w.reshape(1, d))


def _dispatch_plan(meta, counts):
    t = meta.shape[0]
    a = t * TOP_K
    expert_ids = meta[:, :TOP_K]
    ranks = meta[:, TOP_K:2 * TOP_K]
    counts = counts[0, N_GROUPS:N_GROUPS + N_EXPERTS].astype(jnp.int32)
    padded = (counts + MOE_ROWS - 1) // MOE_ROWS * MOE_ROWS
    pad_end = jnp.cumsum(padded)
    pad_start = pad_end - padded
    dest = pad_start[expert_ids] + ranks
    p = a + N_EXPERTS * MOE_ROWS
    nb = p // MOE_ROWS
    token_ids = jnp.repeat(jnp.arange(t, dtype=jnp.int32), TOP_K)
    buf_tok = jnp.zeros((p,), jnp.int32).at[dest.reshape(a)].set(token_ids, unique_indices=True,
                                                                mode="promise_in_bounds")
    block_start = jnp.arange(nb, dtype=jnp.int32) * MOE_ROWS
    block_e = jnp.minimum(jnp.sum((pad_end[None, :] <= block_start[:, None]).astype(jnp.int32), axis=1),
                          N_EXPERTS - 1)
    n_used = (pad_end[-1:] // MOE_ROWS).astype(jnp.int32)
    return buf_tok, block_e, n_used, dest[:, 0], dest[:, 1]


def _pad_cols(w, n):
    return jnp.pad(w, ((0, 0), (0, n - w.shape[1])))


def _router_weights(i, router_group_w, router_group_b, router_expert_w, router_expert_b):
    w_r = _pad_cols(jnp.concatenate([router_group_w[i], router_expert_w[i]], axis=1), LANES)
    b_r = _pad_cols(jnp.concatenate([router_group_b[i], router_expert_b[i]])[None, :], LANES)
    return w_r, b_r


def kernel(x, p, mix_norm_w, w_in, fox_f_bias, pool_w, pool_scale, sgu_norm_w, sgu_w, sgu_b,
           hgrn_lb_logits, hgrn_norm_w, branch_proj, w_out, ffn_norm_w, router_group_w,
           router_group_b, router_expert_w, router_expert_b, expert_w_gate, expert_w_up,
           expert_w_down, ple_norm_w, ple_gate_w, ple_proj_w, final_norm_w):
    bsz, seq, d = x.shape
    t = bsz * seq
    lb_p = jax.nn.softmax(hgrn_lb_logits.astype(F32), axis=0)
    lower_bounds = jnp.cumsum(lb_p, axis=0) - lb_p[0:1]
    sgu_u_col = FOX_QKV_END
    gate_col = FOX_QKV_END + 2 * BRANCH_WIDTH
    h = x.reshape(t, d)
    w_in_t = jnp.transpose(w_in, (0, 2, 1))

    def tiles(*ranges):
        return [c for lo, hi in ranges for c in range(lo, hi, MM_COLS)]

    starts_a = tiles((0, FOX_QKV_END), (POOL_END, SGU_END), (HGRN_END, IN_COLS))
    starts_b = tiles((FOX_F_END, POOL_END), (SGU_END, HGRN_END))
    n_a = len(starts_a) * MM_COLS
    n_b = len(starts_b) * MM_COLS
    q_scale = jnp.where(jnp.arange(n_a) < BRANCH_WIDTH, FOX_HEAD_DIM ** -0.5 * LOG2E, 1.0).astype(F32)[None, :]
    pool_w_bf, branch_proj_bf, w_out_bf, ple_gate_bf, ple_proj_bf = (
        w.astype(BF16) for w in (pool_w, branch_proj, w_out, ple_gate_w, ple_proj_w))
    p_bf = p.reshape(DEPTH, t, PLE_DIM).astype(BF16)
    for i in range(DEPTH):
        w_r, b_r = _router_weights(i, router_group_w, router_group_b, router_expert_w, router_expert_b)
        n, z_f = rms_norm_proj(h, mix_norm_w[i], w_in_t, i, FOX_QKV_END, BF16)
        z_a, (w_gate_bf, w_up_bf) = in_projection(n, w_in_t, i, starts_a, q_scale, BF16,
                                                  cast=((expert_w_gate, CAST_ROWS), (expert_w_up, CAST_ROWS)))
        z_b, (w_down_bf,) = in_projection(n, w_in_t, i, starts_b, jnp.ones((1, n_b), F32), F32,
                                          cast=((expert_w_down, CAST_ROWS),))
        z_a3 = z_a.reshape(bsz, seq, -1)
        z_b3 = z_b.reshape(bsz, seq, -1)

        f_bias = _pad_cols(fox_f_bias[i][None, :].astype(F32), LANES)
        eq, ek = forget_prep(z_f.reshape(bsz, seq, LANES), f_bias)
        br_fox = fox_attention(z_a3, eq, ek).reshape(t, BRANCH_WIDTH)
        br_pool = multiscale_pool(z_b3, pool_w_bf, i, pool_scale[i]).reshape(t, BRANCH_WIDTH)
        br_sgu = spatial_gating(z_a, sgu_u_col, sgu_norm_w[i], sgu_w[i], sgu_b[i])
        br_hgrn = hgrn2(z_b3, BRANCH_WIDTH, lower_bounds[i], hgrn_norm_w[i]).reshape(t, BRANCH_WIDTH)

        merged = merge_branches((br_fox, br_pool, br_sgu, br_hgrn), branch_proj_bf, i, z_a, gate_col)
        h = matmul_residual(merged, w_out_bf, i, h)

        xn, meta, probs, counts = moe_router(h, ffn_norm_w[i], w_r, b_r)
        buf_tok, block_e, n_used, pos0, pos1 = _dispatch_plan(meta, counts)
        y = expert_mlp(xn, buf_tok, block_e, n_used, w_gate_bf, w_up_bf, w_down_bf)
        h, n_ple = moe_combine(y, pos0, pos1, probs, h, ple_norm_w[i])

        h = ple_update(n_ple, ple_gate_bf, p_bf, ple_proj_bf, i, h)
    return rms_norm(h, final_norm_w, F32).reshape(bsz, seq, d)
```
